```python
import math
import jax, jax.numpy as jnp
from jax import lax
import numpy as np

D_MODEL = 1024
BATCH = 2
SEQ = 8192
DEPTH = 2

GM_WIDTH = D_MODEL // 2
GM_GROUPS = 8
GM_CHUNK = 128
NSA_HEADS = 8
NSA_KV_GROUPS = 2
NSA_HPG = NSA_HEADS // NSA_KV_GROUPS
HEAD_DIM = 64
NSA_WIDTH = NSA_HEADS * HEAD_DIM
KV_WIDTH = NSA_KV_GROUPS * HEAD_DIM
CMP_BLOCK = 32
CMP_STRIDE = 16
CMP_HIDDEN = 256
SEL_BLOCK = 64
N_SEL = 16
WINDOW = 512
Q_BLOCK = 128
N_NSA_BRANCHES = 3
N_MERGE_BRANCHES = 2
FORCED_SCORE = 1e6
NUM_BUCKETS = 32
MAX_DISTANCE = 128
D_FF = 2816
CONV_WIDTH = 3
ALPHA = (2.0 * DEPTH) ** 0.25
BETA = (8.0 * DEPTH) ** -0.25
LN_EPS = 1e-5
NEG_INF = -1e30

IN_SEGMENTS = (GM_WIDTH, GM_WIDTH, NSA_WIDTH, KV_WIDTH, KV_WIDTH, KV_WIDTH, KV_WIDTH, KV_WIDTH, KV_WIDTH,
               NSA_HEADS * N_NSA_BRANCHES, D_MODEL * N_MERGE_BRANCHES)
IN_SCALES = (BETA, BETA, 1.0, 1.0, BETA, 1.0, BETA, 1.0, BETA, 1.0, 1.0)
D_IN = sum(IN_SEGMENTS)

kernel_name = 'hybrid_gmlp_nsa_convffn_deepnorm'


def layer_norm(x, g, b):
    xf = x.astype(jnp.float32)
    mu = xf.mean(-1, keepdims=True)
    var = jnp.square(xf - mu).mean(-1, keepdims=True)
    return ((xf - mu) * lax.rsqrt(var + LN_EPS) * g + b).astype(x.dtype)


def split_columns(z, sizes):
    offsets = np.cumsum(sizes)[:-1].tolist()
    return jnp.split(z, offsets, axis=-1)


def t5_bucket(dist):
    n = jnp.maximum(dist, 0)
    max_exact = NUM_BUCKETS // 2
    log_ratio = jnp.log(jnp.maximum(n, 1).astype(jnp.float32) / max_exact) / math.log(MAX_DISTANCE / max_exact)
    large = jnp.minimum(max_exact + (log_ratio * (NUM_BUCKETS - max_exact)).astype(jnp.int32), NUM_BUCKETS - 1)
    return jnp.where(n < max_exact, n, large)


def shared_bias(rel_bias, dist):
    b = jnp.moveaxis(rel_bias[t5_bucket(dist)], -1, 0)
    return b.reshape(NSA_KV_GROUPS, NSA_HPG, *dist.shape)


def masked_softmax(logits, valid):
    logits = jnp.where(valid, logits.astype(jnp.float32), NEG_INF)
    return jax.nn.softmax(logits, axis=-1) * valid


def selection_overlap(seq):
    n_cmp = seq // CMP_STRIDE - 1
    n_blk = seq // SEL_BLOCK
    cs = np.arange(n_cmp)[:, None] * CMP_STRIDE
    ss = np.arange(n_blk)[None, :] * SEL_BLOCK
    ov = np.minimum(cs + CMP_BLOCK, ss + SEL_BLOCK) - np.maximum(cs, ss)
    return jnp.asarray(np.maximum(ov, 0) / CMP_BLOCK, dtype=jnp.float32)


def gmlp_spatial_gating(u, v, ln_g, ln_b, w_s, b_s):
    B, S, _ = u.shape
    gd = GM_WIDTH // GM_GROUPS
    v = layer_norm(v, ln_g, ln_b).reshape(B, S // GM_CHUNK, GM_CHUNK, GM_GROUPS, gd)
    causal = jnp.tril(jnp.ones((GM_CHUNK, GM_CHUNK), w_s.dtype))
    sv = jnp.einsum('gts,bnsgc->bntgc', w_s * causal, v) + b_s.T[None, None, :, :, None]
    return u * sv.reshape(B, S, GM_WIDTH)


def compress(kv, pos, w1, b1, w2):
    B, G, S, dh = kv.shape
    chunks = kv.reshape(B, G, S // CMP_STRIDE, CMP_STRIDE, dh)
    blocks = jnp.concatenate([chunks[:, :, :-1], chunks[:, :, 1:]], axis=3)
    blocks = (blocks + pos).reshape(B, G, -1, CMP_BLOCK * dh)
    return jax.nn.silu(blocks @ w1 + b1) @ w2


def native_sparse_attention(q, kc, vc, ks, vs, kw, vw, gate, rel_bias):
    B, G, HPG, S, dh = q.shape
    n_cmp = kc.shape[2]
    n_blk = S // SEL_BLOCK
    n_sel = min(N_SEL, n_blk)
    scale = dh ** -0.5
    overlap = selection_overlap(S)
    cmp_end = jnp.arange(n_cmp) * CMP_STRIDE + CMP_BLOCK - 1
    ks_blocks = ks.reshape(B, G, n_blk, SEL_BLOCK, dh)
    vs_blocks = vs.reshape(B, G, n_blk, SEL_BLOCK, dh)
    kw_pad = jnp.pad(kw, ((0, 0), (0, 0), (WINDOW, 0), (0, 0)))
    vw_pad = jnp.pad(vw, ((0, 0), (0, 0), (WINDOW, 0), (0, 0)))
    bias_gh = rel_bias.reshape(NUM_BUCKETS, G, HPG)
    b_ix = jnp.arange(B)[:, None, None, None]
    g_ix = jnp.arange(G)[None, :, None, None]
    blk = jnp.arange(n_blk)
    group_bias = jax.vmap(lambda d, tab: tab[t5_bucket(d)], in_axes=(1, 1), out_axes=1)

    def step(i):
        qs = i * Q_BLOCK
        t = qs + jnp.arange(Q_BLOCK)
        qb = lax.dynamic_slice_in_dim(q, qs, Q_BLOCK, axis=3) * scale
        gb = jax.nn.sigmoid(lax.dynamic_slice_in_dim(gate, qs, Q_BLOCK, axis=3))

        dist_c = t[:, None] - cmp_end[None, :]
        logits = jnp.einsum('bghqd,bgkd->bghqk', qb, kc) + shared_bias(rel_bias, dist_c)
        p_c = masked_softmax(logits, dist_c >= 0)
        o_c = jnp.einsum('bghqk,bgkd->bghqd', p_c.astype(vc.dtype), vc)

        imp = jnp.einsum('bghqk,kj->bgqj', p_c, overlap)
        cur = t[:, None] // SEL_BLOCK
        forced = (blk == 0) | (blk == cur) | (blk == cur - 1)
        imp = jnp.where(blk > cur, NEG_INF, jnp.where(forced, FORCED_SCORE, imp))
        _, idx = lax.top_k(imp, n_sel)

        k_sel = ks_blocks[b_ix, g_ix, idx].reshape(B, G, Q_BLOCK, n_sel * SEL_BLOCK, dh)
        v_sel = vs_blocks[b_ix, g_ix, idx].reshape(B, G, Q_BLOCK, n_sel * SEL_BLOCK, dh)
        tok = (idx[..., None] * SEL_BLOCK + jnp.arange(SEL_BLOCK)).reshape(B, G, Q_BLOCK, -1)
        dist_s = t[:, None] - tok
        bias_s = jnp.moveaxis(group_bias(dist_s, bias_gh), -1, 2)
        logits = jnp.einsum('bghqd,bgqkd->bghqk', qb, k_sel) + bias_s
        p_s = masked_softmax(logits, (dist_s >= 0)[:, :, None])
        o_s = jnp.einsum('bghqk,bgqkd->bghqd', p_s.astype(v_sel.dtype), v_sel)

        kwb = lax.dynamic_slice_in_dim(kw_pad, qs, WINDOW + Q_BLOCK, axis=2)
        vwb = lax.dynamic_slice_in_dim(vw_pad, qs, WINDOW + Q_BLOCK, axis=2)
        kpos = qs - WINDOW + jnp.arange(WINDOW + Q_BLOCK)
        dist_w = t[:, None] - kpos[None, :]
        valid_w = (dist_w >= 0) & (dist_w < WINDOW) & (kpos >= 0)[None, :]
        logits = jnp.einsum('bghqd,bgkd->bghqk', qb, kwb) + shared_bias(rel_bias, dist_w)
        p_w = masked_softmax(logits, valid_w)
        o_w = jnp.einsum('bghqk,bgkd->bghqd', p_w.astype(vwb.dtype), vwb)

        return gb[..., 0:1] * o_c + gb[..., 1:2] * o_s + gb[..., 2:3] * o_w

    out = lax.map(step, jnp.arange(S // Q_BLOCK))
    return out.transpose(1, 0, 4, 2, 3, 5).reshape(B, S, G * HPG * dh)


def token_mixers(h, w_in, b_in, gm_ln_g, gm_ln_b, gm_ws, gm_bs, cmp_pos, cmp_w1, cmp_b1, cmp_w2,
                 rel_bias, w_proj_a, w_proj_b, w_out):
    B, S, _ = h.shape
    proj = h @ w_in + b_in
    u, v, q, kc, vc, ks, vs, kw, vw, ng, mg = split_columns(proj, IN_SEGMENTS)

    a = gmlp_spatial_gating(jax.nn.gelu(u), jax.nn.gelu(v), gm_ln_g, gm_ln_b, gm_ws, gm_bs)

    def kv_heads(z):
        return z.reshape(B, S, NSA_KV_GROUPS, HEAD_DIM).transpose(0, 2, 1, 3)
    qh = q.reshape(B, S, NSA_KV_GROUPS, NSA_HPG, HEAD_DIM).transpose(0, 2, 3, 1, 4)
    kc = compress(kv_heads(kc), cmp_pos[0], cmp_w1[0], cmp_b1[0], cmp_w2[0])
    vc = compress(kv_heads(vc), cmp_pos[1], cmp_w1[1], cmp_b1[1], cmp_w2[1])
    gate = ng.reshape(B, S, NSA_KV_GROUPS, NSA_HPG, N_NSA_BRANCHES).transpose(0, 2, 3, 1, 4)
    o = native_sparse_attention(qh, kc, vc, kv_heads(ks), kv_heads(vs), kv_heads(kw), kv_heads(vw), gate, rel_bias)

    g_a, g_b = jnp.split(mg, N_MERGE_BRANCHES, axis=-1)
    y = jax.nn.sigmoid(g_a) * (a @ w_proj_a) + jax.nn.sigmoid(g_b) * (o @ w_proj_b)
    return y @ w_out


def conv_ffn(h, w_up, conv_w, conv_b, w_down):
    S = h.shape[1]
    up = h @ w_up
    pad = jnp.pad(up, ((0, 0), (CONV_WIDTH - 1, 0), (0, 0)))
    c = conv_b + conv_w[0] * pad[:, 0:S] + conv_w[1] * pad[:, 1:S + 1] + conv_w[2] * pad[:, 2:S + 2]
    g, val = jnp.split(c, 2, axis=-1)
    return (jax.nn.silu(g) * val) @ w_down


def setup_inputs(seed: int = 0) -> dict:
    key = jax.random.key(seed)
    ks = jax.random.split(key, 24)
    L = DEPTH

    def nrm(k, shape, scale):
        return jax.random.normal(k, shape, jnp.float32) * scale

    seg_keys = jax.random.split(ks[1], len(IN_SEGMENTS))
    w_in = jnp.concatenate([nrm(sk, (L, D_MODEL, n), s * D_MODEL ** -0.5)
                            for sk, n, s in zip(seg_keys, IN_SEGMENTS, IN_SCALES)], axis=-1)
    return {
        'x': nrm(ks[0], (BATCH, SEQ, D_MODEL), 1.0),
        'w_in': w_in,
        'b_in': nrm(ks[2], (L, D_IN), 0.02),
        'gm_ln_g': 1.0 + nrm(ks[3], (L, GM_WIDTH), 0.05),
        'gm_ln_b': nrm(ks[4], (L, GM_WIDTH), 0.02),
        'gm_ws': nrm(ks[5], (L, GM_GROUPS, GM_CHUNK, GM_CHUNK), GM_CHUNK ** -0.5),
        'gm_bs': 1.0 + nrm(ks[6], (L, GM_GROUPS, GM_CHUNK), 0.05),
        'cmp_pos': nrm(ks[7], (L, 2, CMP_BLOCK, HEAD_DIM), 0.5),
        'cmp_w1': nrm(ks[8], (L, 2, CMP_BLOCK * HEAD_DIM, CMP_HIDDEN), (CMP_BLOCK * HEAD_DIM) ** -0.5),
        'cmp_b1': nrm(ks[9], (L, 2, CMP_HIDDEN), 0.02),
        'cmp_w2': nrm(ks[10], (L, 2, CMP_HIDDEN, HEAD_DIM), CMP_HIDDEN ** -0.5),
        'rel_bias': nrm(ks[11], (NUM_BUCKETS, NSA_HEADS), 0.5),
        'w_proj_a': nrm(ks[12], (L, GM_WIDTH, D_MODEL), BETA * GM_WIDTH ** -0.5),
        'w_proj_b': nrm(ks[13], (L, NSA_WIDTH, D_MODEL), BETA * NSA_WIDTH ** -0.5),
        'w_out': nrm(ks[14], (L, D_MODEL, D_MODEL), BETA * D_MODEL ** -0.5),
        'ln1_g': 1.0 + nrm(ks[15], (L, D_MODEL), 0.05),
        'ln1_b': nrm(ks[16], (L, D_MODEL), 0.02),
        'ffn_w_up': nrm(ks[17], (L, D_MODEL, 2 * D_FF), BETA * D_MODEL ** -0.5),
        'ffn_conv_w': nrm(ks[18], (L, CONV_WIDTH, 2 * D_FF), CONV_WIDTH ** -0.5),
        'ffn_conv_b': nrm(ks[19], (L, 2 * D_FF), 0.02),
        'ffn_w_down': nrm(ks[20], (L, D_FF, D_MODEL), BETA * D_FF ** -0.5),
        'ln2_g': 1.0 + nrm(ks[21], (L, D_MODEL), 0.05),
        'ln2_b': nrm(ks[22], (L, D_MODEL), 0.02),
    }


def reference(x, w_in, b_in, gm_ln_g, gm_ln_b, gm_ws, gm_bs, cmp_pos, cmp_w1, cmp_b1, cmp_w2, rel_bias,
              w_proj_a, w_proj_b, w_out, ln1_g, ln1_b, ffn_w_up, ffn_conv_w, ffn_conv_b, ffn_w_down,
              ln2_g, ln2_b):
    h = x
    for l in range(DEPTH):
        m = token_mixers(h, w_in[l], b_in[l], gm_ln_g[l], gm_ln_b[l], gm_ws[l], gm_bs[l], cmp_pos[l],
                         cmp_w1[l], cmp_b1[l], cmp_w2[l], rel_bias, w_proj_a[l], w_proj_b[l], w_out[l])
        h = layer_norm(ALPHA * h + m, ln1_g[l], ln1_b[l])
        f = conv_ffn(h, ffn_w_up[l], ffn_conv_w[l], ffn_conv_b[l], ffn_w_down[l])
        h = layer_norm(ALPHA * h + f, ln2_g[l], ln2_b[l])
    return h
```

```python
import functools
import math

import jax
import jax.numpy as jnp
import numpy as np
from jax import lax
from jax.experimental import pallas as pl
from jax.experimental.pallas import tpu as pltpu

D_MODEL = 1024
DEPTH = 2
GM_WIDTH = D_MODEL // 2
GM_GROUPS = 8
GM_CHUNK = 128
NSA_HEADS = 8
NSA_KV_GROUPS = 2
NSA_HPG = NSA_HEADS // NSA_KV_GROUPS
HEAD_DIM = 64
NSA_WIDTH = NSA_HEADS * HEAD_DIM
KV_WIDTH = NSA_KV_GROUPS * HEAD_DIM
CMP_BLOCK = 32
CMP_STRIDE = 16
CMP_HIDDEN = 256
SEL_BLOCK = 64
N_SEL = 16
WINDOW = 512
Q_BLOCK = 128
NUM_BUCKETS = 32
MAX_DISTANCE = 128
D_FF = 2816
CONV_WIDTH = 3
ALPHA = (2.0 * DEPTH) ** 0.25
LN_EPS = 1e-5
FORCED_SCORE = 1e6
NEG_INF = -1e30
M_INIT = -1e29

LANE = 128
QCOLS = NSA_HPG * Q_BLOCK
KEY_TILE = 128
FAR_TILES = 4
BAND_TILES = WINDOW // KEY_TILE + 1
PAD_TILES = BAND_TILES - 1
CMP_PAD = 8
CMP_ROWS = 640
CMP_BAND = 16
V_ROWS = 80
FF_CHUNK = 256
N_FF = D_FF // FF_CHUNK
ROW_TILE = 256
VMEM_LIMIT = 48 * 1024 * 1024

F32 = jnp.float32
BF16 = jnp.bfloat16
NT_DIMS = (((1,), (1,)), ((), ()))


def _dot(a, b):
    return jnp.dot(a, b, preferred_element_type=F32)


def _dot_nt(a, b):
    return lax.dot_general(a, b, NT_DIMS, preferred_element_type=F32)


def _layer_norm(x, g, b):
    mu = jnp.mean(x, axis=-1, keepdims=True)
    xc = x - mu
    var = jnp.mean(xc * xc, axis=-1, keepdims=True)
    return xc * lax.rsqrt(var + LN_EPS) * g + b


def _gelu_tanh(x):
    c = math.sqrt(2.0 / math.pi)
    return x * (0.5 * (1.0 + jnp.tanh(c * (x + 0.044715 * (x * x * x)))))


def _sigmoid(x):
    return 1.0 / (1.0 + jnp.exp(-x))


def _params(n_axes):
    return pltpu.CompilerParams(dimension_semantics=("arbitrary",) * n_axes,
                                vmem_limit_bytes=VMEM_LIMIT)


def _full(shape):
    n = len(shape)
    return pl.BlockSpec(shape, lambda *_: (0,) * n)


def _proj_kernel(h_ref, wuv_ref, buv_ref, lng_ref, lnb_ref, wc_ref, bsb_ref, wq_ref, bq_ref,
                 wr_ref, br_ref,
                 a_ref, q_ref, kc_ref, vc_ref, ks_ref, vs_ref, kw_ref, vw_ref, ng_ref):
    tm = h_ref.shape[1]
    hb = h_ref[0].astype(BF16)

    uv = _dot(hb, wuv_ref[...]) + buv_ref[...]
    u = _gelu_tanh(uv[:, :GM_WIDTH])
    v = _layer_norm(_gelu_tanh(uv[:, GM_WIDTH:]), lng_ref[...], lnb_ref[...]).astype(BF16)
    lane = lax.broadcasted_iota(jnp.int32, (GM_CHUNK, LANE), 1)
    group_w = GM_WIDTH // GM_GROUPS
    for ch in range(tm // GM_CHUNK):
        rows = slice(ch * GM_CHUNK, (ch + 1) * GM_CHUNK)
        for pr in range(GM_GROUPS // 2):
            cols = slice(pr * LANE, (pr + 1) * LANE)
            x = v[rows, cols]
            sv = jnp.where(lane < group_w, _dot(wc_ref[2 * pr], x), _dot(wc_ref[2 * pr + 1], x))
            a_ref[0, rows, cols] = (u[rows, cols] * (sv + bsb_ref[pr])).astype(BF16)

    qq = (_dot(hb, wq_ref[...]) + bq_ref[...]) * (HEAD_DIM ** -0.5)
    for hh in range(NSA_HEADS):
        q_ref[0, hh] = qq[:, hh * LANE:(hh + 1) * LANE].astype(BF16)

    r = _dot(hb, wr_ref[...]) + br_ref[...]
    kc_ref[0] = r[:, 0:128]
    vc_ref[0] = r[:, 128:256]
    ks_ref[0, 0] = r[:, 256:384].astype(BF16)
    ks_ref[0, 1] = r[:, 384:512].astype(BF16)
    vs_ref[0] = r[:, 512:640].astype(BF16)
    kw_ref[0, 0] = r[:, 640:768].astype(BF16)
    kw_ref[0, 1] = r[:, 768:896].astype(BF16)
    vw_ref[0] = r[:, 896:1024].astype(BF16)
    ng_ref[0] = r[:, 1024:1152]


def _proj_call(h, p):
    B, S, _ = h.shape
    tm = ROW_TILE
    G = NSA_KV_GROUPS
    row = lambda w: pl.BlockSpec((1, tm, w), lambda b, i: (b, i, 0))
    grp = pl.BlockSpec((1, G, tm, LANE), lambda b, i: (b, 0, i, 0))
    out_shape = (
        jax.ShapeDtypeStruct((B, S, GM_WIDTH), BF16),
        jax.ShapeDtypeStruct((B, NSA_HEADS, S, LANE), BF16),
        jax.ShapeDtypeStruct((B, S, LANE), F32),
        jax.ShapeDtypeStruct((B, S, LANE), F32),
        jax.ShapeDtypeStruct((B, G, S, LANE), BF16),
        jax.ShapeDtypeStruct((B, S, LANE), BF16),
        jax.ShapeDtypeStruct((B, G, S, LANE), BF16),
        jax.ShapeDtypeStruct((B, S, LANE), BF16),
        jax.ShapeDtypeStruct((B, S, LANE), F32),
    )
    out_specs = (row(GM_WIDTH),
                 pl.BlockSpec((1, NSA_HEADS, tm, LANE), lambda b, i: (b, 0, i, 0)),
                 row(LANE), row(LANE), grp, row(LANE), grp, row(LANE), row(LANE))
    in_arrays = (h, p["w_uv"], p["b_uv"], p["gm_ln_g"], p["gm_ln_b"], p["gm_wc"], p["gm_bsb"],
                 p["w_q"], p["b_q"], p["w_r"], p["b_r"])
    in_specs = [row(D_MODEL)] + [_full(a.shape) for a in in_arrays[1:]]
    return pl.pallas_call(
        _proj_kernel, grid=(B, S // tm), in_specs=in_specs, out_specs=out_specs,
        out_shape=out_shape, compiler_params=_params(2), name="proj_gmlp")(*in_arrays)


def _compress_kernel(x_ref, plo_ref, phi_ref, w1lo_ref, w1hi_ref, b1_ref, w2_ref, o_ref):
    x = x_ref[0, 0, 0]
    n = x.shape[0]
    a = _dot((x + plo_ref[0]).astype(BF16), w1lo_ref[0])
    b = _dot((x + phi_ref[0]).astype(BF16), w1hi_ref[0])
    hid = a + pltpu.roll(b, n - 1, 0) + b1_ref[0]
    act = (hid * _sigmoid(hid)).astype(BF16)
    o_ref[0, 0, 0] = _dot(act, w2_ref[0])


def _compress_call(x, p):
    B, G, _, n, w = x.shape
    per_kv = lambda a: pl.BlockSpec((1,) + a.shape[1:], lambda b, g, k: (k,) + (0,) * (a.ndim - 1))
    ws = (p["cmp_pos_lo"], p["cmp_pos_hi"], p["cmp_w1_lo"], p["cmp_w1_hi"], p["cmp_b1"], p["cmp_w2"])
    return pl.pallas_call(
        _compress_kernel, grid=(B, G, 2),
        in_specs=[pl.BlockSpec((1, 1, 1, n, w), lambda b, g, k: (b, g, k, 0, 0))] + [per_kv(a) for a in ws],
        out_specs=pl.BlockSpec((1, 1, 1, n, HEAD_DIM), lambda b, g, k: (b, g, k, 0, 0)),
        out_shape=jax.ShapeDtypeStruct((B, G, 2, n, HEAD_DIM), F32),
        compiler_params=_params(3), name="kv_compress")(x, *ws)


def _nsa_kernel(q_ref, gl_ref, kc_ref, vct_ref, ovt_ref, bct_ref, ks_ref, vst_ref, kw_ref, vwt_ref,
                bt0_ref, bt1_ref, o_ref, sc_ref, lhsf_ref, lhst_ref, m_ref, acc_ref):
    i = pl.program_id(2)
    qa = q_ref[0].reshape(QCOLS, LANE)

    sc_ref[...] = _dot_nt(kc_ref[0, 0], qa)
    band = pl.multiple_of(8 * i, 8)
    sc_ref[pl.ds(band, CMP_BAND), :] += bct_ref[0]
    s = sc_ref[...]
    key = lax.broadcasted_iota(jnp.int32, s.shape, 0)
    s = jnp.where(key >= 8 * i + CMP_BAND, NEG_INF, s)
    m = jnp.maximum(jnp.max(s, axis=0, keepdims=True), M_INIT)
    p = jnp.exp(s - m)
    l = jnp.sum(p, axis=0, keepdims=True)
    r = jnp.where(l > 0.0, 1.0 / l, 0.0)
    oc = _dot(vct_ref[0, 0], p.astype(BF16))[0:HEAD_DIM] * r

    pn = p * r
    psum = pn[:, 0:Q_BLOCK]
    for h in range(1, NSA_HPG):
        psum = psum + pn[:, h * Q_BLOCK:(h + 1) * Q_BLOCK]
    p_hi = psum.astype(BF16)
    p_lo = (psum - p_hi.astype(F32)).astype(BF16)
    imp = _dot(ovt_ref[...], p_hi) + _dot(ovt_ref[...], p_lo)

    n_blk = imp.shape[0]
    blk = lax.broadcasted_iota(jnp.int32, (n_blk, Q_BLOCK), 0)
    qi = lax.broadcasted_iota(jnp.int32, (n_blk, Q_BLOCK), 1)
    cur = (Q_BLOCK // SEL_BLOCK) * i + (qi >= SEL_BLOCK).astype(jnp.int32)
    forced = (blk == 0) | (blk == cur) | (blk == cur - 1)
    val = jnp.where(blk > cur, NEG_INF, jnp.where(forced, FORCED_SCORE, imp))
    blk_f = blk.astype(F32)

    def pick(_, carry):
        val, sel = carry
        mx = jnp.max(val, axis=0, keepdims=True)
        first = jnp.min(jnp.where(val == mx, blk_f, 1e9), axis=0, keepdims=True)
        hit = blk_f == first
        return jnp.where(hit, -3e38, val), jnp.where(hit, 1.0, sel)

    _, sel = lax.fori_loop(0, min(N_SEL, n_blk), pick, (val, jnp.zeros_like(val)))

    n_far = jnp.maximum(i - 1, 0) // FAR_TILES
    neg_far = jnp.where((sel > 0.5) & (blk <= cur), 0.0, NEG_INF)
    neg_tail = jnp.where(blk >= n_far * (FAR_TILES * KEY_TILE // SEL_BLOCK), neg_far, NEG_INF)
    eye = (blk == qi).astype(BF16)
    lhs_neg_far = _dot_nt(eye, neg_far.astype(BF16)).astype(BF16)
    lhs_neg_tail = _dot_nt(eye, neg_tail.astype(BF16)).astype(BF16)
    for h in range(NSA_HPG):
        rows = slice(h * Q_BLOCK, (h + 1) * Q_BLOCK)
        lhsf_ref[rows, 0:LANE] = lhs_neg_far
        lhsf_ref[rows, LANE:2 * LANE] = qa[rows]
        lhst_ref[rows, 0:LANE] = lhs_neg_tail
        lhst_ref[rows, LANE:2 * LANE] = qa[rows]

    m_ref[...] = jnp.full(m_ref.shape, M_INIT, F32)
    acc_ref[...] = jnp.zeros(acc_ref.shape, F32)
    far_keys = FAR_TILES * KEY_TILE

    def far_step(c, carry):
        r0 = pl.multiple_of(PAD_TILES * KEY_TILE + far_keys * c, far_keys)
        s = _dot_nt(ks_ref[0, 0, pl.ds(r0, far_keys), :], lhsf_ref[...])
        m_old = m_ref[...]
        m_new = jnp.maximum(m_old, jnp.max(s, axis=0, keepdims=True))
        p = jnp.exp(s - m_new).astype(BF16)
        pv = jnp.zeros(acc_ref.shape, F32)
        for t in range(FAR_TILES):
            pv = pv + _dot(vst_ref[0, 0, PAD_TILES + FAR_TILES * c + t], p[t * KEY_TILE:(t + 1) * KEY_TILE])
        acc_ref[...] = jnp.exp(m_old - m_new) * acc_ref[...] + pv
        m_ref[...] = m_new
        return carry

    lax.fori_loop(0, n_far, far_step, 0)

    def band_softmax(k_ref, vt_ref, lhs, extra, m_old, acc_old):
        t0 = pl.multiple_of(KEY_TILE * i, KEY_TILE)
        tiles = []
        for t in range(BAND_TILES):
            s = _dot_nt(k_ref[0, 0, pl.ds(t0 + KEY_TILE * t, KEY_TILE), :], lhs)
            tiles.append(s if extra[t] is None else s + extra[t])
        m_new = m_old
        for s in tiles:
            m_new = jnp.maximum(m_new, jnp.max(s, axis=0, keepdims=True))
        acc = jnp.exp(m_old - m_new) * acc_old
        for t, s in enumerate(tiles):
            acc = acc + _dot(vt_ref[0, 0, i + t], jnp.exp(s - m_new).astype(BF16))
        return acc[0:HEAD_DIM] / acc[HEAD_DIM:HEAD_DIM + 1]

    bt0 = bt0_ref[0]
    bt1 = bt1_ref[0]
    extra = [None] * (BAND_TILES - 2) + [bt1, bt0]
    os_ = band_softmax(ks_ref, vst_ref, lhst_ref[...], extra, m_ref[...], acc_ref[...])

    kk = lax.broadcasted_iota(jnp.int32, (KEY_TILE, QCOLS), 0)
    qq = lax.broadcasted_iota(jnp.int32, (KEY_TILE, QCOLS), 1) & (Q_BLOCK - 1)
    edge = jnp.where(kk > qq, 0.0, NEG_INF)
    extra = [edge] + [None] * (BAND_TILES - 3) + [bt1, bt0]
    ow = band_softmax(kw_ref, vwt_ref, qa, extra, jnp.full(m_ref.shape, M_INIT, F32),
                      jnp.zeros(acc_ref.shape, F32))

    g = _sigmoid(gl_ref[0, 0, 0])
    o_ref[0, 0, 0] = g[0:1] * oc + g[1:2] * os_ + g[2:3] * ow


def _nsa_call(q, gl, kc, vct, ovt, bct, ks, vst, kw, vwt, bt0, bt1):
    B, _, S, _ = q.shape
    G = NSA_KV_GROUPS
    nq = S // Q_BLOCK
    per_bg = lambda a: pl.BlockSpec((1, 1) + a.shape[2:], lambda b, g, i: (b, g) + (0,) * (a.ndim - 2))
    per_g = lambda a: pl.BlockSpec((1,) + a.shape[1:], lambda b, g, i: (g,) + (0,) * (a.ndim - 1))
    in_specs = [
        pl.BlockSpec((1, NSA_HPG, Q_BLOCK, LANE), lambda b, g, i: (b, g, i, 0)),
        pl.BlockSpec((1, 1, 1, 8, QCOLS), lambda b, g, i: (b, g, i, 0, 0)),
        per_bg(kc), per_bg(vct), _full(ovt.shape), per_g(bct),
        per_bg(ks), per_bg(vst), per_bg(kw), per_bg(vwt), per_g(bt0), per_g(bt1),
    ]
    scratch = [
        pltpu.VMEM((CMP_ROWS, QCOLS), F32),
        pltpu.VMEM((QCOLS, 2 * LANE), BF16),
        pltpu.VMEM((QCOLS, 2 * LANE), BF16),
        pltpu.VMEM((1, QCOLS), F32),
        pltpu.VMEM((V_ROWS, QCOLS), F32),
    ]
    return pl.pallas_call(
        _nsa_kernel, grid=(B, G, nq), in_specs=in_specs,
        out_specs=pl.BlockSpec((1, 1, 1, HEAD_DIM, QCOLS), lambda b, g, i: (b, g, i, 0, 0)),
        out_shape=jax.ShapeDtypeStruct((B, G, nq, HEAD_DIM, QCOLS), F32),
        scratch_shapes=scratch, compiler_params=_params(3), name="sparse_attention",
    )(q, gl, kc, vct, ovt, bct, ks, vst, kw, vwt, bt0, bt1)


def _merge_kernel(h_ref, a_ref, o_ref, wmg_ref, bmg_ref, wa_ref, wb_ref, wout_ref, lng_ref, lnb_ref, out_ref):
    h = h_ref[0]
    mg = _dot(h.astype(BF16), wmg_ref[...]) + bmg_ref[...]
    y = (_sigmoid(mg[:, :D_MODEL]) * _dot(a_ref[0], wa_ref[...])
         + _sigmoid(mg[:, D_MODEL:]) * _dot(o_ref[0], wb_ref[...]))
    m = _dot(y.astype(BF16), wout_ref[...])
    out_ref[0] = _layer_norm(ALPHA * h + m, lng_ref[...], lnb_ref[...])


def _merge_call(h, a, o, p):
    B, S, _ = h.shape
    tm = ROW_TILE
    row = lambda w: pl.BlockSpec((1, tm, w), lambda b, i: (b, i, 0))
    ws = (p["w_mg"], p["b_mg"], p["w_proj_a"], p["w_proj_b"], p["w_out"], p["ln1_g"], p["ln1_b"])
    return pl.pallas_call(
        _merge_kernel, grid=(B, S // tm),
        in_specs=[row(D_MODEL), row(GM_WIDTH), row(NSA_WIDTH)] + [_full(a.shape) for a in ws],
        out_specs=row(D_MODEL), out_shape=jax.ShapeDtypeStruct((B, S, D_MODEL), F32),
        compiler_params=_params(2), name="merge_norm")(h, a, o, *ws)


def _ffn_kernel(h_ref, wup_ref, cw_ref, wdown_ref, lng_ref, lnb_ref, out_ref, u_ref, carry_ref):
    tm = h_ref.shape[1]

    @pl.when(pl.program_id(1) == 0)
    def _():
        carry_ref[...] = jnp.zeros(carry_ref.shape, F32)

    h = h_ref[0]
    hb = h.astype(BF16)

    def conv(idx):
        up = _dot(hb, wup_ref[idx])
        u_ref[0:8] = carry_ref[idx]
        u_ref[8:8 + tm] = up
        carry_ref[idx] = up[tm - 8:tm]
        cw = cw_ref[idx]
        return (cw[3:4] + cw[0:1] * u_ref[6:6 + tm] + cw[1:2] * u_ref[7:7 + tm] + cw[2:3] * up)

    f = jnp.zeros((tm, D_MODEL), F32)
    for c in range(N_FF):
        gate = conv(c)
        val = conv(N_FF + c)
        act = (gate * _sigmoid(gate) * val).astype(BF16)
        f = f + _dot(act, wdown_ref[c])
    out_ref[0] = _layer_norm(ALPHA * h + f, lng_ref[...], lnb_ref[...])


def _ffn_call(h, p):
    B, S, _ = h.shape
    tm = ROW_TILE
    row = pl.BlockSpec((1, tm, D_MODEL), lambda b, i: (b, i, 0))
    ws = (p["ffn_w_up"], p["ffn_cw"], p["ffn_w_down"], p["ln2_g"], p["ln2_b"])
    return pl.pallas_call(
        _ffn_kernel, grid=(B, S // tm),
        in_specs=[row] + [_full(a.shape) for a in ws],
        out_specs=row, out_shape=jax.ShapeDtypeStruct((B, S, D_MODEL), F32),
        scratch_shapes=[pltpu.VMEM((tm + 8, FF_CHUNK), F32), pltpu.VMEM((2 * N_FF, 8, FF_CHUNK), F32)],
        compiler_params=_params(2), name="conv_ffn_norm")(h, *ws)


def _t5_bucket(n):
    max_exact = NUM_BUCKETS // 2
    log_ratio = jnp.log(jnp.maximum(n, 1).astype(F32) / max_exact) / math.log(MAX_DISTANCE / max_exact)
    large = jnp.minimum(max_exact + (log_ratio * (NUM_BUCKETS - max_exact)).astype(jnp.int32), NUM_BUCKETS - 1)
    return jnp.where(n < max_exact, n, large)


def _bias_tiles(rel_bias):
    G, H = NSA_KV_GROUPS, NSA_HPG
    tab = rel_bias[_t5_bucket(jnp.arange(2 * KEY_TILE, dtype=jnp.int32))] - rel_bias[NUM_BUCKETS - 1]

    def tile(dist):
        t = jnp.where((dist >= 0)[..., None], tab[np.clip(dist, 0, 2 * KEY_TILE - 1)], NEG_INF)
        rows = dist.shape[0]
        return t.reshape(rows, Q_BLOCK, G, H).transpose(2, 0, 3, 1).reshape(G, rows, H * Q_BLOCK)

    ki = np.arange(KEY_TILE)[:, None]
    qi = np.arange(Q_BLOCK)[None, :]
    bt0 = tile(qi - ki)
    bt1 = tile(KEY_TILE + qi - ki)
    rr = np.arange(CMP_BAND)[:, None]
    bct = tile(qi + (CMP_STRIDE * CMP_PAD - CMP_BLOCK + 1) - CMP_STRIDE * rr)
    return bt0, bt1, bct


def _overlap_t(seq):
    n_cmp = seq // CMP_STRIDE - 1
    n_blk = seq // SEL_BLOCK
    cs = np.arange(n_cmp)[:, None] * CMP_STRIDE
    ss = np.arange(n_blk)[None, :] * SEL_BLOCK
    ov = np.maximum(np.minimum(cs + CMP_BLOCK, ss + SEL_BLOCK) - np.maximum(cs, ss), 0) / CMP_BLOCK
    out = np.zeros((LANE, CMP_ROWS), np.float32)
    out[:n_blk, CMP_PAD:CMP_PAD + n_cmp] = ov.T
    return jnp.asarray(out, BF16)


def _pad_cols(w, width):
    return jnp.pad(w, ((0, 0), (0, width - w.shape[1])))


def _layer_params(l, w_in, b_in, gm_ln_g, gm_ln_b, gm_ws, gm_bs, cmp_pos, cmp_w1, cmp_b1, cmp_w2,
                  w_proj_a, w_proj_b, w_out, ln1_g, ln1_b, ffn_w_up, ffn_conv_w, ffn_conv_b, ffn_w_down,
                  ln2_g, ln2_b):
    w, b = w_in[l], b_in[l]
    o_q = 2 * GM_WIDTH
    o_kc = o_q + NSA_WIDTH
    o_vc, o_ks, o_vs, o_kw, o_vw = (o_kc + KV_WIDTH * k for k in range(1, 6))
    o_ng = o_vw + KV_WIDTH
    o_mg = o_ng + NSA_HEADS * 3
    seg = lambda a, o, n: a[..., o:o + n]

    wq = jnp.pad(seg(w, o_q, NSA_WIDTH).reshape(D_MODEL, NSA_HEADS, HEAD_DIM),
                 ((0, 0), (0, 0), (0, LANE - HEAD_DIM))).reshape(D_MODEL, NSA_HEADS * LANE)
    bq = jnp.pad(seg(b, o_q, NSA_WIDTH).reshape(NSA_HEADS, HEAD_DIM), ((0, 0), (0, LANE - HEAD_DIM)))
    bq = bq.at[:, HEAD_DIM].set(NEG_INF / HEAD_DIM ** -0.5).reshape(1, NSA_HEADS * LANE)

    def k_groups(o):
        return [_pad_cols(seg(w, o + g * HEAD_DIM, HEAD_DIM), LANE) for g in range(NSA_KV_GROUPS)], \
               [jnp.pad(seg(b, o + g * HEAD_DIM, HEAD_DIM), (0, LANE - HEAD_DIM)) for g in range(NSA_KV_GROUPS)]

    wks, bks = k_groups(o_ks)
    wkw, bkw = k_groups(o_kw)
    n_ng = NSA_HEADS * 3
    w_r = jnp.concatenate([seg(w, o_kc, KV_WIDTH), seg(w, o_vc, KV_WIDTH), *wks, seg(w, o_vs, KV_WIDTH),
                           *wkw, seg(w, o_vw, KV_WIDTH), _pad_cols(seg(w, o_ng, n_ng), LANE)], axis=1)
    b_r = jnp.concatenate([seg(b, o_kc, KV_WIDTH), seg(b, o_vc, KV_WIDTH), *bks, seg(b, o_vs, KV_WIDTH),
                           *bkw, seg(b, o_vw, KV_WIDTH), jnp.pad(seg(b, o_ng, n_ng), (0, LANE - n_ng))])

    causal = jnp.tril(jnp.ones((GM_CHUNK, GM_CHUNK), F32))
    gw = GM_WIDTH // GM_GROUPS
    bs = gm_bs[l]
    bsb = jnp.concatenate([jnp.broadcast_to(bs[0::2, :, None], (GM_GROUPS // 2, GM_CHUNK, gw)),
                           jnp.broadcast_to(bs[1::2, :, None], (GM_GROUPS // 2, GM_CHUNK, gw))], axis=2)

    half = CMP_BLOCK // 2 * HEAD_DIM
    pos = cmp_pos[l].reshape(2, 1, CMP_BLOCK * HEAD_DIM)
    up = ffn_w_up[l].reshape(D_MODEL, 2 * N_FF, FF_CHUNK).transpose(1, 0, 2)
    cw = jnp.concatenate([ffn_conv_w[l], ffn_conv_b[l][None], jnp.zeros((8 - CONV_WIDTH - 1, 2 * D_FF), F32)])
    return {
        "w_uv": seg(w, 0, 2 * GM_WIDTH).astype(BF16), "b_uv": seg(b, 0, 2 * GM_WIDTH)[None],
        "gm_ln_g": gm_ln_g[l][None], "gm_ln_b": gm_ln_b[l][None],
        "gm_wc": (gm_ws[l] * causal).astype(BF16), "gm_bsb": bsb,
        "w_q": wq.astype(BF16), "b_q": bq, "w_r": w_r.astype(BF16), "b_r": b_r[None],
        "cmp_pos_lo": pos[:, :, :half], "cmp_pos_hi": pos[:, :, half:],
        "cmp_w1_lo": cmp_w1[l][:, :half].astype(BF16), "cmp_w1_hi": cmp_w1[l][:, half:].astype(BF16),
        "cmp_b1": cmp_b1[l][:, None], "cmp_w2": cmp_w2[l].astype(BF16),
        "w_mg": seg(w, o_mg, 2 * D_MODEL).astype(BF16), "b_mg": seg(b, o_mg, 2 * D_MODEL)[None],
        "w_proj_a": w_proj_a[l].astype(BF16), "w_proj_b": w_proj_b[l].astype(BF16),
        "w_out": w_out[l].astype(BF16), "ln1_g": ln1_g[l][None], "ln1_b": ln1_b[l][None],
        "ffn_w_up": up.astype(BF16),
        "ffn_cw": cw.reshape(8, 2 * N_FF, FF_CHUNK).transpose(1, 0, 2),
        "ffn_w_down": ffn_w_down[l].reshape(N_FF, FF_CHUNK, D_MODEL).astype(BF16),
        "ln2_g": ln2_g[l][None], "ln2_b": ln2_b[l][None],
    }


def _key_major(k, flag_col, front_rows):
    B, G, _, W = k.shape
    dummy = jnp.zeros((front_rows, W), k.dtype).at[:, flag_col].set(1)
    return jnp.concatenate([jnp.broadcast_to(dummy, (B, G, front_rows, W)), k], axis=2)


def _value_tiles(v):
    B, S, _ = v.shape
    G = NSA_KV_GROUPS
    nt = S // KEY_TILE
    vt = v.reshape(B, nt, KEY_TILE, G, HEAD_DIM).transpose(0, 3, 1, 4, 2)
    vt = jnp.concatenate([vt, jnp.ones((B, G, nt, 1, KEY_TILE), v.dtype),
                          jnp.zeros((B, G, nt, V_ROWS - HEAD_DIM - 1, KEY_TILE), v.dtype)], axis=3)
    return jnp.pad(vt, ((0, 0), (0, 0), (PAD_TILES, 0), (0, 0), (0, 0)))


def _token_mixers(h, p, consts):
    B, S, _ = h.shape
    G, H = NSA_KV_GROUPS, NSA_HPG
    nq = S // Q_BLOCK
    a, q, kc, vc, ks, vs, kw, vw, ng = _proj_call(h, p)

    n_chunk = S // CMP_STRIDE
    chunks = lambda z: z.reshape(B, n_chunk, CMP_STRIDE, G, HEAD_DIM).transpose(0, 3, 1, 2, 4).reshape(
        B, G, n_chunk, CMP_STRIDE * HEAD_DIM)
    cmp = _compress_call(jnp.stack([chunks(kc), chunks(vc)], axis=2), p)[:, :, :, :n_chunk - 1]
    back = CMP_ROWS - CMP_PAD - (n_chunk - 1)
    kcp = jnp.pad(cmp[:, :, 0], ((0, 0), (0, 0), (0, 0), (0, LANE - HEAD_DIM))).astype(BF16)
    kcp = _key_major(kcp, HEAD_DIM, CMP_PAD)
    kcp = jnp.concatenate([kcp, jnp.broadcast_to(kcp[:, :, :1], (B, G, back, LANE))], axis=2)
    vct = jnp.pad(cmp[:, :, 1].transpose(0, 1, 3, 2),
                  ((0, 0), (0, 0), (0, V_ROWS - HEAD_DIM), (CMP_PAD, back))).astype(BF16)

    ks_aug = _key_major(jnp.concatenate([jnp.broadcast_to(consts["onehot"], (B, G, S, LANE)), ks], axis=3),
                        LANE + HEAD_DIM, PAD_TILES * KEY_TILE)
    kw_aug = _key_major(kw, HEAD_DIM, PAD_TILES * KEY_TILE)
    gl = ng[:, :, :G * H * 3].reshape(B, nq, Q_BLOCK, G, H, 3).transpose(0, 3, 1, 5, 4, 2).reshape(
        B, G, nq, 3, H * Q_BLOCK)
    gl = jnp.pad(gl, ((0, 0), (0, 0), (0, 0), (0, 8 - 3), (0, 0)))

    ot = _nsa_call(q, gl, kcp, vct, consts["ovt"], consts["bct"], ks_aug, _value_tiles(vs),
                   kw_aug, _value_tiles(vw), consts["bt0"], consts["bt1"])
    o = ot.reshape(B, G, nq, HEAD_DIM, H, Q_BLOCK).transpose(0, 2, 5, 1, 4, 3).reshape(B, S, NSA_WIDTH)
    return _merge_call(h, a, o.astype(BF16), p)


def kernel(x, w_in, b_in, gm_ln_g, gm_ln_b, gm_ws, gm_bs, cmp_pos, cmp_w1, cmp_b1, cmp_w2, rel_bias,
           w_proj_a, w_proj_b, w_out, ln1_g, ln1_b, ffn_w_up, ffn_conv_w, ffn_conv_b, ffn_w_down,
           ln2_g, ln2_b):
    B, S, _ = x.shape
    assert S % (FAR_TILES * KEY_TILE) == 0 and N_SEL <= S // SEL_BLOCK <= LANE
    assert S // CMP_STRIDE - 1 + CMP_PAD < CMP_ROWS
    bt0, bt1, bct = _bias_tiles(rel_bias)
    onehot = np.zeros((S, LANE), np.float32)
    onehot[np.arange(S), np.arange(S) // SEL_BLOCK] = 1.0
    consts = {"bt0": bt0, "bt1": bt1, "bct": bct, "ovt": _overlap_t(S), "onehot": jnp.asarray(onehot, BF16)}
    h = x
    for l in range(DEPTH):
        p = _layer_params(l, w_in, b_in, gm_ln_g, gm_ln_b, gm_ws, gm_bs, cmp_pos, cmp_w1, cmp_b1, cmp_w2,
                          w_proj_a, w_proj_b, w_out, ln1_g, ln1_b, ffn_w_up, ffn_conv_w, ffn_conv_b,
                          ffn_w_down, ln2_g, ln2_b)
        h = _token_mixers(h, p, consts)
        h = _ffn_call(h, p)
    return h
```

```python
import functools
import math

import jax
import jax.numpy as jnp
import numpy as np
from jax import lax
from jax.experimental import pallas as pl
from jax.experimental.pallas import tpu as pltpu

D_MODEL = 1024
DEPTH = 2
GM_WIDTH = D_MODEL // 2
GM_GROUPS = 8
GM_CHUNK = 128
NSA_HEADS = 8
NSA_KV_GROUPS = 2
NSA_HPG = NSA_HEADS // NSA_KV_GROUPS
HEAD_DIM = 64
NSA_WIDTH = NSA_HEADS * HEAD_DIM
KV_WIDTH = NSA_KV_GROUPS * HEAD_DIM
CMP_BLOCK = 32
CMP_STRIDE = 16
CMP_HIDDEN = 256
SEL_BLOCK = 64
N_SEL = 16
WINDOW = 512
Q_BLOCK = 128
NUM_BUCKETS = 32
MAX_DISTANCE = 128
D_FF = 2816
CONV_WIDTH = 3
ALPHA = (2.0 * DEPTH) ** 0.25
LN_EPS = 1e-5
FORCED_SCORE = 1e6
NEG_INF = -1e30
M_INIT = -1e29
LOG2E = math.log2(math.e)
Q_SCALE = HEAD_DIM ** -0.5 * LOG2E

LANE = 128
QCOLS = NSA_HPG * Q_BLOCK
KEY_TILE = 128
FAR_TILES = 4
BAND_TILES = WINDOW // KEY_TILE + 1
PAD_TILES = BAND_TILES - 1
CMP_PAD = 8
CMP_ROWS = 640
CMP_BAND = 16
V_ROWS = 80
FF_CHUNK = 256
N_FF = D_FF // FF_CHUNK
ROW_TILE = 256
VMEM_LIMIT = 48 * 1024 * 1024

F32 = jnp.float32
BF16 = jnp.bfloat16
NT_DIMS = (((1,), (1,)), ((), ()))


def _dot(a, b):
    return jnp.dot(a, b, preferred_element_type=F32)


def _dot_nt(a, b):
    return lax.dot_general(a, b, NT_DIMS, preferred_element_type=F32)


def _layer_norm(x, g, b):
    mu = jnp.mean(x, axis=-1, keepdims=True)
    xc = x - mu
    var = jnp.mean(xc * xc, axis=-1, keepdims=True)
    return xc * lax.rsqrt(var + LN_EPS) * g + b


def _gelu_tanh(x):
    c = math.sqrt(2.0 / math.pi)
    return x * (0.5 * (1.0 + jnp.tanh(c * (x + 0.044715 * (x * x * x)))))


def _sigmoid(x):
    return 1.0 / (1.0 + jnp.exp(-x))


def _params(n_axes):
    return pltpu.CompilerParams(dimension_semantics=("arbitrary",) * n_axes,
                                vmem_limit_bytes=VMEM_LIMIT)


def _full(shape):
    n = len(shape)
    return pl.BlockSpec(shape, lambda *_: (0,) * n)


def _proj_kernel(h_ref, wuv_ref, buv_ref, lng_ref, lnb_ref, wc_ref, bsb_ref, wq_ref, bq_ref,
                 wr_ref, br_ref,
                 a_ref, q_ref, kc_ref, vc_ref, ks_ref, vs_ref, kw_ref, vw_ref, ng_ref):
    tm = h_ref.shape[1]
    hb = h_ref[0].astype(BF16)

    uv = _dot(hb, wuv_ref[...]) + buv_ref[...]
    u = _gelu_tanh(uv[:, :GM_WIDTH])
    v = _layer_norm(_gelu_tanh(uv[:, GM_WIDTH:]), lng_ref[...], lnb_ref[...]).astype(BF16)
    lane = lax.broadcasted_iota(jnp.int32, (GM_CHUNK, LANE), 1)
    group_w = GM_WIDTH // GM_GROUPS
    for ch in range(tm // GM_CHUNK):
        rows = slice(ch * GM_CHUNK, (ch + 1) * GM_CHUNK)
        for pr in range(GM_GROUPS // 2):
            cols = slice(pr * LANE, (pr + 1) * LANE)
            x = v[rows, cols]
            sv = jnp.where(lane < group_w, _dot(wc_ref[2 * pr], x), _dot(wc_ref[2 * pr + 1], x))
            a_ref[0, rows, cols] = (u[rows, cols] * (sv + bsb_ref[pr])).astype(BF16)

    qq = (_dot(hb, wq_ref[...]) + bq_ref[...]) * Q_SCALE
    for hh in range(NSA_HEADS):
        q_ref[0, hh] = qq[:, hh * LANE:(hh + 1) * LANE].astype(BF16)

    r = _dot(hb, wr_ref[...]) + br_ref[...]
    kc_ref[0] = r[:, 0:128]
    vc_ref[0] = r[:, 128:256]
    ks_ref[0, 0] = r[:, 256:384].astype(BF16)
    ks_ref[0, 1] = r[:, 384:512].astype(BF16)
    vs_ref[0] = r[:, 512:640].astype(BF16)
    kw_ref[0, 0] = r[:, 640:768].astype(BF16)
    kw_ref[0, 1] = r[:, 768:896].astype(BF16)
    vw_ref[0] = r[:, 896:1024].astype(BF16)
    ng_ref[0] = r[:, 1024:1152]


def _proj_call(h, p):
    B, S, _ = h.shape
    tm = ROW_TILE
    G = NSA_KV_GROUPS
    row = lambda w: pl.BlockSpec((1, tm, w), lambda b, i: (b, i, 0))
    grp = pl.BlockSpec((1, G, tm, LANE), lambda b, i: (b, 0, i, 0))
    out_shape = (
        jax.ShapeDtypeStruct((B, S, GM_WIDTH), BF16),
        jax.ShapeDtypeStruct((B, NSA_HEADS, S, LANE), BF16),
        jax.ShapeDtypeStruct((B, S, LANE), F32),
        jax.ShapeDtypeStruct((B, S, LANE), F32),
        jax.ShapeDtypeStruct((B, G, S, LANE), BF16),
        jax.ShapeDtypeStruct((B, S, LANE), BF16),
        jax.ShapeDtypeStruct((B, G, S, LANE), BF16),
        jax.ShapeDtypeStruct((B, S, LANE), BF16),
        jax.ShapeDtypeStruct((B, S, LANE), F32),
    )
    out_specs = (row(GM_WIDTH),
                 pl.BlockSpec((1, NSA_HEADS, tm, LANE), lambda b, i: (b, 0, i, 0)),
                 row(LANE), row(LANE), grp, row(LANE), grp, row(LANE), row(LANE))
    in_arrays = (h, p["w_uv"], p["b_uv"], p["gm_ln_g"], p["gm_ln_b"], p["gm_wc"], p["gm_bsb"],
                 p["w_q"], p["b_q"], p["w_r"], p["b_r"])
    in_specs = [row(D_MODEL)] + [_full(a.shape) for a in in_arrays[1:]]
    return pl.pallas_call(
        _proj_kernel, grid=(B, S // tm), in_specs=in_specs, out_specs=out_specs,
        out_shape=out_shape, compiler_params=_params(2), name="proj_gmlp")(*in_arrays)


def _compress_kernel(x_ref, plo_ref, phi_ref, w1lo_ref, w1hi_ref, b1_ref, w2_ref, o_ref):
    x = x_ref[0, 0, 0]
    n = x.shape[0]
    a = _dot((x + plo_ref[0]).astype(BF16), w1lo_ref[0])
    b = _dot((x + phi_ref[0]).astype(BF16), w1hi_ref[0])
    hid = a + pltpu.roll(b, n - 1, 0) + b1_ref[0]
    act = (hid * _sigmoid(hid)).astype(BF16)
    o_ref[0, 0, 0] = _dot(act, w2_ref[0])


def _compress_call(x, p):
    B, G, _, n, w = x.shape
    per_kv = lambda a: pl.BlockSpec((1,) + a.shape[1:], lambda b, g, k: (k,) + (0,) * (a.ndim - 1))
    ws = (p["cmp_pos_lo"], p["cmp_pos_hi"], p["cmp_w1_lo"], p["cmp_w1_hi"], p["cmp_b1"], p["cmp_w2"])
    return pl.pallas_call(
        _compress_kernel, grid=(B, G, 2),
        in_specs=[pl.BlockSpec((1, 1, 1, n, w), lambda b, g, k: (b, g, k, 0, 0))] + [per_kv(a) for a in ws],
        out_specs=pl.BlockSpec((1, 1, 1, n, HEAD_DIM), lambda b, g, k: (b, g, k, 0, 0)),
        out_shape=jax.ShapeDtypeStruct((B, G, 2, n, HEAD_DIM), F32),
        compiler_params=_params(3), name="kv_compress")(x, *ws)


def _nsa_kernel(q_ref, gl_ref, kc_ref, vct_ref, ovt_ref, bct_ref, ks_ref, vst_ref, vst2_ref, kw_ref, vwt_ref,
                bt0_ref, bt1_ref, o_ref, sc_ref, rhsf_ref, rhst_ref, m_ref, acc_ref,
                sbuf0_ref, sbuf1_ref, mbuf0_ref, mbuf1_ref):
    i = pl.program_id(2)
    n_blk = LANE
    blk = lax.broadcasted_iota(jnp.int32, (n_blk, Q_BLOCK), 0)
    qi = lax.broadcasted_iota(jnp.int32, (n_blk, Q_BLOCK), 1)
    eye = (blk == qi).astype(BF16)
    qt = jnp.concatenate([_dot_nt(eye, q_ref[0, h]) for h in range(NSA_HPG)], axis=1).astype(BF16)

    sc_ref[...] = _dot(kc_ref[0, 0], qt)
    band = pl.multiple_of(8 * i, 8)
    sc_ref[pl.ds(band, CMP_BAND), :] += bct_ref[0]
    s = sc_ref[...]
    key = lax.broadcasted_iota(jnp.int32, s.shape, 0)
    s = jnp.where(key >= 8 * i + CMP_BAND, NEG_INF, s)
    m = jnp.maximum(jnp.max(s, axis=0, keepdims=True), M_INIT)
    p = jnp.exp2(s - m)
    l = jnp.sum(p, axis=0, keepdims=True)
    r = jnp.where(l > 0.0, 1.0 / l, 0.0)
    oc = _dot(vct_ref[0, 0], p.astype(BF16))[0:HEAD_DIM] * r

    pn = p * r
    psum = pn[:, 0:Q_BLOCK]
    for h in range(1, NSA_HPG):
        psum = psum + pn[:, h * Q_BLOCK:(h + 1) * Q_BLOCK]
    p_hi = psum.astype(BF16)
    p_lo = (psum - p_hi.astype(F32)).astype(BF16)
    imp = _dot(ovt_ref[...], p_hi) + _dot(ovt_ref[...], p_lo)

    cur = (Q_BLOCK // SEL_BLOCK) * i + (qi >= SEL_BLOCK).astype(jnp.int32)
    forced = (blk == 0) | (blk == cur) | (blk == cur - 1)
    val = jnp.where(blk > cur, NEG_INF, jnp.where(forced, FORCED_SCORE, imp))
    blk_f = blk.astype(F32)

    def pick(_, carry):
        val, sel = carry
        mx = jnp.max(val, axis=0, keepdims=True)
        first = jnp.min(jnp.where(val == mx, blk_f, 1e9), axis=0, keepdims=True)
        hit = blk_f == first
        return jnp.where(hit, -3e38, val), jnp.where(hit, 1.0, sel)

    _, sel = lax.fori_loop(0, min(N_SEL, n_blk), pick, (val, jnp.zeros_like(val)))

    n_far = jnp.maximum(i - 1, 0) // FAR_TILES
    neg_far = jnp.where((sel > 0.5) & (blk <= cur), 0.0, NEG_INF)
    neg_tail = jnp.where(blk >= n_far * (FAR_TILES * KEY_TILE // SEL_BLOCK), neg_far, NEG_INF)
    per_head = lambda x: jnp.concatenate([x.astype(BF16)] * NSA_HPG, axis=1)
    rhsf_ref[0, 0:LANE] = per_head(neg_far)
    rhsf_ref[1, 0:LANE] = jnp.full((LANE, QCOLS), NEG_INF, BF16)
    rhst_ref[0:LANE] = per_head(neg_tail)
    rhsf_ref[0, LANE:2 * LANE] = qt
    rhsf_ref[1, LANE:2 * LANE] = qt
    rhst_ref[LANE:2 * LANE] = qt

    m_ref[...] = jnp.full(m_ref.shape, M_INIT, F32)
    acc_ref[...] = jnp.zeros(acc_ref.shape, F32)
    far_keys = FAR_TILES * KEY_TILE

    def far_logits(c):
        r0 = pl.multiple_of(PAD_TILES * KEY_TILE + far_keys * jnp.minimum(c, n_far - 1), far_keys)
        rhs = rhsf_ref[(c >= n_far).astype(jnp.int32)]
        s = _dot(ks_ref[0, 0, pl.ds(r0, far_keys), :], rhs)
        return s, jnp.max(s, axis=0, keepdims=True)

    def far_softmax(c, s_ref, mx_ref):
        tile0 = (PAD_TILES + FAR_TILES * jnp.minimum(c, n_far - 1)) // 2
        m_old = m_ref[...]
        m_new = jnp.maximum(m_old, mx_ref[...])
        p = jnp.exp2(s_ref[...] - m_new).astype(BF16)
        pv = jnp.zeros(acc_ref.shape, F32)
        for t in range(FAR_TILES // 2):
            pv = pv + _dot(vst2_ref[0, 0, tile0 + t], p[2 * t * KEY_TILE:2 * (t + 1) * KEY_TILE])
        acc_ref[...] = jnp.exp2(m_old - m_new) * acc_ref[...] + pv
        m_ref[...] = m_new

    bufs = ((sbuf0_ref, mbuf0_ref), (sbuf1_ref, mbuf1_ref))

    @pl.when(n_far > 0)
    def _():
        for u, (s_ref, mx_ref) in enumerate(bufs):
            s_ref[...], mx_ref[...] = far_logits(u)

    def far_pair(j, carry):
        for u, (s_ref, mx_ref) in enumerate(bufs):
            ahead = far_logits(2 * j + u + 2)
            far_softmax(2 * j + u, s_ref, mx_ref)
            s_ref[...], mx_ref[...] = ahead
        return carry

    lax.fori_loop(0, (n_far + 1) // 2, far_pair, 0)

    def band_softmax(k_ref, vt_ref, rhs, extra, m_old, acc_old):
        t0 = pl.multiple_of(KEY_TILE * i, KEY_TILE)
        tiles = []
        for t in range(BAND_TILES):
            s = _dot(k_ref[0, 0, pl.ds(t0 + KEY_TILE * t, KEY_TILE), :], rhs)
            tiles.append(s if extra[t] is None else s + extra[t])
        m_new = m_old
        for s in tiles:
            m_new = jnp.maximum(m_new, jnp.max(s, axis=0, keepdims=True))
        acc = jnp.exp2(m_old - m_new) * acc_old
        for t, s in enumerate(tiles):
            acc = acc + _dot(vt_ref[0, 0, i + t], jnp.exp2(s - m_new).astype(BF16))
        return acc[0:HEAD_DIM] / acc[HEAD_DIM:HEAD_DIM + 1]

    bt0 = bt0_ref[0]
    bt1 = bt1_ref[0]
    extra = [None] * (BAND_TILES - 2) + [bt1, bt0]
    os_ = band_softmax(ks_ref, vst_ref, rhst_ref[...], extra, m_ref[...], acc_ref[...])

    kk = lax.broadcasted_iota(jnp.int32, (KEY_TILE, QCOLS), 0)
    qq = lax.broadcasted_iota(jnp.int32, (KEY_TILE, QCOLS), 1) & (Q_BLOCK - 1)
    edge = jnp.where(kk > qq, 0.0, NEG_INF)
    extra = [edge] + [None] * (BAND_TILES - 3) + [bt1, bt0]
    ow = band_softmax(kw_ref, vwt_ref, qt, extra, jnp.full(m_ref.shape, M_INIT, F32),
                      jnp.zeros(acc_ref.shape, F32))

    g = _sigmoid(gl_ref[0, 0, 0])
    o_ref[0, 0, 0] = g[0:1] * oc + g[1:2] * os_ + g[2:3] * ow


def _nsa_call(q, gl, kc, vct, ovt, bct, ks, vst, vst2, kw, vwt, bt0, bt1):
    B, _, S, _ = q.shape
    G = NSA_KV_GROUPS
    nq = S // Q_BLOCK
    per_bg = lambda a: pl.BlockSpec((1, 1) + a.shape[2:], lambda b, g, i: (b, g) + (0,) * (a.ndim - 2))
    per_g = lambda a: pl.BlockSpec((1,) + a.shape[1:], lambda b, g, i: (g,) + (0,) * (a.ndim - 1))
    in_specs = [
        pl.BlockSpec((1, NSA_HPG, Q_BLOCK, LANE), lambda b, g, i: (b, g, i, 0)),
        pl.BlockSpec((1, 1, 1, 8, QCOLS), lambda b, g, i: (b, g, i, 0, 0)),
        per_bg(kc), per_bg(vct), _full(ovt.shape), per_g(bct),
        per_bg(ks), per_bg(vst), per_bg(vst2), per_bg(kw), per_bg(vwt), per_g(bt0), per_g(bt1),
    ]
    scratch = [
        pltpu.VMEM((CMP_ROWS, QCOLS), F32),
        pltpu.VMEM((2, 2 * LANE, QCOLS), BF16),
        pltpu.VMEM((2 * LANE, QCOLS), BF16),
        pltpu.VMEM((1, QCOLS), F32),
        pltpu.VMEM((V_ROWS, QCOLS), F32),
        pltpu.VMEM((FAR_TILES * KEY_TILE, QCOLS), F32),
        pltpu.VMEM((FAR_TILES * KEY_TILE, QCOLS), F32),
        pltpu.VMEM((1, QCOLS), F32),
        pltpu.VMEM((1, QCOLS), F32),
    ]
    return pl.pallas_call(
        _nsa_kernel, grid=(B, G, nq), in_specs=in_specs,
        out_specs=pl.BlockSpec((1, 1, 1, HEAD_DIM, QCOLS), lambda b, g, i: (b, g, i, 0, 0)),
        out_shape=jax.ShapeDtypeStruct((B, G, nq, HEAD_DIM, QCOLS), F32),
        scratch_shapes=scratch, compiler_params=_params(3), name="sparse_attention",
    )(q, gl, kc, vct, ovt, bct, ks, vst, vst2, kw, vwt, bt0, bt1)


def _merge_kernel(h_ref, a_ref, o_ref, wmg_ref, bmg_ref, wa_ref, wb_ref, wout_ref, lng_ref, lnb_ref, out_ref):
    h = h_ref[0]
    mg = _dot(h.astype(BF16), wmg_ref[...]) + bmg_ref[...]
    y = (_sigmoid(mg[:, :D_MODEL]) * _dot(a_ref[0], wa_ref[...])
         + _sigmoid(mg[:, D_MODEL:]) * _dot(o_ref[0], wb_ref[...]))
    m = _dot(y.astype(BF16), wout_ref[...])
    out_ref[0] = _layer_norm(ALPHA * h + m, lng_ref[...], lnb_ref[...])


def _merge_call(h, a, o, p):
    B, S, _ = h.shape
    tm = ROW_TILE
    row = lambda w: pl.BlockSpec((1, tm, w), lambda b, i: (b, i, 0))
    ws = (p["w_mg"], p["b_mg"], p["w_proj_a"], p["w_proj_b"], p["w_out"], p["ln1_g"], p["ln1_b"])
    return pl.pallas_call(
        _merge_kernel, grid=(B, S // tm),
        in_specs=[row(D_MODEL), row(GM_WIDTH), row(NSA_WIDTH)] + [_full(a.shape) for a in ws],
        out_specs=row(D_MODEL), out_shape=jax.ShapeDtypeStruct((B, S, D_MODEL), F32),
        compiler_params=_params(2), name="merge_norm")(h, a, o, *ws)


def _ffn_kernel(h_ref, wup_ref, cw_ref, wdown_ref, lng_ref, lnb_ref, out_ref, u_ref, carry_ref):
    tm = h_ref.shape[1]

    @pl.when(pl.program_id(1) == 0)
    def _():
        carry_ref[...] = jnp.zeros(carry_ref.shape, F32)

    h = h_ref[0]
    hb = h.astype(BF16)

    def conv(idx):
        up = _dot(hb, wup_ref[idx])
        u_ref[0:8] = carry_ref[idx]
        u_ref[8:8 + tm] = up
        carry_ref[idx] = up[tm - 8:tm]
        cw = cw_ref[idx]
        return (cw[3:4] + cw[0:1] * u_ref[6:6 + tm] + cw[1:2] * u_ref[7:7 + tm] + cw[2:3] * up)

    f = jnp.zeros((tm, D_MODEL), F32)
    for c in range(N_FF):
        gate = conv(c)
        val = conv(N_FF + c)
        act = (gate * _sigmoid(gate) * val).astype(BF16)
        f = f + _dot(act, wdown_ref[c])
    out_ref[0] = _layer_norm(ALPHA * h + f, lng_ref[...], lnb_ref[...])


def _ffn_call(h, p):
    B, S, _ = h.shape
    tm = ROW_TILE
    row = pl.BlockSpec((1, tm, D_MODEL), lambda b, i: (b, i, 0))
    ws = (p["ffn_w_up"], p["ffn_cw"], p["ffn_w_down"], p["ln2_g"], p["ln2_b"])
    return pl.pallas_call(
        _ffn_kernel, grid=(B, S // tm),
        in_specs=[row] + [_full(a.shape) for a in ws],
        out_specs=row, out_shape=jax.ShapeDtypeStruct((B, S, D_MODEL), F32),
        scratch_shapes=[pltpu.VMEM((tm + 8, FF_CHUNK), F32), pltpu.VMEM((2 * N_FF, 8, FF_CHUNK), F32)],
        compiler_params=_params(2), name="conv_ffn_norm")(h, *ws)


def _t5_bucket(n):
    max_exact = NUM_BUCKETS // 2
    log_ratio = jnp.log(jnp.maximum(n, 1).astype(F32) / max_exact) / math.log(MAX_DISTANCE / max_exact)
    large = jnp.minimum(max_exact + (log_ratio * (NUM_BUCKETS - max_exact)).astype(jnp.int32), NUM_BUCKETS - 1)
    return jnp.where(n < max_exact, n, large)


def _bias_tiles(rel_bias):
    G, H = NSA_KV_GROUPS, NSA_HPG
    tab = rel_bias[_t5_bucket(jnp.arange(2 * KEY_TILE, dtype=jnp.int32))] - rel_bias[NUM_BUCKETS - 1]
    tab = tab * LOG2E

    def tile(dist):
        t = jnp.where((dist >= 0)[..., None], tab[np.clip(dist, 0, 2 * KEY_TILE - 1)], NEG_INF)
        rows = dist.shape[0]
        return t.reshape(rows, Q_BLOCK, G, H).transpose(2, 0, 3, 1).reshape(G, rows, H * Q_BLOCK)

    ki = np.arange(KEY_TILE)[:, None]
    qi = np.arange(Q_BLOCK)[None, :]
    bt0 = tile(qi - ki)
    bt1 = tile(KEY_TILE + qi - ki)
    rr = np.arange(CMP_BAND)[:, None]
    bct = tile(qi + (CMP_STRIDE * CMP_PAD - CMP_BLOCK + 1) - CMP_STRIDE * rr)
    return bt0, bt1, bct


def _overlap_t(seq):
    n_cmp = seq // CMP_STRIDE - 1
    n_blk = seq // SEL_BLOCK
    cs = np.arange(n_cmp)[:, None] * CMP_STRIDE
    ss = np.arange(n_blk)[None, :] * SEL_BLOCK
    ov = np.maximum(np.minimum(cs + CMP_BLOCK, ss + SEL_BLOCK) - np.maximum(cs, ss), 0) / CMP_BLOCK
    out = np.zeros((LANE, CMP_ROWS), np.float32)
    out[:n_blk, CMP_PAD:CMP_PAD + n_cmp] = ov.T
    return jnp.asarray(out, BF16)


def _pad_cols(w, width):
    return jnp.pad(w, ((0, 0), (0, width - w.shape[1])))


def _layer_params(l, w_in, b_in, gm_ln_g, gm_ln_b, gm_ws, gm_bs, cmp_pos, cmp_w1, cmp_b1, cmp_w2,
                  w_proj_a, w_proj_b, w_out, ln1_g, ln1_b, ffn_w_up, ffn_conv_w, ffn_conv_b, ffn_w_down,
                  ln2_g, ln2_b):
    w, b = w_in[l], b_in[l]
    o_q = 2 * GM_WIDTH
    o_kc = o_q + NSA_WIDTH
    o_vc, o_ks, o_vs, o_kw, o_vw = (o_kc + KV_WIDTH * k for k in range(1, 6))
    o_ng = o_vw + KV_WIDTH
    o_mg = o_ng + NSA_HEADS * 3
    seg = lambda a, o, n: a[..., o:o + n]

    wq = jnp.pad(seg(w, o_q, NSA_WIDTH).reshape(D_MODEL, NSA_HEADS, HEAD_DIM),
                 ((0, 0), (0, 0), (0, LANE - HEAD_DIM))).reshape(D_MODEL, NSA_HEADS * LANE)
    bq = jnp.pad(seg(b, o_q, NSA_WIDTH).reshape(NSA_HEADS, HEAD_DIM), ((0, 0), (0, LANE - HEAD_DIM)))
    bq = bq.at[:, HEAD_DIM].set(NEG_INF / Q_SCALE).reshape(1, NSA_HEADS * LANE)

    def k_groups(o):
        return [_pad_cols(seg(w, o + g * HEAD_DIM, HEAD_DIM), LANE) for g in range(NSA_KV_GROUPS)], \
               [jnp.pad(seg(b, o + g * HEAD_DIM, HEAD_DIM), (0, LANE - HEAD_DIM)) for g in range(NSA_KV_GROUPS)]

    wks, bks = k_groups(o_ks)
    wkw, bkw = k_groups(o_kw)
    n_ng = NSA_HEADS * 3
    w_r = jnp.concatenate([seg(w, o_kc, KV_WIDTH), seg(w, o_vc, KV_WIDTH), *wks, seg(w, o_vs, KV_WIDTH),
                           *wkw, seg(w, o_vw, KV_WIDTH), _pad_cols(seg(w, o_ng, n_ng), LANE)], axis=1)
    b_r = jnp.concatenate([seg(b, o_kc, KV_WIDTH), seg(b, o_vc, KV_WIDTH), *bks, seg(b, o_vs, KV_WIDTH),
                           *bkw, seg(b, o_vw, KV_WIDTH), jnp.pad(seg(b, o_ng, n_ng), (0, LANE - n_ng))])

    causal = jnp.tril(jnp.ones((GM_CHUNK, GM_CHUNK), F32))
    gw = GM_WIDTH // GM_GROUPS
    bs = gm_bs[l]
    bsb = jnp.concatenate([jnp.broadcast_to(bs[0::2, :, None], (GM_GROUPS // 2, GM_CHUNK, gw)),
                           jnp.broadcast_to(bs[1::2, :, None], (GM_GROUPS // 2, GM_CHUNK, gw))], axis=2)

    half = CMP_BLOCK // 2 * HEAD_DIM
    pos = cmp_pos[l].reshape(2, 1, CMP_BLOCK * HEAD_DIM)
    up = ffn_w_up[l].reshape(D_MODEL, 2 * N_FF, FF_CHUNK).transpose(1, 0, 2)
    cw = jnp.concatenate([ffn_conv_w[l], ffn_conv_b[l][None], jnp.zeros((8 - CONV_WIDTH - 1, 2 * D_FF), F32)])
    return {
        "w_uv": seg(w, 0, 2 * GM_WIDTH).astype(BF16), "b_uv": seg(b, 0, 2 * GM_WIDTH)[None],
        "gm_ln_g": gm_ln_g[l][None], "gm_ln_b": gm_ln_b[l][None],
        "gm_wc": (gm_ws[l] * causal).astype(BF16), "gm_bsb": bsb,
        "w_q": wq.astype(BF16), "b_q": bq, "w_r": w_r.astype(BF16), "b_r": b_r[None],
        "cmp_pos_lo": pos[:, :, :half], "cmp_pos_hi": pos[:, :, half:],
        "cmp_w1_lo": cmp_w1[l][:, :half].astype(BF16), "cmp_w1_hi": cmp_w1[l][:, half:].astype(BF16),
        "cmp_b1": cmp_b1[l][:, None], "cmp_w2": cmp_w2[l].astype(BF16),
        "w_mg": seg(w, o_mg, 2 * D_MODEL).astype(BF16), "b_mg": seg(b, o_mg, 2 * D_MODEL)[None],
        "w_proj_a": w_proj_a[l].astype(BF16), "w_proj_b": w_proj_b[l].astype(BF16),
        "w_out": w_out[l].astype(BF16), "ln1_g": ln1_g[l][None], "ln1_b": ln1_b[l][None],
        "ffn_w_up": up.astype(BF16),
        "ffn_cw": cw.reshape(8, 2 * N_FF, FF_CHUNK).transpose(1, 0, 2),
        "ffn_w_down": ffn_w_down[l].reshape(N_FF, FF_CHUNK, D_MODEL).astype(BF16),
        "ln2_g": ln2_g[l][None], "ln2_b": ln2_b[l][None],
    }


def _key_major(k, flag_col, front_rows):
    B, G, _, W = k.shape
    dummy = jnp.zeros((front_rows, W), k.dtype).at[:, flag_col].set(1)
    return jnp.concatenate([jnp.broadcast_to(dummy, (B, G, front_rows, W)), k], axis=2)


def _value_tiles(v, tile=KEY_TILE):
    B, S, _ = v.shape
    G = NSA_KV_GROUPS
    nt = S // tile
    vt = v.reshape(B, nt, tile, G, HEAD_DIM).transpose(0, 3, 1, 4, 2)
    vt = jnp.concatenate([vt, jnp.ones((B, G, nt, 1, tile), v.dtype),
                          jnp.zeros((B, G, nt, V_ROWS - HEAD_DIM - 1, tile), v.dtype)], axis=3)
    return jnp.pad(vt, ((0, 0), (0, 0), (PAD_TILES * KEY_TILE // tile, 0), (0, 0), (0, 0)))


def _token_mixers(h, p, consts):
    B, S, _ = h.shape
    G, H = NSA_KV_GROUPS, NSA_HPG
    nq = S // Q_BLOCK
    a, q, kc, vc, ks, vs, kw, vw, ng = _proj_call(h, p)

    n_chunk = S // CMP_STRIDE
    chunks = lambda z: z.reshape(B, n_chunk, CMP_STRIDE, G, HEAD_DIM).transpose(0, 3, 1, 2, 4).reshape(
        B, G, n_chunk, CMP_STRIDE * HEAD_DIM)
    cmp = _compress_call(jnp.stack([chunks(kc), chunks(vc)], axis=2), p)[:, :, :, :n_chunk - 1]
    back = CMP_ROWS - CMP_PAD - (n_chunk - 1)
    kcp = jnp.pad(cmp[:, :, 0], ((0, 0), (0, 0), (0, 0), (0, LANE - HEAD_DIM))).astype(BF16)
    kcp = _key_major(kcp, HEAD_DIM, CMP_PAD)
    kcp = jnp.concatenate([kcp, jnp.broadcast_to(kcp[:, :, :1], (B, G, back, LANE))], axis=2)
    vct = jnp.pad(cmp[:, :, 1].transpose(0, 1, 3, 2),
                  ((0, 0), (0, 0), (0, V_ROWS - HEAD_DIM), (CMP_PAD, back))).astype(BF16)

    ks_aug = _key_major(jnp.concatenate([jnp.broadcast_to(consts["onehot"], (B, G, S, LANE)), ks], axis=3),
                        LANE + HEAD_DIM, PAD_TILES * KEY_TILE)
    kw_aug = _key_major(kw, HEAD_DIM, PAD_TILES * KEY_TILE)
    gl = ng[:, :, :G * H * 3].reshape(B, nq, Q_BLOCK, G, H, 3).transpose(0, 3, 1, 5, 4, 2).reshape(
        B, G, nq, 3, H * Q_BLOCK)
    gl = jnp.pad(gl, ((0, 0), (0, 0), (0, 0), (0, 8 - 3), (0, 0)))

    ot = _nsa_call(q, gl, kcp, vct, consts["ovt"], consts["bct"], ks_aug, _value_tiles(vs),
                   _value_tiles(vs, 2 * KEY_TILE), kw_aug, _value_tiles(vw), consts["bt0"], consts["bt1"])
    o = ot.reshape(B, G, nq, HEAD_DIM, H, Q_BLOCK).transpose(0, 2, 5, 1, 4, 3).reshape(B, S, NSA_WIDTH)
    return _merge_call(h, a, o.astype(BF16), p)


def kernel(x, w_in, b_in, gm_ln_g, gm_ln_b, gm_ws, gm_bs, cmp_pos, cmp_w1, cmp_b1, cmp_w2, rel_bias,
           w_proj_a, w_proj_b, w_out, ln1_g, ln1_b, ffn_w_up, ffn_conv_w, ffn_conv_b, ffn_w_down,
           ln2_g, ln2_b):
    B, S, _ = x.shape
    assert S % (FAR_TILES * KEY_TILE) == 0 and N_SEL <= S // SEL_BLOCK <= LANE
    assert S // CMP_STRIDE - 1 + CMP_PAD < CMP_ROWS
    bt0, bt1, bct = _bias_tiles(rel_bias)
    onehot = np.zeros((S, LANE), np.float32)
    onehot[np.arange(S), np.arange(S) // SEL_BLOCK] = 1.0
    consts = {"bt0": bt0, "bt1": bt1, "bct": bct, "ovt": _overlap_t(S), "onehot": jnp.asarray(onehot, BF16)}
    h = x
    for l in range(DEPTH):
        p = _layer_params(l, w_in, b_in, gm_ln_g, gm_ln_b, gm_ws, gm_bs, cmp_pos, cmp_w1, cmp_b1, cmp_w2,
                          w_proj_a, w_proj_b, w_out, ln1_g, ln1_b, ffn_w_up, ffn_conv_w, ffn_conv_b,
                          ffn_w_down, ln2_g, ln2_b)
        h = _token_mixers(h, p, consts)
        h = _ffn_call(h, p)
    return h
```

```python
import math

import jax
import jax.numpy as jnp
import numpy as np
from jax import lax
from jax.experimental import pallas as pl
from jax.experimental.pallas import tpu as pltpu

D_MODEL = 1024
DEPTH = 2
GM_WIDTH = D_MODEL // 2
GM_GROUPS = 8
GM_CHUNK = 128
NSA_HEADS = 8
NSA_KV_GROUPS = 2
NSA_HPG = NSA_HEADS // NSA_KV_GROUPS
HEAD_DIM = 64
NSA_WIDTH = NSA_HEADS * HEAD_DIM
KV_WIDTH = NSA_KV_GROUPS * HEAD_DIM
N_GATES = 3
CMP_BLOCK = 32
CMP_STRIDE = 16
CMP_HIDDEN = 256
SEL_BLOCK = 64
N_SEL = 16
WINDOW = 512
Q_BLOCK = 128
NUM_BUCKETS = 32
MAX_DISTANCE = 128
D_FF = 2816
CONV_WIDTH = 3
ALPHA = (2.0 * DEPTH) ** 0.25
LN_EPS = 1e-5
FORCED_SCORE = 1e6
NEG_INF = -1e30
M_INIT = -1e29
LOG2E = math.log2(math.e)
Q_SCALE = HEAD_DIM ** -0.5 * LOG2E

LANE = 128
QCOLS = NSA_HPG * Q_BLOCK
KEY_TILE = 128
FAR_TILES = 4
FAR_KEYS = FAR_TILES * KEY_TILE
FAR_AHEAD = 2
FAR_BUFS = 2
BAND_TILES = WINDOW // KEY_TILE + 1
PAD_TILES = BAND_TILES - 1
CMP_PAD = 8
CMP_ROWS = 640
CMP_BAND = 16
V_ROWS = 80
FF_CHUNK = 256
N_FF = D_FF // FF_CHUNK
ROW_TILE = 256
VMEM_LIMIT = 56 * 1024 * 1024

F32 = jnp.float32
BF16 = jnp.bfloat16
NT_DIMS = (((1,), (1,)), ((), ()))


def _dot(a, b):
    return jnp.dot(a, b, preferred_element_type=F32)


def _dot_nt(a, b):
    return lax.dot_general(a, b, NT_DIMS, preferred_element_type=F32)


def _layer_norm(x, g, b):
    mu = jnp.mean(x, axis=-1, keepdims=True)
    xc = x - mu
    var = jnp.mean(xc * xc, axis=-1, keepdims=True)
    return xc * lax.rsqrt(var + LN_EPS) * g + b


def _gelu_tanh(x):
    c = math.sqrt(2.0 / math.pi)
    return x * (0.5 * (1.0 + jnp.tanh(c * (x + 0.044715 * (x * x * x)))))


def _sigmoid(x):
    return 1.0 / (1.0 + jnp.exp(-x))


def _params(n_axes):
    return pltpu.CompilerParams(dimension_semantics=("arbitrary",) * n_axes,
                                vmem_limit_bytes=VMEM_LIMIT)


def _full(shape):
    n = len(shape)
    return pl.BlockSpec(shape, lambda *_: (0,) * n)


def _proj_kernel(h_ref, wuv_ref, buv_ref, lng_ref, lnb_ref, wc_ref, bsb_ref, wq_ref, bq_ref,
                 wr_ref, br_ref, wvt_ref, bvt_ref,
                 a_ref, q_ref, kvc_ref, ks_ref, kw_ref, vst_ref, vst2_ref, vwt_ref, ngt_ref):
    tm = h_ref.shape[1]
    G = NSA_KV_GROUPS
    hb = h_ref[0].astype(BF16)

    uv = _dot(hb, wuv_ref[...]) + buv_ref[...]
    u = _gelu_tanh(uv[:, :GM_WIDTH])
    v = _layer_norm(_gelu_tanh(uv[:, GM_WIDTH:]), lng_ref[...], lnb_ref[...]).astype(BF16)
    lane = lax.broadcasted_iota(jnp.int32, (GM_CHUNK, LANE), 1)
    group_w = GM_WIDTH // GM_GROUPS
    for ch in range(tm // GM_CHUNK):
        rows = slice(ch * GM_CHUNK, (ch + 1) * GM_CHUNK)
        for pr in range(GM_GROUPS // 2):
            cols = slice(pr * LANE, (pr + 1) * LANE)
            x = v[rows, cols]
            sv = jnp.where(lane < group_w, _dot(wc_ref[2 * pr], x), _dot(wc_ref[2 * pr + 1], x))
            a_ref[0, rows, cols] = (u[rows, cols] * (sv + bsb_ref[pr])).astype(BF16)

    qq = (_dot(hb, wq_ref[...]) + bq_ref[...]) * Q_SCALE
    for hh in range(NSA_HEADS):
        q_ref[0, hh] = qq[:, hh * LANE:(hh + 1) * LANE].astype(BF16)

    r = _dot(hb, wr_ref[...]) + br_ref[...]
    kvc_ref[0, 0] = r[:, 0:LANE]
    kvc_ref[0, 1] = r[:, LANE:2 * LANE]
    token = pl.program_id(1) * tm + lax.broadcasted_iota(jnp.int32, (tm, LANE), 0)
    sel_block = lax.shift_right_logical(token, int(math.log2(SEL_BLOCK)))
    onehot = (lax.broadcasted_iota(jnp.int32, (tm, LANE), 1) == sel_block).astype(BF16)
    for g in range(G):
        ks_ref[0, g, :, 0:LANE] = onehot
        ks_ref[0, g, :, LANE:2 * LANE] = r[:, (2 + g) * LANE:(3 + g) * LANE].astype(BF16)
        kw_ref[0, g] = r[:, (4 + g) * LANE:(5 + g) * LANE].astype(BF16)
    ngt_ref[0] = r[:, 6 * LANE:7 * LANE].T

    vt = (_dot_nt(wvt_ref[...], hb) + bvt_ref[...]).astype(BF16)
    for g in range(G):
        sel_rows = slice(g * V_ROWS, (g + 1) * V_ROWS)
        win_rows = slice((G + g) * V_ROWS, (G + g + 1) * V_ROWS)
        vst2_ref[0, g, 0] = vt[sel_rows]
        for t in range(tm // KEY_TILE):
            cols = slice(t * KEY_TILE, (t + 1) * KEY_TILE)
            vst_ref[0, g, t] = vt[sel_rows, cols]
            vwt_ref[0, g, t] = vt[win_rows, cols]


def _proj_call(h, p):
    B, S, _ = h.shape
    tm = 2 * KEY_TILE
    G = NSA_KV_GROUPS
    nt = S // KEY_TILE
    row = lambda w: pl.BlockSpec((1, tm, w), lambda b, i: (b, i, 0))
    lead = lambda n, w: pl.BlockSpec((1, n, tm, w), lambda b, i: (b, 0, i, 0))
    vtile = lambda n, w: pl.BlockSpec((1, G, n, V_ROWS, w), lambda b, i: (b, 0, i, 0, 0))
    out_shape = (
        jax.ShapeDtypeStruct((B, S, GM_WIDTH), BF16),
        jax.ShapeDtypeStruct((B, NSA_HEADS, S, LANE), BF16),
        jax.ShapeDtypeStruct((B, 2, S, LANE), F32),
        jax.ShapeDtypeStruct((B, G, S, 2 * LANE), BF16),
        jax.ShapeDtypeStruct((B, G, S, LANE), BF16),
        jax.ShapeDtypeStruct((B, G, nt, V_ROWS, KEY_TILE), BF16),
        jax.ShapeDtypeStruct((B, G, nt // 2, V_ROWS, 2 * KEY_TILE), BF16),
        jax.ShapeDtypeStruct((B, G, nt, V_ROWS, KEY_TILE), BF16),
        jax.ShapeDtypeStruct((B, LANE, S), F32),
    )
    out_specs = (row(GM_WIDTH), lead(NSA_HEADS, LANE), lead(2, LANE), lead(G, 2 * LANE), lead(G, LANE),
                 vtile(tm // KEY_TILE, KEY_TILE), vtile(1, 2 * KEY_TILE), vtile(tm // KEY_TILE, KEY_TILE),
                 pl.BlockSpec((1, LANE, tm), lambda b, i: (b, 0, i)))
    in_arrays = (h, p["w_uv"], p["b_uv"], p["gm_ln_g"], p["gm_ln_b"], p["gm_wc"], p["gm_bsb"],
                 p["w_q"], p["b_q"], p["w_r"], p["b_r"], p["w_vt"], p["b_vt"])
    in_specs = [row(D_MODEL)] + [_full(a.shape) for a in in_arrays[1:]]
    return pl.pallas_call(
        _proj_kernel, grid=(B, S // tm), in_specs=in_specs, out_specs=out_specs,
        out_shape=out_shape, compiler_params=_params(2), name="proj_gmlp")(*in_arrays)


def _compress_kernel(x_ref, pos_ref, w1_ref, b1_ref, w2k_ref, w2vt_ref, kc_ref, vct_ref):
    kv = pl.program_id(1)
    n = x_ref.shape[2] // CMP_STRIDE
    half = CMP_BLOCK // 2
    pad_row = (lax.broadcasted_iota(jnp.int32, (1, LANE), 1) == HEAD_DIM).astype(F32)
    for g in range(NSA_KV_GROUPS):
        first = jnp.zeros((n, CMP_HIDDEN), F32)
        second = jnp.zeros((n, CMP_HIDDEN), F32)
        for r in range(half):
            xr = x_ref[0, 0, pl.ds(r, n, stride=CMP_STRIDE), :]
            first = first + _dot((xr + pos_ref[0, r:r + 1, :]).astype(BF16), w1_ref[0, g, r])
            second = second + _dot((xr + pos_ref[0, half + r:half + r + 1, :]).astype(BF16),
                                   w1_ref[0, g, half + r])
        hid = first + pltpu.roll(second, n - 1, 0) + b1_ref[0]
        act = (hid * _sigmoid(hid)).astype(BF16)

        @pl.when(kv == 0)
        def _():
            res = _dot(act, w2k_ref[...])
            row = lax.broadcasted_iota(jnp.int32, res.shape, 0)
            res = jnp.where(row == n - 1, pad_row, res)
            kc_ref[0, g] = jnp.concatenate(
                [jnp.broadcast_to(pad_row, (CMP_PAD, LANE)), res,
                 jnp.broadcast_to(pad_row, (CMP_ROWS - CMP_PAD - n, LANE))], axis=0).astype(BF16)

        @pl.when(kv == 1)
        def _():
            vt = _dot_nt(w2vt_ref[...], act)
            col = lax.broadcasted_iota(jnp.int32, vt.shape, 1)
            vt = jnp.where(col == n - 1, 0.0, vt)
            vt = jnp.concatenate([vt, jnp.zeros((V_ROWS, CMP_ROWS - n), F32)], axis=1)
            vct_ref[0, g] = pltpu.roll(vt, CMP_PAD, 1).astype(BF16)


def _compress_call(kvc, p):
    B, _, S, _ = kvc.shape
    G = NSA_KV_GROUPS
    per_kv = lambda a: pl.BlockSpec((1,) + a.shape[1:], lambda b, k: (k,) + (0,) * (a.ndim - 1))
    return pl.pallas_call(
        _compress_kernel, grid=(B, 2),
        in_specs=[pl.BlockSpec((1, 1, S, LANE), lambda b, k: (b, k, 0, 0)),
                  per_kv(p["cmp_pos"]), per_kv(p["cmp_w1"]), per_kv(p["cmp_b1"]),
                  _full(p["cmp_w2k"].shape), _full(p["cmp_w2vt"].shape)],
        out_specs=(pl.BlockSpec((1, G, CMP_ROWS, LANE), lambda b, k: (b, 0, 0, 0)),
                   pl.BlockSpec((1, G, V_ROWS, CMP_ROWS), lambda b, k: (b, 0, 0, 0))),
        out_shape=(jax.ShapeDtypeStruct((B, G, CMP_ROWS, LANE), BF16),
                   jax.ShapeDtypeStruct((B, G, V_ROWS, CMP_ROWS), BF16)),
        compiler_params=_params(2), name="kv_compress",
    )(kvc, p["cmp_pos"], p["cmp_w1"], p["cmp_b1"], p["cmp_w2k"], p["cmp_w2vt"])


def _nsa_kernel(q_ref, ngt_ref, kc_ref, vct_ref, ovt_ref, bct_ref, ks_ref, vst_ref, vst2_ref, kw_ref, vwt_ref,
                bt0_ref, bt1_ref, o_ref, sc_ref, rhsf_ref, rhst_ref, m_ref, acc_ref, *ring_refs):
    sbuf_refs, mbuf_refs = ring_refs[:FAR_BUFS], ring_refs[FAR_BUFS:]
    grp = pl.program_id(1)
    i = pl.program_id(2)
    n_blk = LANE
    blk = lax.broadcasted_iota(jnp.int32, (n_blk, Q_BLOCK), 0)
    qi = lax.broadcasted_iota(jnp.int32, (n_blk, Q_BLOCK), 1)
    eye = (blk == qi).astype(BF16)
    qt = jnp.concatenate([_dot_nt(eye, q_ref[0, h]) for h in range(NSA_HPG)], axis=1).astype(BF16)
    bt0 = bt0_ref[0]
    bt1 = bt1_ref[0]


    def band_logits(k_ref, rhs, extra, flag_col):
        width = k_ref.shape[3]
        before_start = (lax.broadcasted_iota(jnp.int32, (KEY_TILE, width), 1) == flag_col).astype(BF16)
        tiles = []
        for t in range(BAND_TILES):
            tau = i - PAD_TILES + t
            row0 = pl.multiple_of(jnp.maximum(tau, 0) * KEY_TILE, KEY_TILE)
            k = k_ref[0, 0, pl.ds(row0, KEY_TILE), :]
            if t < PAD_TILES:
                k = jnp.where(jnp.broadcast_to(tau, k.shape) >= 0, k, before_start)
            s = _dot(k, rhs)
            tiles.append(s if extra[t] is None else s + extra[t])
        return tiles

    def band_softmax(tiles, vt_ref, m_old, acc_old):
        m_new = m_old
        for s in tiles:
            m_new = jnp.maximum(m_new, jnp.max(s, axis=0, keepdims=True))
        acc = jnp.exp2(m_old - m_new) * acc_old
        for t, s in enumerate(tiles):
            vt = vt_ref[0, 0, jnp.maximum(i - PAD_TILES + t, 0)]
            acc = acc + _dot(vt, jnp.exp2(s - m_new).astype(BF16))
        return acc[0:HEAD_DIM] / acc[HEAD_DIM:HEAD_DIM + 1]

    sc_ref[...] = _dot(kc_ref[0, 0], qt)
    band = pl.multiple_of(8 * i, 8)
    sc_ref[pl.ds(band, CMP_BAND), :] += bct_ref[0]
    s = sc_ref[...]
    key = lax.broadcasted_iota(jnp.int32, s.shape, 0)
    s = jnp.where(key >= 8 * i + CMP_BAND, NEG_INF, s)
    m = jnp.maximum(jnp.max(s, axis=0, keepdims=True), M_INIT)
    p = jnp.exp2(s - m)
    l = jnp.sum(p, axis=0, keepdims=True)
    r = jnp.where(l > 0.0, 1.0 / l, 0.0)
    oc = _dot(vct_ref[0, 0], p.astype(BF16))[0:HEAD_DIM] * r

    pn = p * r
    psum = pn[:, 0:Q_BLOCK]
    for h in range(1, NSA_HPG):
        psum = psum + pn[:, h * Q_BLOCK:(h + 1) * Q_BLOCK]
    p_hi = psum.astype(BF16)
    p_lo = (psum - p_hi.astype(F32)).astype(BF16)
    imp = _dot(ovt_ref[...], p_hi) + _dot(ovt_ref[...], p_lo)

    cur = (Q_BLOCK // SEL_BLOCK) * i + (qi >= SEL_BLOCK).astype(jnp.int32)
    forced = (blk == 0) | (blk == cur) | (blk == cur - 1)
    taken = -3e38
    val = jnp.where(forced, taken, jnp.where(blk > cur, NEG_INF, imp))
    blk_f = blk.astype(F32)

    def pick(_, carry):
        val, sel = carry
        mx = jnp.max(val, axis=0, keepdims=True)
        first = jnp.min(jnp.where(val == mx, blk_f, 1e9), axis=0, keepdims=True)
        hit = blk_f == first
        return jnp.where(hit, taken, val), jnp.where(hit, 1.0, sel)

    _, sel = lax.fori_loop(0, N_SEL - 3, pick, (val, forced.astype(F32)))

    n_far = jnp.maximum(i - 1, 0) // FAR_TILES
    neg_far = jnp.where((sel > 0.5) & (blk <= cur), 0.0, NEG_INF)
    neg_tail = jnp.where(blk >= n_far * (FAR_KEYS // SEL_BLOCK), neg_far, NEG_INF)
    per_head = lambda x: jnp.concatenate([x.astype(BF16)] * NSA_HPG, axis=1)
    rhsf_ref[0, 0:LANE] = per_head(neg_far)
    rhsf_ref[1, 0:LANE] = jnp.full((LANE, QCOLS), NEG_INF, BF16)
    rhst_ref[0:LANE] = per_head(neg_tail)
    rhsf_ref[0, LANE:2 * LANE] = qt
    rhsf_ref[1, LANE:2 * LANE] = qt
    rhst_ref[LANE:2 * LANE] = qt

    m_ref[...] = jnp.full(m_ref.shape, M_INIT, F32)
    acc_ref[...] = jnp.zeros(acc_ref.shape, F32)
    last_far = jnp.maximum(n_far - 1, 0)

    def far_logits(c):
        r0 = pl.multiple_of(FAR_KEYS * jnp.minimum(c, last_far), FAR_KEYS)
        rhs = rhsf_ref[(c >= n_far).astype(jnp.int32)]
        s = _dot(ks_ref[0, 0, pl.ds(r0, FAR_KEYS), :], rhs)
        return s, jnp.max(s, axis=0, keepdims=True)

    def far_softmax(c, s_ref, mx_ref):
        tile0 = (FAR_TILES // 2) * jnp.minimum(c, last_far)
        m_old = m_ref[...]
        m_new = jnp.maximum(m_old, mx_ref[...])
        p = jnp.exp2(s_ref[...] - m_new).astype(BF16)
        pv = jnp.zeros(acc_ref.shape, F32)
        for t in range(FAR_TILES // 2):
            pv = pv + _dot(vst2_ref[0, 0, tile0 + t], p[2 * t * KEY_TILE:2 * (t + 1) * KEY_TILE])
        acc_ref[...] = jnp.exp2(m_old - m_new) * acc_ref[...] + pv
        m_ref[...] = m_new

    bufs = tuple(zip(sbuf_refs, mbuf_refs))
    n_buf = len(bufs)
    for u in range(FAR_AHEAD):
        bufs[u][0][...], bufs[u][1][...] = far_logits(u)

    kk = lax.broadcasted_iota(jnp.int32, (KEY_TILE, QCOLS), 0)
    qq = lax.broadcasted_iota(jnp.int32, (KEY_TILE, QCOLS), 1) & (Q_BLOCK - 1)
    edge = jnp.where(kk > qq, 0.0, NEG_INF)
    win_tiles = band_logits(kw_ref, qt, [edge] + [None] * (BAND_TILES - 3) + [bt1, bt0], HEAD_DIM)
    ow = band_softmax(win_tiles, vwt_ref, jnp.full(m_ref.shape, M_INIT, F32), jnp.zeros(acc_ref.shape, F32))

    n_trips = jnp.maximum(n_far - FAR_AHEAD + n_buf - 1, 0) // n_buf

    def far_ring(j, carry):
        for u in range(n_buf):
            c = n_buf * j + u
            ahead = far_logits(c + FAR_AHEAD)
            far_softmax(c, *bufs[u])
            ahead_s, ahead_mx = bufs[(u + FAR_AHEAD) % n_buf]
            ahead_s[...], ahead_mx[...] = ahead
        return carry

    lax.fori_loop(0, n_trips, far_ring, 0)

    tail_tiles = band_logits(ks_ref, rhst_ref[...], [None] * (BAND_TILES - 2) + [bt1, bt0], LANE + HEAD_DIM)
    for u in range(FAR_AHEAD):
        far_softmax(n_buf * n_trips + u, *bufs[u])
    os_ = band_softmax(tail_tiles, vst_ref, m_ref[...], acc_ref[...])

    def gate(branch):
        rows = [ngt_ref[0, pl.ds(NSA_HPG * N_GATES * grp + N_GATES * h + branch, 1), :] for h in range(NSA_HPG)]
        return _sigmoid(jnp.concatenate(rows, axis=1))

    o = (gate(0) * oc + gate(1) * os_ + gate(2) * ow).astype(BF16)
    heads = jnp.concatenate([o[:, h * Q_BLOCK:(h + 1) * Q_BLOCK] for h in range(NSA_HPG)], axis=0)
    o_ref[0] = _dot_nt(eye, heads).astype(BF16)


def _nsa_call(q, ngt, kc, vct, ovt, bct, ks, vst, vst2, kw, vwt, bt0, bt1):
    B, _, S, _ = q.shape
    G = NSA_KV_GROUPS
    nq = S // Q_BLOCK
    per_bg = lambda a: pl.BlockSpec((1, 1) + a.shape[2:], lambda b, g, i: (b, g) + (0,) * (a.ndim - 2))
    per_g = lambda a: pl.BlockSpec((1,) + a.shape[1:], lambda b, g, i: (g,) + (0,) * (a.ndim - 1))
    in_specs = [
        pl.BlockSpec((1, NSA_HPG, Q_BLOCK, LANE), lambda b, g, i: (b, g, i, 0)),
        pl.BlockSpec((1, LANE, Q_BLOCK), lambda b, g, i: (b, 0, i)),
        per_bg(kc), per_bg(vct), _full(ovt.shape), per_g(bct),
        per_bg(ks), per_bg(vst), per_bg(vst2), per_bg(kw), per_bg(vwt), per_g(bt0), per_g(bt1),
    ]
    scratch = [
        pltpu.VMEM((CMP_ROWS, QCOLS), F32),
        pltpu.VMEM((2, 2 * LANE, QCOLS), BF16),
        pltpu.VMEM((2 * LANE, QCOLS), BF16),
        pltpu.VMEM((1, QCOLS), F32),
        pltpu.VMEM((V_ROWS, QCOLS), F32),
    ]
    scratch += [pltpu.VMEM((FAR_KEYS, QCOLS), F32)] * FAR_BUFS
    scratch += [pltpu.VMEM((1, QCOLS), F32)] * FAR_BUFS
    return pl.pallas_call(
        _nsa_kernel, grid=(B, G, nq), in_specs=in_specs,
        out_specs=pl.BlockSpec((1, Q_BLOCK, NSA_HPG * HEAD_DIM), lambda b, g, i: (b, i, g)),
        out_shape=jax.ShapeDtypeStruct((B, S, NSA_WIDTH), BF16),
        scratch_shapes=scratch, compiler_params=_params(3), name="sparse_attention",
    )(q, ngt, kc, vct, ovt, bct, ks, vst, vst2, kw, vwt, bt0, bt1)


def _merge_kernel(h_ref, a_ref, o_ref, wmg_ref, bmg_ref, wa_ref, wb_ref, wout_ref, lng_ref, lnb_ref, out_ref):
    h = h_ref[0]
    mg = _dot(h.astype(BF16), wmg_ref[...]) + bmg_ref[...]
    y = (_sigmoid(mg[:, :D_MODEL]) * _dot(a_ref[0], wa_ref[...])
         + _sigmoid(mg[:, D_MODEL:]) * _dot(o_ref[0], wb_ref[...]))
    m = _dot(y.astype(BF16), wout_ref[...])
    out_ref[0] = _layer_norm(ALPHA * h + m, lng_ref[...], lnb_ref[...])


def _merge_call(h, a, o, p):
    B, S, _ = h.shape
    tm = ROW_TILE
    row = lambda w: pl.BlockSpec((1, tm, w), lambda b, i: (b, i, 0))
    ws = (p["w_mg"], p["b_mg"], p["w_proj_a"], p["w_proj_b"], p["w_out"], p["ln1_g"], p["ln1_b"])
    return pl.pallas_call(
        _merge_kernel, grid=(B, S // tm),
        in_specs=[row(D_MODEL), row(GM_WIDTH), row(NSA_WIDTH)] + [_full(a.shape) for a in ws],
        out_specs=row(D_MODEL), out_shape=jax.ShapeDtypeStruct((B, S, D_MODEL), F32),
        compiler_params=_params(2), name="merge_norm")(h, a, o, *ws)


def _ffn_kernel(h_ref, wup_ref, cw_ref, wdown_ref, lng_ref, lnb_ref, out_ref, carry_ref):
    tm = h_ref.shape[1]

    @pl.when(pl.program_id(1) == 0)
    def _():
        carry_ref[...] = jnp.zeros(carry_ref.shape, F32)

    h = h_ref[0]
    hb = h.astype(BF16)
    row8 = lax.broadcasted_iota(jnp.int32, (8, FF_CHUNK), 0)

    def up_proj(c):
        return _dot(hb, wup_ref[c]), _dot(hb, wup_ref[N_FF + c])

    def conv(up, idx):
        prev = carry_ref[idx]
        carry_ref[idx] = up[tm - 8:tm]
        cw = cw_ref[idx]
        out = cw[3:4] + cw[2:3] * up
        for shift in (1, 2):
            rolled = pltpu.roll(up, shift, 0)
            head = jnp.where(row8 < shift, pltpu.roll(prev, shift, 0), rolled[0:8])
            out = out + cw[2 - shift:3 - shift] * jnp.concatenate([head, rolled[8:]], axis=0)
        return out

    f = jnp.zeros((tm, D_MODEL), F32)
    ups = up_proj(0)
    for c in range(N_FF):
        g_up, v_up = ups
        if c + 1 < N_FF:
            ups = up_proj(c + 1)
        gate = conv(g_up, c)
        act = (gate * _sigmoid(gate) * conv(v_up, N_FF + c)).astype(BF16)
        f = f + _dot(act, wdown_ref[c])
    out_ref[0] = _layer_norm(ALPHA * h + f, lng_ref[...], lnb_ref[...])


def _ffn_call(h, p):
    B, S, _ = h.shape
    tm = ROW_TILE
    row = pl.BlockSpec((1, tm, D_MODEL), lambda b, i: (b, i, 0))
    ws = (p["ffn_w_up"], p["ffn_cw"], p["ffn_w_down"], p["ln2_g"], p["ln2_b"])
    return pl.pallas_call(
        _ffn_kernel, grid=(B, S // tm),
        in_specs=[row] + [_full(a.shape) for a in ws],
        out_specs=row, out_shape=jax.ShapeDtypeStruct((B, S, D_MODEL), F32),
        scratch_shapes=[pltpu.VMEM((2 * N_FF, 8, FF_CHUNK), F32)],
        compiler_params=_params(2), name="conv_ffn_norm")(h, *ws)


def _t5_bucket(n):
    max_exact = NUM_BUCKETS // 2
    log_ratio = jnp.log(jnp.maximum(n, 1).astype(F32) / max_exact) / math.log(MAX_DISTANCE / max_exact)
    large = jnp.minimum(max_exact + (log_ratio * (NUM_BUCKETS - max_exact)).astype(jnp.int32), NUM_BUCKETS - 1)
    return jnp.where(n < max_exact, n, large)


def _bias_tiles(rel_bias):
    G, H = NSA_KV_GROUPS, NSA_HPG
    tab = rel_bias[_t5_bucket(jnp.arange(2 * KEY_TILE, dtype=jnp.int32))] - rel_bias[NUM_BUCKETS - 1]
    tab = tab * LOG2E

    def tile(dist):
        t = jnp.where((dist >= 0)[..., None], tab[np.clip(dist, 0, 2 * KEY_TILE - 1)], NEG_INF)
        rows = dist.shape[0]
        return t.reshape(rows, Q_BLOCK, G, H).transpose(2, 0, 3, 1).reshape(G, rows, H * Q_BLOCK)

    ki = np.arange(KEY_TILE)[:, None]
    qi = np.arange(Q_BLOCK)[None, :]
    bt0 = tile(qi - ki)
    bt1 = tile(KEY_TILE + qi - ki)
    rr = np.arange(CMP_BAND)[:, None]
    bct = tile(qi + (CMP_STRIDE * CMP_PAD - CMP_BLOCK + 1) - CMP_STRIDE * rr)
    return bt0, bt1, bct


def _overlap_t(seq):
    n_cmp = seq // CMP_STRIDE - 1
    n_blk = seq // SEL_BLOCK
    cs = np.arange(n_cmp)[:, None] * CMP_STRIDE
    ss = np.arange(n_blk)[None, :] * SEL_BLOCK
    ov = np.maximum(np.minimum(cs + CMP_BLOCK, ss + SEL_BLOCK) - np.maximum(cs, ss), 0) / CMP_BLOCK
    out = np.zeros((LANE, CMP_ROWS), np.float32)
    out[:n_blk, CMP_PAD:CMP_PAD + n_cmp] = ov.T
    return jnp.asarray(out, BF16)


def _pad_cols(w, width):
    return jnp.pad(w, ((0, 0), (0, width - w.shape[1])))


def _layer_params(l, w_in, b_in, gm_ln_g, gm_ln_b, gm_ws, gm_bs, cmp_pos, cmp_w1, cmp_b1, cmp_w2,
                  w_proj_a, w_proj_b, w_out, ln1_g, ln1_b, ffn_w_up, ffn_conv_w, ffn_conv_b, ffn_w_down,
                  ln2_g, ln2_b):
    G = NSA_KV_GROUPS
    w, b = w_in[l], b_in[l]
    o_q = 2 * GM_WIDTH
    o_kc = o_q + NSA_WIDTH
    o_vc, o_ks, o_vs, o_kw, o_vw = (o_kc + KV_WIDTH * k for k in range(1, 6))
    o_ng = o_vw + KV_WIDTH
    n_ng = NSA_HEADS * N_GATES
    o_mg = o_ng + n_ng
    seg = lambda a, o, n: a[..., o:o + n]

    wq = jnp.pad(seg(w, o_q, NSA_WIDTH).reshape(D_MODEL, NSA_HEADS, HEAD_DIM),
                 ((0, 0), (0, 0), (0, LANE - HEAD_DIM))).reshape(D_MODEL, NSA_HEADS * LANE)
    bq = jnp.pad(seg(b, o_q, NSA_WIDTH).reshape(NSA_HEADS, HEAD_DIM), ((0, 0), (0, LANE - HEAD_DIM)))
    bq = bq.at[:, HEAD_DIM].set(NEG_INF / Q_SCALE).reshape(1, NSA_HEADS * LANE)

    def k_groups(o):
        return [_pad_cols(seg(w, o + g * HEAD_DIM, HEAD_DIM), LANE) for g in range(G)], \
               [jnp.pad(seg(b, o + g * HEAD_DIM, HEAD_DIM), (0, LANE - HEAD_DIM)) for g in range(G)]

    wks, bks = k_groups(o_ks)
    wkw, bkw = k_groups(o_kw)
    w_r = jnp.concatenate([seg(w, o_kc, KV_WIDTH), seg(w, o_vc, KV_WIDTH), *wks, *wkw,
                           _pad_cols(seg(w, o_ng, n_ng), LANE)], axis=1)
    b_r = jnp.concatenate([seg(b, o_kc, KV_WIDTH), seg(b, o_vc, KV_WIDTH), *bks, *bkw,
                           jnp.pad(seg(b, o_ng, n_ng), (0, LANE - n_ng))])

    def v_rows(o):
        wt = jnp.pad(seg(w, o, HEAD_DIM).T, ((0, V_ROWS - HEAD_DIM), (0, 0)))
        bt = jnp.pad(seg(b, o, HEAD_DIM), (0, V_ROWS - HEAD_DIM)).at[HEAD_DIM].set(1.0)
        return wt, bt

    vparts = [v_rows(o + g * HEAD_DIM) for o in (o_vs, o_vw) for g in range(G)]
    w_vt = jnp.concatenate([wt for wt, _ in vparts], axis=0)
    b_vt = jnp.concatenate([bt for _, bt in vparts])[:, None]

    causal = jnp.tril(jnp.ones((GM_CHUNK, GM_CHUNK), F32))
    gw = GM_WIDTH // GM_GROUPS
    bs = gm_bs[l]
    bsb = jnp.concatenate([jnp.broadcast_to(bs[0::2, :, None], (GM_GROUPS // 2, GM_CHUNK, gw)),
                           jnp.broadcast_to(bs[1::2, :, None], (GM_GROUPS // 2, GM_CHUNK, gw))], axis=2)

    w1 = cmp_w1[l].reshape(2, CMP_BLOCK, HEAD_DIM, CMP_HIDDEN)
    w1 = jnp.stack([jnp.pad(w1, ((0, 0), (0, 0), (g * HEAD_DIM, (G - 1 - g) * HEAD_DIM), (0, 0)))
                    for g in range(G)], axis=1)
    w2 = cmp_w2[l]
    up = ffn_w_up[l].reshape(D_MODEL, 2 * N_FF, FF_CHUNK).transpose(1, 0, 2)
    cw = jnp.concatenate([ffn_conv_w[l], ffn_conv_b[l][None], jnp.zeros((8 - CONV_WIDTH - 1, 2 * D_FF), F32)])
    return {
        "w_uv": seg(w, 0, 2 * GM_WIDTH).astype(BF16), "b_uv": seg(b, 0, 2 * GM_WIDTH)[None],
        "gm_ln_g": gm_ln_g[l][None], "gm_ln_b": gm_ln_b[l][None],
        "gm_wc": (gm_ws[l] * causal).astype(BF16), "gm_bsb": bsb,
        "w_q": wq.astype(BF16), "b_q": bq, "w_r": w_r.astype(BF16), "b_r": b_r[None],
        "w_vt": w_vt.astype(BF16), "b_vt": b_vt,
        "cmp_pos": jnp.tile(cmp_pos[l], (1, 1, G)), "cmp_w1": w1.astype(BF16), "cmp_b1": cmp_b1[l][:, None],
        "cmp_w2k": _pad_cols(w2[0], LANE).astype(BF16),
        "cmp_w2vt": jnp.pad(w2[1].T, ((0, V_ROWS - HEAD_DIM), (0, 0))).astype(BF16),
        "w_mg": seg(w, o_mg, 2 * D_MODEL).astype(BF16), "b_mg": seg(b, o_mg, 2 * D_MODEL)[None],
        "w_proj_a": w_proj_a[l].astype(BF16), "w_proj_b": w_proj_b[l].astype(BF16),
        "w_out": w_out[l].astype(BF16), "ln1_g": ln1_g[l][None], "ln1_b": ln1_b[l][None],
        "ffn_w_up": up.astype(BF16),
        "ffn_cw": cw.reshape(8, 2 * N_FF, FF_CHUNK).transpose(1, 0, 2),
        "ffn_w_down": ffn_w_down[l].reshape(N_FF, FF_CHUNK, D_MODEL).astype(BF16),
        "ln2_g": ln2_g[l][None], "ln2_b": ln2_b[l][None],
    }


def kernel(x, w_in, b_in, gm_ln_g, gm_ln_b, gm_ws, gm_bs, cmp_pos, cmp_w1, cmp_b1, cmp_w2, rel_bias,
           w_proj_a, w_proj_b, w_out, ln1_g, ln1_b, ffn_w_up, ffn_conv_w, ffn_conv_b, ffn_w_down,
           ln2_g, ln2_b):
    B, S, _ = x.shape
    assert S % FAR_KEYS == 0 and N_SEL <= S // SEL_BLOCK <= LANE
    assert S // CMP_STRIDE - 1 + CMP_PAD < CMP_ROWS
    bt0, bt1, bct = _bias_tiles(rel_bias)
    ovt = _overlap_t(S)
    h = x
    for l in range(DEPTH):
        p = _layer_params(l, w_in, b_in, gm_ln_g, gm_ln_b, gm_ws, gm_bs, cmp_pos, cmp_w1, cmp_b1, cmp_w2,
                          w_proj_a, w_proj_b, w_out, ln1_g, ln1_b, ffn_w_up, ffn_conv_w, ffn_conv_b,
                          ffn_w_down, ln2_g, ln2_b)
        a, q, kvc, ks, kw, vst, vst2, vwt, ngt = _proj_call(h, p)
        kc, vct = _compress_call(kvc, p)
        o = _nsa_call(q, ngt, kc, vct, ovt, bct, ks, vst, vst2, kw, vwt, bt0, bt1)
        h = _merge_call(h, a, o, p)
        h = _ffn_call(h, p)
    return h
```

```python
import math

import jax
import jax.numpy as jnp
import numpy as np
from jax import lax
from jax.experimental import pallas as pl
from jax.experimental.pallas import tpu as pltpu

D_MODEL = 1024
DEPTH = 2
GM_WIDTH = D_MODEL // 2
GM_GROUPS = 8
GM_CHUNK = 128
NSA_HEADS = 8
NSA_KV_GROUPS = 2
NSA_HPG = NSA_HEADS // NSA_KV_GROUPS
HEAD_DIM = 64
NSA_WIDTH = NSA_HEADS * HEAD_DIM
KV_WIDTH = NSA_KV_GROUPS * HEAD_DIM
N_GATES = 3
CMP_BLOCK = 32
CMP_STRIDE = 16
CMP_HIDDEN = 256
SEL_BLOCK = 64
N_SEL = 16
WINDOW = 512
Q_BLOCK = 128
NUM_BUCKETS = 32
MAX_DISTANCE = 128
D_FF = 2816
CONV_WIDTH = 3
ALPHA = (2.0 * DEPTH) ** 0.25
LN_EPS = 1e-5
FORCED_SCORE = 1e6
NEG_INF = -1e30
M_INIT = -1e29
LOG2E = math.log2(math.e)
Q_SCALE = HEAD_DIM ** -0.5 * LOG2E

LANE = 128
QCOLS = NSA_HPG * Q_BLOCK
KEY_TILE = 128
FAR_TILES = 4
FAR_KEYS = FAR_TILES * KEY_TILE
FAR_AHEAD = 2
FAR_BUFS = 2
BAND_TILES = WINDOW // KEY_TILE + 1
PAD_TILES = BAND_TILES - 1
CMP_PAD = 8
CMP_ROWS = 640
CMP_BAND = 16
V_ROWS = 80
FF_CHUNK = 256
N_FF = D_FF // FF_CHUNK
ROW_TILE = 256
VMEM_LIMIT = 56 * 1024 * 1024

F32 = jnp.float32
BF16 = jnp.bfloat16
NT_DIMS = (((1,), (1,)), ((), ()))


def _dot(a, b):
    return jnp.dot(a, b, preferred_element_type=F32)


def _dot_nt(a, b):
    return lax.dot_general(a, b, NT_DIMS, preferred_element_type=F32)


def _layer_norm(x, g, b):
    mu = jnp.mean(x, axis=-1, keepdims=True)
    xc = x - mu
    var = jnp.mean(xc * xc, axis=-1, keepdims=True)
    return xc * lax.rsqrt(var + LN_EPS) * g + b


def _gelu_tanh(x):
    c = math.sqrt(2.0 / math.pi)
    return x * (0.5 * (1.0 + jnp.tanh(c * (x + 0.044715 * (x * x * x)))))


def _sigmoid(x):
    return 1.0 / (1.0 + jnp.exp(-x))


def _params(n_axes):
    return pltpu.CompilerParams(dimension_semantics=("arbitrary",) * n_axes,
                                vmem_limit_bytes=VMEM_LIMIT)


def _full(shape):
    n = len(shape)
    return pl.BlockSpec(shape, lambda *_: (0,) * n)


def _proj_kernel(h_ref, wuv_ref, buv_ref, lng_ref, lnb_ref, wc_ref, bsb_ref, wq_ref, bq_ref,
                 wr_ref, br_ref, wvt_ref, bvt_ref,
                 a_ref, q_ref, kvc_ref, ks_ref, kw_ref, vst_ref, vst2_ref, vwt_ref, ngt_ref):
    tm = h_ref.shape[1]
    G = NSA_KV_GROUPS
    hb = h_ref[0].astype(BF16)

    uv = _dot(hb, wuv_ref[...]) + buv_ref[...]
    u = _gelu_tanh(uv[:, :GM_WIDTH])
    v = _layer_norm(_gelu_tanh(uv[:, GM_WIDTH:]), lng_ref[...], lnb_ref[...]).astype(BF16)
    lane = lax.broadcasted_iota(jnp.int32, (GM_CHUNK, LANE), 1)
    group_w = GM_WIDTH // GM_GROUPS
    for ch in range(tm // GM_CHUNK):
        rows = slice(ch * GM_CHUNK, (ch + 1) * GM_CHUNK)
        for pr in range(GM_GROUPS // 2):
            cols = slice(pr * LANE, (pr + 1) * LANE)
            x = v[rows, cols]
            sv = jnp.where(lane < group_w, _dot(wc_ref[2 * pr], x), _dot(wc_ref[2 * pr + 1], x))
            a_ref[0, rows, cols] = (u[rows, cols] * (sv + bsb_ref[pr])).astype(BF16)

    qq = (_dot(hb, wq_ref[...]) + bq_ref[...]) * Q_SCALE
    for hh in range(NSA_HEADS):
        q_ref[0, hh] = qq[:, hh * LANE:(hh + 1) * LANE].astype(BF16)

    r = _dot(hb, wr_ref[...]) + br_ref[...]
    kvc_ref[0, 0] = r[:, 0:LANE]
    kvc_ref[0, 1] = r[:, LANE:2 * LANE]
    token = pl.program_id(1) * tm + lax.broadcasted_iota(jnp.int32, (tm, LANE), 0)
    sel_block = lax.shift_right_logical(token, int(math.log2(SEL_BLOCK)))
    onehot = (lax.broadcasted_iota(jnp.int32, (tm, LANE), 1) == sel_block).astype(BF16)
    for g in range(G):
        ks_ref[0, g, :, 0:LANE] = onehot
        ks_ref[0, g, :, LANE:2 * LANE] = r[:, (2 + g) * LANE:(3 + g) * LANE].astype(BF16)
        kw_ref[0, g] = r[:, (4 + g) * LANE:(5 + g) * LANE].astype(BF16)
    ngt_ref[0] = r[:, 6 * LANE:7 * LANE].T

    vt = (_dot_nt(wvt_ref[...], hb) + bvt_ref[...]).astype(BF16)
    for g in range(G):
        sel_rows = slice(g * V_ROWS, (g + 1) * V_ROWS)
        win_rows = slice((G + g) * V_ROWS, (G + g + 1) * V_ROWS)
        vst2_ref[0, g, 0] = vt[sel_rows]
        for t in range(tm // KEY_TILE):
            cols = slice(t * KEY_TILE, (t + 1) * KEY_TILE)
            vst_ref[0, g, t] = vt[sel_rows, cols]
            vwt_ref[0, g, t] = vt[win_rows, cols]


def _proj_call(h, p):
    B, S, _ = h.shape
    tm = 2 * KEY_TILE
    G = NSA_KV_GROUPS
    nt = S // KEY_TILE
    row = lambda w: pl.BlockSpec((1, tm, w), lambda b, i: (b, i, 0))
    lead = lambda n, w: pl.BlockSpec((1, n, tm, w), lambda b, i: (b, 0, i, 0))
    vtile = lambda n, w: pl.BlockSpec((1, G, n, V_ROWS, w), lambda b, i: (b, 0, i, 0, 0))
    out_shape = (
        jax.ShapeDtypeStruct((B, S, GM_WIDTH), BF16),
        jax.ShapeDtypeStruct((B, NSA_HEADS, S, LANE), BF16),
        jax.ShapeDtypeStruct((B, 2, S, LANE), F32),
        jax.ShapeDtypeStruct((B, G, S, 2 * LANE), BF16),
        jax.ShapeDtypeStruct((B, G, S, LANE), BF16),
        jax.ShapeDtypeStruct((B, G, nt, V_ROWS, KEY_TILE), BF16),
        jax.ShapeDtypeStruct((B, G, nt // 2, V_ROWS, 2 * KEY_TILE), BF16),
        jax.ShapeDtypeStruct((B, G, nt, V_ROWS, KEY_TILE), BF16),
        jax.ShapeDtypeStruct((B, LANE, S), F32),
    )
    out_specs = (row(GM_WIDTH), lead(NSA_HEADS, LANE), lead(2, LANE), lead(G, 2 * LANE), lead(G, LANE),
                 vtile(tm // KEY_TILE, KEY_TILE), vtile(1, 2 * KEY_TILE), vtile(tm // KEY_TILE, KEY_TILE),
                 pl.BlockSpec((1, LANE, tm), lambda b, i: (b, 0, i)))
    in_arrays = (h, p["w_uv"], p["b_uv"], p["gm_ln_g"], p["gm_ln_b"], p["gm_wc"], p["gm_bsb"],
                 p["w_q"], p["b_q"], p["w_r"], p["b_r"], p["w_vt"], p["b_vt"])
    in_specs = [row(D_MODEL)] + [_full(a.shape) for a in in_arrays[1:]]
    return pl.pallas_call(
        _proj_kernel, grid=(B, S // tm), in_specs=in_specs, out_specs=out_specs,
        out_shape=out_shape, compiler_params=_params(2), name="proj_gmlp")(*in_arrays)


def _compress_kernel(x_ref, pos_ref, w1_ref, b1_ref, w2k_ref, w2vt_ref, kc_ref, vct_ref):
    kv = pl.program_id(1)
    n = x_ref.shape[2] // CMP_STRIDE
    half = CMP_BLOCK // 2
    pad_row = (lax.broadcasted_iota(jnp.int32, (1, LANE), 1) == HEAD_DIM).astype(F32)
    for g in range(NSA_KV_GROUPS):
        first = jnp.zeros((n, CMP_HIDDEN), F32)
        second = jnp.zeros((n, CMP_HIDDEN), F32)
        for r in range(half):
            xr = x_ref[0, 0, pl.ds(r, n, stride=CMP_STRIDE), :]
            first = first + _dot((xr + pos_ref[0, r:r + 1, :]).astype(BF16), w1_ref[0, g, r])
            second = second + _dot((xr + pos_ref[0, half + r:half + r + 1, :]).astype(BF16),
                                   w1_ref[0, g, half + r])
        hid = first + pltpu.roll(second, n - 1, 0) + b1_ref[0]
        act = (hid * _sigmoid(hid)).astype(BF16)

        @pl.when(kv == 0)
        def _():
            res = _dot(act, w2k_ref[...])
            row = lax.broadcasted_iota(jnp.int32, res.shape, 0)
            res = jnp.where(row == n - 1, pad_row, res)
            kc_ref[0, g] = jnp.concatenate(
                [jnp.broadcast_to(pad_row, (CMP_PAD, LANE)), res,
                 jnp.broadcast_to(pad_row, (CMP_ROWS - CMP_PAD - n, LANE))], axis=0).astype(BF16)

        @pl.when(kv == 1)
        def _():
            vt = _dot_nt(w2vt_ref[...], act)
            col = lax.broadcasted_iota(jnp.int32, vt.shape, 1)
            vt = jnp.where(col == n - 1, 0.0, vt)
            vt = jnp.concatenate([vt, jnp.zeros((V_ROWS, CMP_ROWS - n), F32)], axis=1)
            vct_ref[0, g] = pltpu.roll(vt, CMP_PAD, 1).astype(BF16)


def _compress_call(kvc, p):
    B, _, S, _ = kvc.shape
    G = NSA_KV_GROUPS
    per_kv = lambda a: pl.BlockSpec((1,) + a.shape[1:], lambda b, k: (k,) + (0,) * (a.ndim - 1))
    return pl.pallas_call(
        _compress_kernel, grid=(B, 2),
        in_specs=[pl.BlockSpec((1, 1, S, LANE), lambda b, k: (b, k, 0, 0)),
                  per_kv(p["cmp_pos"]), per_kv(p["cmp_w1"]), per_kv(p["cmp_b1"]),
                  _full(p["cmp_w2k"].shape), _full(p["cmp_w2vt"].shape)],
        out_specs=(pl.BlockSpec((1, G, CMP_ROWS, LANE), lambda b, k: (b, 0, 0, 0)),
                   pl.BlockSpec((1, G, V_ROWS, CMP_ROWS), lambda b, k: (b, 0, 0, 0))),
        out_shape=(jax.ShapeDtypeStruct((B, G, CMP_ROWS, LANE), BF16),
                   jax.ShapeDtypeStruct((B, G, V_ROWS, CMP_ROWS), BF16)),
        compiler_params=_params(2), name="kv_compress",
    )(kvc, p["cmp_pos"], p["cmp_w1"], p["cmp_b1"], p["cmp_w2k"], p["cmp_w2vt"])


def _nsa_kernel(q_ref, ngt_ref, kc_ref, vct_ref, ovt_ref, bct_ref, ks_ref, vst_ref, vst2_ref, kw_ref, vwt_ref,
                bt0_ref, bt1_ref, o_ref, sc_ref, rhsf_ref, rhst_ref, m_ref, acc_ref, *ring_refs):
    sbuf_refs, mbuf_refs = ring_refs[:FAR_BUFS], ring_refs[FAR_BUFS:]
    grp = pl.program_id(1)
    i = pl.program_id(2)
    n_blk = LANE
    blk = lax.broadcasted_iota(jnp.int32, (n_blk, Q_BLOCK), 0)
    qi = lax.broadcasted_iota(jnp.int32, (n_blk, Q_BLOCK), 1)
    eye = (blk == qi).astype(BF16)
    qt = jnp.concatenate([_dot_nt(eye, q_ref[0, h]) for h in range(NSA_HPG)], axis=1).astype(BF16)
    bt0 = bt0_ref[0]
    bt1 = bt1_ref[0]


    def band_logits(k_ref, rhs, extra, flag_col):
        width = k_ref.shape[3]
        before_start = (lax.broadcasted_iota(jnp.int32, (KEY_TILE, width), 1) == flag_col).astype(BF16)
        tiles = []
        for t in range(BAND_TILES):
            tau = i - PAD_TILES + t
            row0 = pl.multiple_of(jnp.maximum(tau, 0) * KEY_TILE, KEY_TILE)
            k = k_ref[0, 0, pl.ds(row0, KEY_TILE), :]
            if t < PAD_TILES:
                k = jnp.where(jnp.broadcast_to(tau, k.shape) >= 0, k, before_start)
            s = _dot(k, rhs)
            tiles.append(s if extra[t] is None else s + extra[t])
        return tiles

    def band_softmax(tiles, vt_ref, m_old, acc_old):
        m_new = m_old
        for s in tiles:
            m_new = jnp.maximum(m_new, jnp.max(s, axis=0, keepdims=True))
        acc = jnp.exp2(m_old - m_new) * acc_old
        for t, s in enumerate(tiles):
            vt = vt_ref[0, 0, jnp.maximum(i - PAD_TILES + t, 0)]
            acc = acc + _dot(vt, jnp.exp2(s - m_new).astype(BF16))
        return acc[0:HEAD_DIM] / acc[HEAD_DIM:HEAD_DIM + 1]

    sc_ref[...] = _dot(kc_ref[0, 0], qt)
    band = pl.multiple_of(8 * i, 8)
    sc_ref[pl.ds(band, CMP_BAND), :] += bct_ref[0]
    s = sc_ref[...]
    key = lax.broadcasted_iota(jnp.int32, s.shape, 0)
    s = jnp.where(key >= 8 * i + CMP_BAND, NEG_INF, s)
    m = jnp.maximum(jnp.max(s, axis=0, keepdims=True), M_INIT)
    p = jnp.exp2(s - m)
    l = jnp.sum(p, axis=0, keepdims=True)
    r = jnp.where(l > 0.0, 1.0 / l, 0.0)
    oc = _dot(vct_ref[0, 0], p.astype(BF16))[0:HEAD_DIM] * r

    pn = p * r
    psum = pn[:, 0:Q_BLOCK]
    for h in range(1, NSA_HPG):
        psum = psum + pn[:, h * Q_BLOCK:(h + 1) * Q_BLOCK]
    p_hi = psum.astype(BF16)
    p_lo = (psum - p_hi.astype(F32)).astype(BF16)
    imp = _dot(ovt_ref[...], p_hi) + _dot(ovt_ref[...], p_lo)

    cur = (Q_BLOCK // SEL_BLOCK) * i + (qi >= SEL_BLOCK).astype(jnp.int32)
    forced = (blk == 0) | (blk == cur) | (blk == cur - 1)
    taken = -3e38
    val = jnp.where(forced, taken, jnp.where(blk > cur, NEG_INF, imp))
    blk_f = blk.astype(F32)

    def pick(_, carry):
        val, sel = carry
        mx = jnp.max(val, axis=0, keepdims=True)
        first = jnp.min(jnp.where(val == mx, blk_f, 1e9), axis=0, keepdims=True)
        hit = blk_f == first
        return jnp.where(hit, taken, val), jnp.where(hit, 1.0, sel)

    _, sel = lax.fori_loop(0, N_SEL - 3, pick, (val, forced.astype(F32)))

    n_far = jnp.maximum(i - 1, 0) // FAR_TILES
    neg_far = jnp.where((sel > 0.5) & (blk <= cur), 0.0, NEG_INF)
    neg_tail = jnp.where(blk >= n_far * (FAR_KEYS // SEL_BLOCK), neg_far, NEG_INF)
    per_head = lambda x: jnp.concatenate([x.astype(BF16)] * NSA_HPG, axis=1)
    rhsf_ref[0, 0:LANE] = per_head(neg_far)
    rhsf_ref[1, 0:LANE] = jnp.full((LANE, QCOLS), NEG_INF, BF16)
    rhst_ref[0:LANE] = per_head(neg_tail)
    rhsf_ref[0, LANE:2 * LANE] = qt
    rhsf_ref[1, LANE:2 * LANE] = qt
    rhst_ref[LANE:2 * LANE] = qt

    m_ref[...] = jnp.full(m_ref.shape, M_INIT, F32)
    acc_ref[...] = jnp.zeros(acc_ref.shape, F32)
    last_far = jnp.maximum(n_far - 1, 0)

    def far_logits(c):
        r0 = pl.multiple_of(FAR_KEYS * jnp.minimum(c, last_far), FAR_KEYS)
        rhs = rhsf_ref[(c >= n_far).astype(jnp.int32)]
        s = _dot(ks_ref[0, 0, pl.ds(r0, FAR_KEYS), :], rhs)
        return s, jnp.max(s, axis=0, keepdims=True)

    def far_softmax(c, s_ref, mx_ref):
        tile0 = (FAR_TILES // 2) * jnp.minimum(c, last_far)
        m_old = m_ref[...]
        m_new = jnp.maximum(m_old, mx_ref[...])
        p = jnp.exp2(s_ref[...] - m_new).astype(BF16)
        pv = jnp.zeros(acc_ref.shape, F32)
        for t in range(FAR_TILES // 2):
            pv = pv + _dot(vst2_ref[0, 0, tile0 + t], p[2 * t * KEY_TILE:2 * (t + 1) * KEY_TILE])
        acc_ref[...] = jnp.exp2(m_old - m_new) * acc_ref[...] + pv
        m_ref[...] = m_new

    bufs = tuple(zip(sbuf_refs, mbuf_refs))
    n_buf = len(bufs)
    for u in range(FAR_AHEAD):
        bufs[u][0][...], bufs[u][1][...] = far_logits(u)

    kk = lax.broadcasted_iota(jnp.int32, (KEY_TILE, QCOLS), 0)
    qq = lax.broadcasted_iota(jnp.int32, (KEY_TILE, QCOLS), 1) & (Q_BLOCK - 1)
    edge = jnp.where(kk > qq, 0.0, NEG_INF)
    win_tiles = band_logits(kw_ref, qt, [edge] + [None] * (BAND_TILES - 3) + [bt1, bt0], HEAD_DIM)
    ow = band_softmax(win_tiles, vwt_ref, jnp.full(m_ref.shape, M_INIT, F32), jnp.zeros(acc_ref.shape, F32))

    n_trips = jnp.maximum(n_far - FAR_AHEAD + n_buf - 1, 0) // n_buf

    def far_ring(j, carry):
        for u in range(n_buf):
            c = n_buf * j + u
            ahead = far_logits(c + FAR_AHEAD)
            far_softmax(c, *bufs[u])
            ahead_s, ahead_mx = bufs[(u + FAR_AHEAD) % n_buf]
            ahead_s[...], ahead_mx[...] = ahead
        return carry

    lax.fori_loop(0, n_trips, far_ring, 0)

    tail_tiles = band_logits(ks_ref, rhst_ref[...], [None] * (BAND_TILES - 2) + [bt1, bt0], LANE + HEAD_DIM)
    for u in range(FAR_AHEAD):
        far_softmax(n_buf * n_trips + u, *bufs[u])
    os_ = band_softmax(tail_tiles, vst_ref, m_ref[...], acc_ref[...])

    def gate(branch):
        rows = [ngt_ref[0, pl.ds(NSA_HPG * N_GATES * grp + N_GATES * h + branch, 1), :] for h in range(NSA_HPG)]
        return _sigmoid(jnp.concatenate(rows, axis=1))

    o = (gate(0) * oc + gate(1) * os_ + gate(2) * ow).astype(BF16)
    heads = jnp.concatenate([o[:, h * Q_BLOCK:(h + 1) * Q_BLOCK] for h in range(NSA_HPG)], axis=0)
    o_ref[0] = _dot_nt(eye, heads).astype(BF16)


def _nsa_call(q, ngt, kc, vct, ovt, bct, ks, vst, vst2, kw, vwt, bt0, bt1):
    B, _, S, _ = q.shape
    G = NSA_KV_GROUPS
    nq = S // Q_BLOCK
    per_bg = lambda a: pl.BlockSpec((1, 1) + a.shape[2:], lambda b, g, i: (b, g) + (0,) * (a.ndim - 2))
    per_g = lambda a: pl.BlockSpec((1,) + a.shape[1:], lambda b, g, i: (g,) + (0,) * (a.ndim - 1))
    in_specs = [
        pl.BlockSpec((1, NSA_HPG, Q_BLOCK, LANE), lambda b, g, i: (b, g, i, 0)),
        pl.BlockSpec((1, LANE, Q_BLOCK), lambda b, g, i: (b, 0, i)),
        per_bg(kc), per_bg(vct), _full(ovt.shape), per_g(bct),
        per_bg(ks), per_bg(vst), per_bg(vst2), per_bg(kw), per_bg(vwt), per_g(bt0), per_g(bt1),
    ]
    scratch = [
        pltpu.VMEM((CMP_ROWS, QCOLS), F32),
        pltpu.VMEM((2, 2 * LANE, QCOLS), BF16),
        pltpu.VMEM((2 * LANE, QCOLS), BF16),
        pltpu.VMEM((1, QCOLS), F32),
        pltpu.VMEM((V_ROWS, QCOLS), F32),
    ]
    scratch += [pltpu.VMEM((FAR_KEYS, QCOLS), F32)] * FAR_BUFS
    scratch += [pltpu.VMEM((1, QCOLS), F32)] * FAR_BUFS
    return pl.pallas_call(
        _nsa_kernel, grid=(B, G, nq), in_specs=in_specs,
        out_specs=pl.BlockSpec((1, Q_BLOCK, NSA_HPG * HEAD_DIM), lambda b, g, i: (b, i, g)),
        out_shape=jax.ShapeDtypeStruct((B, S, NSA_WIDTH), BF16),
        scratch_shapes=scratch, compiler_params=_params(3), name="sparse_attention",
    )(q, ngt, kc, vct, ovt, bct, ks, vst, vst2, kw, vwt, bt0, bt1)


def _nsa2_kernel(q_ref, ngt_ref, kc_ref, vct_ref, ovt_ref, bct_ref, ks_ref, vst_ref, vst2_ref, kw_ref, vwt_ref,
                 bt0_ref, bt1_ref, o_ref, sc_ref, rhsf_ref, rhst_ref, m_ref, acc_ref, *ring_refs):
    G = NSA_KV_GROUPS
    groups = range(G)
    sbuf_refs = [ring_refs[g * FAR_BUFS:(g + 1) * FAR_BUFS] for g in groups]
    mbuf_refs = [ring_refs[(G + g) * FAR_BUFS:(G + g + 1) * FAR_BUFS] for g in groups]
    i = pl.program_id(1)
    n_blk = LANE
    blk = lax.broadcasted_iota(jnp.int32, (n_blk, Q_BLOCK), 0)
    qi = lax.broadcasted_iota(jnp.int32, (n_blk, Q_BLOCK), 1)
    eye = (blk == qi).astype(BF16)
    qt = [jnp.concatenate([_dot_nt(eye, q_ref[0, g * NSA_HPG + h]) for h in range(NSA_HPG)], axis=1).astype(BF16)
          for g in groups]
    bt0 = [bt0_ref[g] for g in groups]
    bt1 = [bt1_ref[g] for g in groups]

    def band_logits(k_ref, g, rhs, extra, flag_col):
        width = k_ref.shape[3]
        before_start = (lax.broadcasted_iota(jnp.int32, (KEY_TILE, width), 1) == flag_col).astype(BF16)
        tiles = []
        for t in range(BAND_TILES):
            tau = i - PAD_TILES + t
            row0 = pl.multiple_of(jnp.maximum(tau, 0) * KEY_TILE, KEY_TILE)
            k = k_ref[0, g, pl.ds(row0, KEY_TILE), :]
            if t < PAD_TILES:
                k = jnp.where(jnp.broadcast_to(tau, k.shape) >= 0, k, before_start)
            s = _dot(k, rhs)
            tiles.append(s if extra[t] is None else s + extra[t])
        return tiles

    def band_softmax(tiles, vt_ref, g, m_old, acc_old):
        m_new = m_old
        for s in tiles:
            m_new = jnp.maximum(m_new, jnp.max(s, axis=0, keepdims=True))
        acc = jnp.exp2(m_old - m_new) * acc_old
        for t, s in enumerate(tiles):
            vt = vt_ref[0, g, jnp.maximum(i - PAD_TILES + t, 0)]
            acc = acc + _dot(vt, jnp.exp2(s - m_new).astype(BF16))
        return acc[0:HEAD_DIM] / acc[HEAD_DIM:HEAD_DIM + 1]

    band = pl.multiple_of(8 * i, 8)
    visible = 8 * i + CMP_BAND

    def compressed_stage(rows):
        def softmax(g):
            s = sc_ref[g, 0:rows]
            key = lax.broadcasted_iota(jnp.int32, s.shape, 0)
            s = jnp.where(key >= visible, NEG_INF, s)
            m = jnp.maximum(jnp.max(s, axis=0, keepdims=True), M_INIT)
            p = jnp.exp2(s - m)
            l = jnp.sum(p, axis=0, keepdims=True)
            r = jnp.where(l > 0.0, 1.0 / l, 0.0)
            oc = _dot(vct_ref[0, g, :, 0:rows], p.astype(BF16))[0:HEAD_DIM] * r
            pn = p * r
            psum = pn[:, 0:Q_BLOCK]
            for h in range(1, NSA_HPG):
                psum = psum + pn[:, h * Q_BLOCK:(h + 1) * Q_BLOCK]
            p_hi = psum.astype(BF16)
            p_lo = (psum - p_hi.astype(F32)).astype(BF16)
            ovt = ovt_ref[:, 0:rows]
            return oc, _dot(ovt, p_hi) + _dot(ovt, p_lo)

        def run():
            for g in groups:
                sc_ref[g, 0:rows] = _dot(kc_ref[0, g, 0:rows], qt[g])
                sc_ref[g, pl.ds(band, CMP_BAND), :] += bct_ref[g]
            return tuple(x for g in groups for x in softmax(g))
        return run

    row_options = tuple(range(LANE, CMP_ROWS + 1, LANE))
    flat = lax.switch((visible + LANE - 1) // LANE - 1, [compressed_stage(rows) for rows in row_options])
    oc, imp = flat[0::2], flat[1::2]

    cur = (Q_BLOCK // SEL_BLOCK) * i + (qi >= SEL_BLOCK).astype(jnp.int32)
    forced = (blk == 0) | (blk == cur) | (blk == cur - 1)
    taken = -3e38
    blk_f = blk.astype(F32)

    def pick(_, carry):
        out = []
        for val, sel in carry:
            mx = jnp.max(val, axis=0, keepdims=True)
            first = jnp.min(jnp.where(val == mx, blk_f, 1e9), axis=0, keepdims=True)
            hit = blk_f == first
            out.append((jnp.where(hit, taken, val), jnp.where(hit, 1.0, sel)))
        return tuple(out)

    start = tuple((jnp.where(forced, taken, jnp.where(blk > cur, NEG_INF, imp[g])), forced.astype(F32))
                  for g in groups)
    picked = lax.fori_loop(0, N_SEL - 3, pick, start)

    n_far = jnp.maximum(i - 1, 0) // FAR_TILES
    last_far = jnp.maximum(n_far - 1, 0)
    per_head = lambda x: jnp.concatenate([x.astype(BF16)] * NSA_HPG, axis=1)
    for g in groups:
        neg_far = jnp.where((picked[g][1] > 0.5) & (blk <= cur), 0.0, NEG_INF)
        neg_tail = jnp.where(blk >= n_far * (FAR_KEYS // SEL_BLOCK), neg_far, NEG_INF)
        rhsf_ref[g, 0, 0:LANE] = per_head(neg_far)
        rhsf_ref[g, 1, 0:LANE] = jnp.full((LANE, QCOLS), NEG_INF, BF16)
        rhst_ref[g, 0:LANE] = per_head(neg_tail)
        rhsf_ref[g, 0, LANE:2 * LANE] = qt[g]
        rhsf_ref[g, 1, LANE:2 * LANE] = qt[g]
        rhst_ref[g, LANE:2 * LANE] = qt[g]
        m_ref[g] = jnp.full(m_ref.shape[1:], M_INIT, F32)
        acc_ref[g] = jnp.zeros(acc_ref.shape[1:], F32)

    def far_logits(g, c):
        r0 = pl.multiple_of(FAR_KEYS * jnp.minimum(c, last_far), FAR_KEYS)
        rhs = rhsf_ref[g, (c >= n_far).astype(jnp.int32)]
        s = _dot(ks_ref[0, g, pl.ds(r0, FAR_KEYS), :], rhs)
        return s, jnp.max(s, axis=0, keepdims=True)

    def far_softmax(g, c, s_ref, mx_ref):
        tile0 = (FAR_TILES // 2) * jnp.minimum(c, last_far)
        m_old = m_ref[g]
        m_new = jnp.maximum(m_old, mx_ref[...])
        p = jnp.exp2(s_ref[...] - m_new).astype(BF16)
        pv = jnp.zeros(acc_ref.shape[1:], F32)
        for t in range(FAR_TILES // 2):
            pv = pv + _dot(vst2_ref[0, g, tile0 + t], p[2 * t * KEY_TILE:2 * (t + 1) * KEY_TILE])
        acc_ref[g] = jnp.exp2(m_old - m_new) * acc_ref[g] + pv
        m_ref[g] = m_new

    bufs = [tuple(zip(sbuf_refs[g], mbuf_refs[g])) for g in groups]
    for u in range(FAR_AHEAD):
        for g in groups:
            bufs[g][u][0][...], bufs[g][u][1][...] = far_logits(g, u)

    kk = lax.broadcasted_iota(jnp.int32, (KEY_TILE, QCOLS), 0)
    qq = lax.broadcasted_iota(jnp.int32, (KEY_TILE, QCOLS), 1) & (Q_BLOCK - 1)
    edge = jnp.where(kk > qq, 0.0, NEG_INF)
    win_tiles = [band_logits(kw_ref, g, qt[g], [edge] + [None] * (BAND_TILES - 3) + [bt1[g], bt0[g]], HEAD_DIM)
                 for g in groups]
    ow = [band_softmax(win_tiles[g], vwt_ref, g, jnp.full(m_ref.shape[1:], M_INIT, F32),
                       jnp.zeros(acc_ref.shape[1:], F32)) for g in groups]

    n_trips = jnp.maximum(n_far - FAR_AHEAD + FAR_BUFS - 1, 0) // FAR_BUFS

    def far_ring(j, carry):
        for u in range(FAR_BUFS):
            c = FAR_BUFS * j + u
            ahead = [far_logits(g, c + FAR_AHEAD) for g in groups]
            for g in groups:
                far_softmax(g, c, *bufs[g][u])
            for g in groups:
                ahead_s, ahead_mx = bufs[g][(u + FAR_AHEAD) % FAR_BUFS]
                ahead_s[...], ahead_mx[...] = ahead[g]
        return carry

    lax.fori_loop(0, n_trips, far_ring, 0)

    tail_tiles = [band_logits(ks_ref, g, rhst_ref[g], [None] * (BAND_TILES - 2) + [bt1[g], bt0[g]],
                              LANE + HEAD_DIM) for g in groups]
    for u in range(FAR_AHEAD):
        for g in groups:
            far_softmax(g, FAR_BUFS * n_trips + u, *bufs[g][u])
    os_ = [band_softmax(tail_tiles[g], vst_ref, g, m_ref[g], acc_ref[g]) for g in groups]

    for g in groups:
        def gate(branch):
            rows = [ngt_ref[0, NSA_HPG * N_GATES * g + N_GATES * h + branch:
                            NSA_HPG * N_GATES * g + N_GATES * h + branch + 1, :] for h in range(NSA_HPG)]
            return _sigmoid(jnp.concatenate(rows, axis=1))

        o = (gate(0) * oc[g] + gate(1) * os_[g] + gate(2) * ow[g]).astype(BF16)
        heads = jnp.concatenate([o[:, h * Q_BLOCK:(h + 1) * Q_BLOCK] for h in range(NSA_HPG)], axis=0)
        o_ref[0, :, g * NSA_HPG * HEAD_DIM:(g + 1) * NSA_HPG * HEAD_DIM] = _dot_nt(eye, heads).astype(BF16)


def _nsa2_call(q, ngt, kc, vct, ovt, bct, ks, vst, vst2, kw, vwt, bt0, bt1):
    B, _, S, _ = q.shape
    G = NSA_KV_GROUPS
    nq = S // Q_BLOCK
    per_b = lambda a: pl.BlockSpec((1,) + a.shape[1:], lambda b, i: (b,) + (0,) * (a.ndim - 1),
                                   pipeline_mode=pl.Buffered(1))
    in_specs = [
        pl.BlockSpec((1, NSA_HEADS, Q_BLOCK, LANE), lambda b, i: (b, 0, i, 0)),
        pl.BlockSpec((1, LANE, Q_BLOCK), lambda b, i: (b, 0, i)),
        per_b(kc), per_b(vct), _full(ovt.shape), _full(bct.shape),
        per_b(ks), per_b(vst), per_b(vst2), per_b(kw), per_b(vwt), _full(bt0.shape), _full(bt1.shape),
    ]
    scratch = [
        pltpu.VMEM((G, CMP_ROWS, QCOLS), F32),
        pltpu.VMEM((G, 2, 2 * LANE, QCOLS), BF16),
        pltpu.VMEM((G, 2 * LANE, QCOLS), BF16),
        pltpu.VMEM((G, 1, QCOLS), F32),
        pltpu.VMEM((G, V_ROWS, QCOLS), F32),
    ]
    scratch += [pltpu.VMEM((FAR_KEYS, QCOLS), F32)] * (G * FAR_BUFS)
    scratch += [pltpu.VMEM((1, QCOLS), F32)] * (G * FAR_BUFS)
    return pl.pallas_call(
        _nsa2_kernel, grid=(B, nq), in_specs=in_specs,
        out_specs=pl.BlockSpec((1, Q_BLOCK, NSA_WIDTH), lambda b, i: (b, i, 0)),
        out_shape=jax.ShapeDtypeStruct((B, S, NSA_WIDTH), BF16),
        scratch_shapes=scratch, compiler_params=_params(2), name="sparse_attention",
    )(q, ngt, kc, vct, ovt, bct, ks, vst, vst2, kw, vwt, bt0, bt1)


def _merge_kernel(h_ref, a_ref, o_ref, wmg_ref, bmg_ref, wa_ref, wb_ref, wout_ref, lng_ref, lnb_ref, out_ref):
    h = h_ref[0]
    mg = _dot(h.astype(BF16), wmg_ref[...]) + bmg_ref[...]
    y = (_sigmoid(mg[:, :D_MODEL]) * _dot(a_ref[0], wa_ref[...])
         + _sigmoid(mg[:, D_MODEL:]) * _dot(o_ref[0], wb_ref[...]))
    m = _dot(y.astype(BF16), wout_ref[...])
    out_ref[0] = _layer_norm(ALPHA * h + m, lng_ref[...], lnb_ref[...])


def _merge_call(h, a, o, p):
    B, S, _ = h.shape
    tm = ROW_TILE
    row = lambda w: pl.BlockSpec((1, tm, w), lambda b, i: (b, i, 0))
    ws = (p["w_mg"], p["b_mg"], p["w_proj_a"], p["w_proj_b"], p["w_out"], p["ln1_g"], p["ln1_b"])
    return pl.pallas_call(
        _merge_kernel, grid=(B, S // tm),
        in_specs=[row(D_MODEL), row(GM_WIDTH), row(NSA_WIDTH)] + [_full(a.shape) for a in ws],
        out_specs=row(D_MODEL), out_shape=jax.ShapeDtypeStruct((B, S, D_MODEL), F32),
        compiler_params=_params(2), name="merge_norm")(h, a, o, *ws)


def _ffn_kernel(h_ref, wup_ref, cw_ref, wdown_ref, lng_ref, lnb_ref, out_ref, carry_ref):
    tm = h_ref.shape[1]

    @pl.when(pl.program_id(1) == 0)
    def _():
        carry_ref[...] = jnp.zeros(carry_ref.shape, F32)

    h = h_ref[0]
    hb = h.astype(BF16)
    row8 = lax.broadcasted_iota(jnp.int32, (8, FF_CHUNK), 0)

    def up_proj(c):
        cols = lambda idx: slice(idx * FF_CHUNK, (idx + 1) * FF_CHUNK)
        return _dot(hb, wup_ref[:, cols(c)]), _dot(hb, wup_ref[:, cols(N_FF + c)])

    def conv(up, idx):
        prev = carry_ref[idx]
        carry_ref[idx] = up[tm - 8:tm]
        cw = cw_ref[:, idx * FF_CHUNK:(idx + 1) * FF_CHUNK]
        out = cw[3:4] + cw[2:3] * up
        for shift in (1, 2):
            rolled = pltpu.roll(up, shift, 0)
            head = jnp.where(row8 < shift, pltpu.roll(prev, shift, 0), rolled[0:8])
            out = out + cw[2 - shift:3 - shift] * jnp.concatenate([head, rolled[8:]], axis=0)
        return out

    f = jnp.zeros((tm, D_MODEL), F32)
    ups = up_proj(0)
    for c in range(N_FF):
        g_up, v_up = ups
        if c + 1 < N_FF:
            ups = up_proj(c + 1)
        gate = conv(g_up, c)
        act = (gate * _sigmoid(gate) * conv(v_up, N_FF + c)).astype(BF16)
        f = f + _dot(act, wdown_ref[c])
    out_ref[0] = _layer_norm(ALPHA * h + f, lng_ref[...], lnb_ref[...])


def _ffn_call(h, p):
    B, S, _ = h.shape
    tm = ROW_TILE
    row = pl.BlockSpec((1, tm, D_MODEL), lambda b, i: (b, i, 0))
    ws = (p["ffn_w_up"], p["ffn_cw"], p["ffn_w_down"], p["ln2_g"], p["ln2_b"])
    return pl.pallas_call(
        _ffn_kernel, grid=(B, S // tm),
        in_specs=[row] + [_full(a.shape) for a in ws],
        out_specs=row, out_shape=jax.ShapeDtypeStruct((B, S, D_MODEL), F32),
        scratch_shapes=[pltpu.VMEM((2 * N_FF, 8, FF_CHUNK), F32)],
        compiler_params=_params(2), name="conv_ffn_norm")(h, *ws)


def _t5_bucket(n):
    max_exact = NUM_BUCKETS // 2
    log_ratio = jnp.log(jnp.maximum(n, 1).astype(F32) / max_exact) / math.log(MAX_DISTANCE / max_exact)
    large = jnp.minimum(max_exact + (log_ratio * (NUM_BUCKETS - max_exact)).astype(jnp.int32), NUM_BUCKETS - 1)
    return jnp.where(n < max_exact, n, large)


def _bias_tiles(rel_bias):
    G, H = NSA_KV_GROUPS, NSA_HPG
    tab = rel_bias[_t5_bucket(jnp.arange(2 * KEY_TILE, dtype=jnp.int32))] - rel_bias[NUM_BUCKETS - 1]
    tab = tab * LOG2E

    def tile(offset, n, row_step=1):
        d = np.arange(2 * n)
        dist = offset + np.where(d < n, d, d - 2 * n)
        w = jnp.where((dist >= 0)[:, None], tab[np.clip(dist, 0, 2 * KEY_TILE - 1)], NEG_INF)
        m = jnp.broadcast_to(w[None], (n, 2 * n, w.shape[1])).reshape(2 * n * n, -1)[:n * (2 * n - 1)]
        t = m.reshape(n, 2 * n - 1, -1)[::row_step, :Q_BLOCK]
        rows = t.shape[0]
        return t.reshape(rows, Q_BLOCK, G, H).transpose(2, 0, 3, 1).reshape(G, rows, H * Q_BLOCK)

    bt0 = tile(0, KEY_TILE)
    bt1 = tile(KEY_TILE, KEY_TILE)
    bct = tile(CMP_STRIDE * CMP_PAD - CMP_BLOCK + 1, CMP_STRIDE * CMP_BAND, CMP_STRIDE)
    return bt0, bt1, bct


def _overlap_t(seq):
    n_cmp = seq // CMP_STRIDE - 1
    n_blk = seq // SEL_BLOCK
    cs = np.arange(n_cmp)[:, None] * CMP_STRIDE
    ss = np.arange(n_blk)[None, :] * SEL_BLOCK
    ov = np.maximum(np.minimum(cs + CMP_BLOCK, ss + SEL_BLOCK) - np.maximum(cs, ss), 0) / CMP_BLOCK
    out = np.zeros((LANE, CMP_ROWS), np.float32)
    out[:n_blk, CMP_PAD:CMP_PAD + n_cmp] = ov.T
    return jnp.asarray(out, BF16)


def _pad_cols(w, width):
    return jnp.pad(w, ((0, 0), (0, width - w.shape[1])))


def _layer_params(l, w_in, b_in, gm_ln_g, gm_ln_b, gm_ws, gm_bs, cmp_pos, cmp_w1, cmp_b1, cmp_w2,
                  w_proj_a, w_proj_b, w_out, ln1_g, ln1_b, ffn_w_up, ffn_conv_w, ffn_conv_b, ffn_w_down,
                  ln2_g, ln2_b):
    G = NSA_KV_GROUPS
    w, b = w_in[l], b_in[l]
    o_q = 2 * GM_WIDTH
    o_kc = o_q + NSA_WIDTH
    o_vc, o_ks, o_vs, o_kw, o_vw = (o_kc + KV_WIDTH * k for k in range(1, 6))
    o_ng = o_vw + KV_WIDTH
    n_ng = NSA_HEADS * N_GATES
    o_mg = o_ng + n_ng
    seg = lambda a, o, n: a[..., o:o + n]

    wq = jnp.pad(seg(w, o_q, NSA_WIDTH).reshape(D_MODEL, NSA_HEADS, HEAD_DIM),
                 ((0, 0), (0, 0), (0, LANE - HEAD_DIM))).reshape(D_MODEL, NSA_HEADS * LANE)
    bq = jnp.pad(seg(b, o_q, NSA_WIDTH).reshape(NSA_HEADS, HEAD_DIM), ((0, 0), (0, LANE - HEAD_DIM)))
    bq = bq.at[:, HEAD_DIM].set(NEG_INF / Q_SCALE).reshape(1, NSA_HEADS * LANE)

    def k_groups(o):
        return [_pad_cols(seg(w, o + g * HEAD_DIM, HEAD_DIM), LANE) for g in range(G)], \
               [jnp.pad(seg(b, o + g * HEAD_DIM, HEAD_DIM), (0, LANE - HEAD_DIM)) for g in range(G)]

    wks, bks = k_groups(o_ks)
    wkw, bkw = k_groups(o_kw)
    w_r = jnp.concatenate([seg(w, o_kc, KV_WIDTH), seg(w, o_vc, KV_WIDTH), *wks, *wkw,
                           _pad_cols(seg(w, o_ng, n_ng), LANE)], axis=1)
    b_r = jnp.concatenate([seg(b, o_kc, KV_WIDTH), seg(b, o_vc, KV_WIDTH), *bks, *bkw,
                           jnp.pad(seg(b, o_ng, n_ng), (0, LANE - n_ng))])

    def v_rows(o):
        wt = jnp.pad(seg(w, o, HEAD_DIM).T, ((0, V_ROWS - HEAD_DIM), (0, 0)))
        bt = jnp.pad(seg(b, o, HEAD_DIM), (0, V_ROWS - HEAD_DIM)).at[HEAD_DIM].set(1.0)
        return wt, bt

    vparts = [v_rows(o + g * HEAD_DIM) for o in (o_vs, o_vw) for g in range(G)]
    w_vt = jnp.concatenate([wt for wt, _ in vparts], axis=0)
    b_vt = jnp.concatenate([bt for _, bt in vparts])[:, None]

    causal = jnp.tril(jnp.ones((GM_CHUNK, GM_CHUNK), F32))
    gw = GM_WIDTH // GM_GROUPS
    bs = gm_bs[l]
    bsb = jnp.concatenate([jnp.broadcast_to(bs[0::2, :, None], (GM_GROUPS // 2, GM_CHUNK, gw)),
                           jnp.broadcast_to(bs[1::2, :, None], (GM_GROUPS // 2, GM_CHUNK, gw))], axis=2)

    w1 = cmp_w1[l].reshape(2, CMP_BLOCK, HEAD_DIM, CMP_HIDDEN)
    w1 = jnp.stack([jnp.pad(w1, ((0, 0), (0, 0), (g * HEAD_DIM, (G - 1 - g) * HEAD_DIM), (0, 0)))
                    for g in range(G)], axis=1)
    w2 = cmp_w2[l]
    cw = jnp.concatenate([ffn_conv_w[l], ffn_conv_b[l][None], jnp.zeros((8 - CONV_WIDTH - 1, 2 * D_FF), F32)])
    return {
        "w_uv": seg(w, 0, 2 * GM_WIDTH).astype(BF16), "b_uv": seg(b, 0, 2 * GM_WIDTH)[None],
        "gm_ln_g": gm_ln_g[l][None], "gm_ln_b": gm_ln_b[l][None],
        "gm_wc": (gm_ws[l] * causal).astype(BF16), "gm_bsb": bsb,
        "w_q": wq.astype(BF16), "b_q": bq, "w_r": w_r.astype(BF16), "b_r": b_r[None],
        "w_vt": w_vt.astype(BF16), "b_vt": b_vt,
        "cmp_pos": jnp.tile(cmp_pos[l], (1, 1, G)), "cmp_w1": w1.astype(BF16), "cmp_b1": cmp_b1[l][:, None],
        "cmp_w2k": _pad_cols(w2[0], LANE).astype(BF16),
        "cmp_w2vt": jnp.pad(w2[1].T, ((0, V_ROWS - HEAD_DIM), (0, 0))).astype(BF16),
        "w_mg": seg(w, o_mg, 2 * D_MODEL).astype(BF16), "b_mg": seg(b, o_mg, 2 * D_MODEL)[None],
        "w_proj_a": w_proj_a[l].astype(BF16), "w_proj_b": w_proj_b[l].astype(BF16),
        "w_out": w_out[l].astype(BF16), "ln1_g": ln1_g[l][None], "ln1_b": ln1_b[l][None],
        "ffn_w_up": ffn_w_up[l].astype(BF16), "ffn_cw": cw,
        "ffn_w_down": ffn_w_down[l].reshape(N_FF, FF_CHUNK, D_MODEL).astype(BF16),
        "ln2_g": ln2_g[l][None], "ln2_b": ln2_b[l][None],
    }


def kernel(x, w_in, b_in, gm_ln_g, gm_ln_b, gm_ws, gm_bs, cmp_pos, cmp_w1, cmp_b1, cmp_w2, rel_bias,
           w_proj_a, w_proj_b, w_out, ln1_g, ln1_b, ffn_w_up, ffn_conv_w, ffn_conv_b, ffn_w_down,
           ln2_g, ln2_b):
    B, S, _ = x.shape
    assert S % FAR_KEYS == 0 and N_SEL <= S // SEL_BLOCK <= LANE
    assert S // CMP_STRIDE - 1 + CMP_PAD < CMP_ROWS
    bt0, bt1, bct = _bias_tiles(rel_bias)
    ovt = _overlap_t(S)
    h = x
    for l in range(DEPTH):
        p = _layer_params(l, w_in, b_in, gm_ln_g, gm_ln_b, gm_ws, gm_bs, cmp_pos, cmp_w1, cmp_b1, cmp_w2,
                          w_proj_a, w_proj_b, w_out, ln1_g, ln1_b, ffn_w_up, ffn_conv_w, ffn_conv_b,
                          ffn_w_down, ln2_g, ln2_b)
        a, q, kvc, ks, kw, vst, vst2, vwt, ngt = _proj_call(h, p)
        kc, vct = _compress_call(kvc, p)
        o = _nsa2_call(q, ngt, kc, vct, ovt, bct, ks, vst, vst2, kw, vwt, bt0, bt1)
        h = _merge_call(h, a, o, p)
        h = _ffn_call(h, p)
    return h
```

```python
import math

import jax
import jax.numpy as jnp
import numpy as np
from jax import lax
from jax.experimental import pallas as pl
from jax.experimental.pallas import tpu as pltpu

D_MODEL = 1024
DEPTH = 2
GM_WIDTH = D_MODEL // 2
GM_GROUPS = 8
GM_CHUNK = 128
NSA_HEADS = 8
NSA_KV_GROUPS = 2
NSA_HPG = NSA_HEADS // NSA_KV_GROUPS
HEAD_DIM = 64
NSA_WIDTH = NSA_HEADS * HEAD_DIM
KV_WIDTH = NSA_KV_GROUPS * HEAD_DIM
N_GATES = 3
CMP_BLOCK = 32
CMP_STRIDE = 16
CMP_HIDDEN = 256
SEL_BLOCK = 64
N_SEL = 16
WINDOW = 512
Q_BLOCK = 128
NUM_BUCKETS = 32
MAX_DISTANCE = 128
D_FF = 2816
CONV_WIDTH = 3
ALPHA = (2.0 * DEPTH) ** 0.25
LN_EPS = 1e-5
FORCED_SCORE = 1e6
NEG_INF = -1e30
M_INIT = -1e29
LOG2E = math.log2(math.e)
Q_SCALE = HEAD_DIM ** -0.5 * LOG2E

LANE = 128
QCOLS = NSA_HPG * Q_BLOCK
KEY_TILE = 128
FAR_TILES = 4
FAR_KEYS = FAR_TILES * KEY_TILE
FAR_AHEAD = 2
FAR_BUFS = 2
BAND_TILES = WINDOW // KEY_TILE + 1
PAD_TILES = BAND_TILES - 1
CMP_PAD = 8
CMP_ROWS = 640
CMP_BAND = 16
V_ROWS = 80
FF_CHUNK = 256
N_FF = D_FF // FF_CHUNK
FF_AHEAD = 2
ROW_TILE = 256
VMEM_LIMIT = 56 * 1024 * 1024

F32 = jnp.float32
BF16 = jnp.bfloat16
NT_DIMS = (((1,), (1,)), ((), ()))


def _dot(a, b):
    return jnp.dot(a, b, preferred_element_type=F32)


def _dot_nt(a, b):
    return lax.dot_general(a, b, NT_DIMS, preferred_element_type=F32)


def _layer_norm(x, g, b):
    mu = jnp.mean(x, axis=-1, keepdims=True)
    xc = x - mu
    var = jnp.mean(xc * xc, axis=-1, keepdims=True)
    return xc * lax.rsqrt(var + LN_EPS) * g + b


def _gelu_tanh(x):
    c = math.sqrt(2.0 / math.pi)
    return x * (0.5 * (1.0 + jnp.tanh(c * (x + 0.044715 * (x * x * x)))))


def _sigmoid(x):
    return 1.0 / (1.0 + jnp.exp(-x))


def _params(n_axes):
    return pltpu.CompilerParams(dimension_semantics=("arbitrary",) * n_axes,
                                vmem_limit_bytes=VMEM_LIMIT)


def _full(shape):
    n = len(shape)
    return pl.BlockSpec(shape, lambda *_: (0,) * n)


def _proj_kernel(h_ref, wuv_ref, buv_ref, lng_ref, lnb_ref, wc_ref, bsb_ref, wr_ref, br_ref, wt_ref, bt_ref,
                 a_ref, q_ref, kvc_ref, ks_ref, kw_ref, vst_ref, vst2_ref, vwt_ref, ngt_ref):
    tm = h_ref.shape[1]
    G = NSA_KV_GROUPS
    hb = h_ref[0].astype(BF16)

    uv = _dot(hb, wuv_ref[...]) + buv_ref[...]
    r = _dot(hb, wr_ref[...]) + br_ref[...]
    tr = ((_dot_nt(wt_ref[...], hb) + bt_ref[:, 0:1]) * bt_ref[:, 1:2]).astype(BF16)

    u = _gelu_tanh(uv[:, :GM_WIDTH])
    v = _layer_norm(_gelu_tanh(uv[:, GM_WIDTH:]), lng_ref[...], lnb_ref[...]).astype(BF16)
    lane = lax.broadcasted_iota(jnp.int32, (GM_CHUNK, LANE), 1)
    group_w = GM_WIDTH // GM_GROUPS
    for ch in range(tm // GM_CHUNK):
        rows = slice(ch * GM_CHUNK, (ch + 1) * GM_CHUNK)
        for pr in range(GM_GROUPS // 2):
            cols = slice(pr * LANE, (pr + 1) * LANE)
            x = v[rows, cols]
            sv = jnp.where(lane < group_w, _dot(wc_ref[2 * pr], x), _dot(wc_ref[2 * pr + 1], x))
            a_ref[0, rows, cols] = (u[rows, cols] * (sv + bsb_ref[pr])).astype(BF16)

    for hh in range(NSA_HEADS):
        q_ref[0, hh] = tr[hh * LANE:(hh + 1) * LANE]
    vt = tr[NSA_HEADS * LANE:]

    kvc_ref[0, 0] = r[:, 0:LANE]
    kvc_ref[0, 1] = r[:, LANE:2 * LANE]
    token = pl.program_id(1) * tm + lax.broadcasted_iota(jnp.int32, (tm, LANE), 0)
    sel_block = lax.shift_right_logical(token, int(math.log2(SEL_BLOCK)))
    onehot = (lax.broadcasted_iota(jnp.int32, (tm, LANE), 1) == sel_block).astype(BF16)
    for g in range(G):
        ks_ref[0, g, :, 0:LANE] = onehot
        ks_ref[0, g, :, LANE:2 * LANE] = r[:, (2 + g) * LANE:(3 + g) * LANE].astype(BF16)
        kw_ref[0, g] = r[:, (4 + g) * LANE:(5 + g) * LANE].astype(BF16)
    ngt_ref[0] = r[:, 6 * LANE:7 * LANE].T

    for g in range(G):
        sel_rows = slice(g * V_ROWS, (g + 1) * V_ROWS)
        win_rows = slice((G + g) * V_ROWS, (G + g + 1) * V_ROWS)
        vst2_ref[0, g, 0] = vt[sel_rows]
        for t in range(tm // KEY_TILE):
            cols = slice(t * KEY_TILE, (t + 1) * KEY_TILE)
            vst_ref[0, g, t] = vt[sel_rows, cols]
            vwt_ref[0, g, t] = vt[win_rows, cols]


def _proj_call(h, p):
    B, S, _ = h.shape
    tm = 2 * KEY_TILE
    G = NSA_KV_GROUPS
    nt = S // KEY_TILE
    row = lambda w: pl.BlockSpec((1, tm, w), lambda b, i: (b, i, 0))
    lead = lambda n, w: pl.BlockSpec((1, n, tm, w), lambda b, i: (b, 0, i, 0))
    vtile = lambda n, w: pl.BlockSpec((1, G, n, V_ROWS, w), lambda b, i: (b, 0, i, 0, 0))
    out_shape = (
        jax.ShapeDtypeStruct((B, S, GM_WIDTH), BF16),
        jax.ShapeDtypeStruct((B, NSA_HEADS, LANE, S), BF16),
        jax.ShapeDtypeStruct((B, 2, S, LANE), F32),
        jax.ShapeDtypeStruct((B, G, S, 2 * LANE), BF16),
        jax.ShapeDtypeStruct((B, G, S, LANE), BF16),
        jax.ShapeDtypeStruct((B, G, nt, V_ROWS, KEY_TILE), BF16),
        jax.ShapeDtypeStruct((B, G, nt // 2, V_ROWS, 2 * KEY_TILE), BF16),
        jax.ShapeDtypeStruct((B, G, nt, V_ROWS, KEY_TILE), BF16),
        jax.ShapeDtypeStruct((B, LANE, S), F32),
    )
    out_specs = (row(GM_WIDTH), pl.BlockSpec((1, NSA_HEADS, LANE, tm), lambda b, i: (b, 0, 0, i)),
                 lead(2, LANE), lead(G, 2 * LANE), lead(G, LANE),
                 vtile(tm // KEY_TILE, KEY_TILE), vtile(1, 2 * KEY_TILE), vtile(tm // KEY_TILE, KEY_TILE),
                 pl.BlockSpec((1, LANE, tm), lambda b, i: (b, 0, i)))
    in_arrays = (h, p["w_uv"], p["b_uv"], p["gm_ln_g"], p["gm_ln_b"], p["gm_wc"], p["gm_bsb"],
                 p["w_r"], p["b_r"], p["w_t"], p["b_t"])
    in_specs = [row(D_MODEL)] + [_full(a.shape) for a in in_arrays[1:]]
    return pl.pallas_call(
        _proj_kernel, grid=(B, S // tm), in_specs=in_specs, out_specs=out_specs,
        out_shape=out_shape, compiler_params=_params(2), name="proj_gmlp")(*in_arrays)


def _compress_kernel(x_ref, pos_ref, w1_ref, b1_ref, w2k_ref, w2vt_ref, kc_ref, vct_ref):
    kv = pl.program_id(1)
    n = x_ref.shape[2] // CMP_STRIDE
    half = CMP_BLOCK // 2
    pad_row = (lax.broadcasted_iota(jnp.int32, (1, LANE), 1) == HEAD_DIM).astype(F32)
    for g in range(NSA_KV_GROUPS):
        first = jnp.zeros((n, CMP_HIDDEN), F32)
        second = jnp.zeros((n, CMP_HIDDEN), F32)
        for r in range(half):
            xr = x_ref[0, 0, pl.ds(r, n, stride=CMP_STRIDE), :]
            first = first + _dot((xr + pos_ref[0, r:r + 1, :]).astype(BF16), w1_ref[0, g, r])
            second = second + _dot((xr + pos_ref[0, half + r:half + r + 1, :]).astype(BF16),
                                   w1_ref[0, g, half + r])
        hid = first + pltpu.roll(second, n - 1, 0) + b1_ref[0]
        act = (hid * _sigmoid(hid)).astype(BF16)

        @pl.when(kv == 0)
        def _():
            res = _dot(act, w2k_ref[...])
            row = lax.broadcasted_iota(jnp.int32, res.shape, 0)
            res = jnp.where(row == n - 1, pad_row, res)
            kc_ref[0, g] = jnp.concatenate(
                [jnp.broadcast_to(pad_row, (CMP_PAD, LANE)), res,
                 jnp.broadcast_to(pad_row, (CMP_ROWS - CMP_PAD - n, LANE))], axis=0).astype(BF16)

        @pl.when(kv == 1)
        def _():
            vt = _dot_nt(w2vt_ref[...], act)
            col = lax.broadcasted_iota(jnp.int32, vt.shape, 1)
            vt = jnp.where(col == n - 1, 0.0, vt)
            vt = jnp.concatenate([vt, jnp.zeros((V_ROWS, CMP_ROWS - n), F32)], axis=1)
            vct_ref[0, g] = pltpu.roll(vt, CMP_PAD, 1).astype(BF16)


def _compress_call(kvc, p):
    B, _, S, _ = kvc.shape
    G = NSA_KV_GROUPS
    per_kv = lambda a: pl.BlockSpec((1,) + a.shape[1:], lambda b, k: (k,) + (0,) * (a.ndim - 1))
    return pl.pallas_call(
        _compress_kernel, grid=(B, 2),
        in_specs=[pl.BlockSpec((1, 1, S, LANE), lambda b, k: (b, k, 0, 0)),
                  per_kv(p["cmp_pos"]), per_kv(p["cmp_w1"]), per_kv(p["cmp_b1"]),
                  _full(p["cmp_w2k"].shape), _full(p["cmp_w2vt"].shape)],
        out_specs=(pl.BlockSpec((1, G, CMP_ROWS, LANE), lambda b, k: (b, 0, 0, 0)),
                   pl.BlockSpec((1, G, V_ROWS, CMP_ROWS), lambda b, k: (b, 0, 0, 0))),
        out_shape=(jax.ShapeDtypeStruct((B, G, CMP_ROWS, LANE), BF16),
                   jax.ShapeDtypeStruct((B, G, V_ROWS, CMP_ROWS), BF16)),
        compiler_params=_params(2), name="kv_compress",
    )(kvc, p["cmp_pos"], p["cmp_w1"], p["cmp_b1"], p["cmp_w2k"], p["cmp_w2vt"])


def _nsa_kernel(q_ref, ngt_ref, kc_ref, vct_ref, ovt_ref, bct_ref, ks_ref, vst_ref, vst2_ref, kw_ref, vwt_ref,
                 bt0_ref, bt1_ref, o_ref, sc_ref, rhsf_ref, rhst_ref, m_ref, acc_ref, *ring_refs):
    G = NSA_KV_GROUPS
    groups = range(G)
    sbuf_refs = [ring_refs[g * FAR_BUFS:(g + 1) * FAR_BUFS] for g in groups]
    mbuf_refs = [ring_refs[(G + g) * FAR_BUFS:(G + g + 1) * FAR_BUFS] for g in groups]
    i = pl.program_id(1)
    n_blk = LANE
    blk = lax.broadcasted_iota(jnp.int32, (n_blk, Q_BLOCK), 0)
    qi = lax.broadcasted_iota(jnp.int32, (n_blk, Q_BLOCK), 1)
    eye = (blk == qi).astype(BF16)
    qt = [jnp.concatenate([q_ref[0, g * NSA_HPG + h] for h in range(NSA_HPG)], axis=1) for g in groups]
    bt0 = [bt0_ref[g] for g in groups]
    bt1 = [bt1_ref[g] for g in groups]

    def band_logits(k_ref, g, rhs, extra, flag_col):
        width = k_ref.shape[3]
        before_start = (lax.broadcasted_iota(jnp.int32, (KEY_TILE, width), 1) == flag_col).astype(BF16)
        tiles = []
        for t in range(BAND_TILES):
            tau = i - PAD_TILES + t
            row0 = pl.multiple_of(jnp.maximum(tau, 0) * KEY_TILE, KEY_TILE)
            k = k_ref[0, g, pl.ds(row0, KEY_TILE), :]
            if t < PAD_TILES:
                k = jnp.where(jnp.broadcast_to(tau, k.shape) >= 0, k, before_start)
            s = _dot(k, rhs)
            tiles.append(s if extra[t] is None else s + extra[t])
        return tiles

    def band_softmax(tiles, vt_ref, g, m_old, acc_old):
        m_new = m_old
        for s in tiles:
            m_new = jnp.maximum(m_new, jnp.max(s, axis=0, keepdims=True))
        acc = jnp.exp2(m_old - m_new) * acc_old
        for t, s in enumerate(tiles):
            vt = vt_ref[0, g, jnp.maximum(i - PAD_TILES + t, 0)]
            acc = acc + _dot(vt, jnp.exp2(s - m_new).astype(BF16))
        return acc[0:HEAD_DIM] / acc[HEAD_DIM:HEAD_DIM + 1]

    band = pl.multiple_of(8 * i, 8)
    visible = 8 * i + CMP_BAND

    def compressed_stage(rows):
        def softmax(g):
            s = sc_ref[g, 0:rows]
            key = lax.broadcasted_iota(jnp.int32, s.shape, 0)
            s = jnp.where(key >= visible, NEG_INF, s)
            m = jnp.maximum(jnp.max(s, axis=0, keepdims=True), M_INIT)
            p = jnp.exp2(s - m)
            l = jnp.sum(p, axis=0, keepdims=True)
            r = jnp.where(l > 0.0, 1.0 / l, 0.0)
            oc = _dot(vct_ref[0, g, :, 0:rows], p.astype(BF16))[0:HEAD_DIM] * r
            pn = p * r
            psum = pn[:, 0:Q_BLOCK]
            for h in range(1, NSA_HPG):
                psum = psum + pn[:, h * Q_BLOCK:(h + 1) * Q_BLOCK]
            p_hi = psum.astype(BF16)
            p_lo = (psum - p_hi.astype(F32)).astype(BF16)
            ovt = ovt_ref[:, 0:rows]
            return oc, _dot(ovt, p_hi) + _dot(ovt, p_lo)

        def run():
            for g in groups:
                sc_ref[g, 0:rows] = _dot(kc_ref[0, g, 0:rows], qt[g])
                sc_ref[g, pl.ds(band, CMP_BAND), :] += bct_ref[g]
            return tuple(x for g in groups for x in softmax(g))
        return run

    row_options = tuple(range(LANE, CMP_ROWS + 1, LANE))
    flat = lax.switch((visible + LANE - 1) // LANE - 1, [compressed_stage(rows) for rows in row_options])
    oc, imp = flat[0::2], flat[1::2]

    cur = (Q_BLOCK // SEL_BLOCK) * i + (qi >= SEL_BLOCK).astype(jnp.int32)
    forced = (blk == 0) | (blk == cur) | (blk == cur - 1)
    taken = -3e38
    blk_f = blk.astype(F32)
    start = tuple((jnp.where(forced, taken, jnp.where(blk > cur, NEG_INF, imp[g])), forced.astype(F32))
                  for g in groups)

    def select_stage(rows):
        def pick(_, carry):
            out = []
            for val, sel in carry:
                mx = jnp.max(val, axis=0, keepdims=True)
                first = jnp.min(jnp.where(val == mx, blk_f[0:rows], 1e9), axis=0, keepdims=True)
                hit = blk_f[0:rows] == first
                out.append((jnp.where(hit, taken, val), jnp.where(hit, 1.0, sel)))
            return tuple(out)

        def run():
            head = lax.fori_loop(0, N_SEL - 3, pick, tuple((v[0:rows], s[0:rows]) for v, s in start))
            return tuple(jnp.concatenate([sel, jnp.zeros((n_blk - rows, Q_BLOCK), F32)], axis=0)
                         if rows < n_blk else sel for _, sel in head)
        return run

    row_step = n_blk // 4
    row_options = tuple(range(row_step, n_blk + 1, row_step))
    n_cand = (Q_BLOCK // SEL_BLOCK) * (i + 1)
    picked = lax.switch((n_cand + row_step - 1) // row_step - 1, [select_stage(rows) for rows in row_options])

    n_far = jnp.maximum(i - 1, 0) // FAR_TILES
    last_far = jnp.maximum(n_far - 1, 0)
    per_head = lambda x: jnp.concatenate([x.astype(BF16)] * NSA_HPG, axis=1)
    for g in groups:
        neg_far = jnp.where((picked[g] > 0.5) & (blk <= cur), 0.0, NEG_INF)
        neg_tail = jnp.where(blk >= n_far * (FAR_KEYS // SEL_BLOCK), neg_far, NEG_INF)
        rhsf_ref[g, 0, 0:LANE] = per_head(neg_far)
        rhsf_ref[g, 1, 0:LANE] = jnp.full((LANE, QCOLS), NEG_INF, BF16)
        rhst_ref[g, 0:LANE] = per_head(neg_tail)
        rhsf_ref[g, 0, LANE:2 * LANE] = qt[g]
        rhsf_ref[g, 1, LANE:2 * LANE] = qt[g]
        rhst_ref[g, LANE:2 * LANE] = qt[g]
        m_ref[g] = jnp.full(m_ref.shape[1:], M_INIT, F32)
        acc_ref[g] = jnp.zeros(acc_ref.shape[1:], F32)

    def far_logits(g, c):
        r0 = pl.multiple_of(FAR_KEYS * jnp.minimum(c, last_far), FAR_KEYS)
        rhs = rhsf_ref[g, (c >= n_far).astype(jnp.int32)]
        s = _dot(ks_ref[0, g, pl.ds(r0, FAR_KEYS), :], rhs)
        return s, jnp.max(s, axis=0, keepdims=True)

    def far_softmax(g, c, s_ref, mx_ref):
        tile0 = (FAR_TILES // 2) * jnp.minimum(c, last_far)
        m_old = m_ref[g]
        m_new = jnp.maximum(m_old, mx_ref[...])
        p = jnp.exp2(s_ref[...] - m_new).astype(BF16)
        pv = jnp.zeros(acc_ref.shape[1:], F32)
        for t in range(FAR_TILES // 2):
            pv = pv + _dot(vst2_ref[0, g, tile0 + t], p[2 * t * KEY_TILE:2 * (t + 1) * KEY_TILE])
        acc_ref[g] = jnp.exp2(m_old - m_new) * acc_ref[g] + pv
        m_ref[g] = m_new

    bufs = [tuple(zip(sbuf_refs[g], mbuf_refs[g])) for g in groups]
    for u in range(FAR_AHEAD):
        for g in groups:
            bufs[g][u][0][...], bufs[g][u][1][...] = far_logits(g, u)

    kk = lax.broadcasted_iota(jnp.int32, (KEY_TILE, QCOLS), 0)
    qq = lax.broadcasted_iota(jnp.int32, (KEY_TILE, QCOLS), 1) & (Q_BLOCK - 1)
    edge = jnp.where(kk > qq, 0.0, NEG_INF)
    win_tiles = [band_logits(kw_ref, g, qt[g], [edge] + [None] * (BAND_TILES - 3) + [bt1[g], bt0[g]], HEAD_DIM)
                 for g in groups]
    ow = [band_softmax(win_tiles[g], vwt_ref, g, jnp.full(m_ref.shape[1:], M_INIT, F32),
                       jnp.zeros(acc_ref.shape[1:], F32)) for g in groups]

    n_trips = jnp.maximum(n_far - FAR_AHEAD + FAR_BUFS - 1, 0) // FAR_BUFS

    def far_ring(j, carry):
        for u in range(FAR_BUFS):
            c = FAR_BUFS * j + u
            ahead = [far_logits(g, c + FAR_AHEAD) for g in groups]
            for g in groups:
                far_softmax(g, c, *bufs[g][u])
            for g in groups:
                ahead_s, ahead_mx = bufs[g][(u + FAR_AHEAD) % FAR_BUFS]
                ahead_s[...], ahead_mx[...] = ahead[g]
        return carry

    lax.fori_loop(0, n_trips, far_ring, 0)

    tail_tiles = [band_logits(ks_ref, g, rhst_ref[g], [None] * (BAND_TILES - 2) + [bt1[g], bt0[g]],
                              LANE + HEAD_DIM) for g in groups]
    for u in range(FAR_AHEAD):
        for g in groups:
            far_softmax(g, FAR_BUFS * n_trips + u, *bufs[g][u])
    os_ = [band_softmax(tail_tiles[g], vst_ref, g, m_ref[g], acc_ref[g]) for g in groups]

    for g in groups:
        def gate(branch):
            rows = [ngt_ref[0, NSA_HPG * N_GATES * g + N_GATES * h + branch:
                            NSA_HPG * N_GATES * g + N_GATES * h + branch + 1, :] for h in range(NSA_HPG)]
            return _sigmoid(jnp.concatenate(rows, axis=1))

        o = (gate(0) * oc[g] + gate(1) * os_[g] + gate(2) * ow[g]).astype(BF16)
        heads = jnp.concatenate([o[:, h * Q_BLOCK:(h + 1) * Q_BLOCK] for h in range(NSA_HPG)], axis=0)
        o_ref[0, :, g * NSA_HPG * HEAD_DIM:(g + 1) * NSA_HPG * HEAD_DIM] = _dot_nt(eye, heads).astype(BF16)


def _nsa_call(q, ngt, kc, vct, ovt, bct, ks, vst, vst2, kw, vwt, bt0, bt1):
    B, _, _, S = q.shape
    G = NSA_KV_GROUPS
    nq = S // Q_BLOCK
    per_b = lambda a: pl.BlockSpec((1,) + a.shape[1:], lambda b, i: (b,) + (0,) * (a.ndim - 1),
                                   pipeline_mode=pl.Buffered(1))
    in_specs = [
        pl.BlockSpec((1, NSA_HEADS, LANE, Q_BLOCK), lambda b, i: (b, 0, 0, i)),
        pl.BlockSpec((1, LANE, Q_BLOCK), lambda b, i: (b, 0, i)),
        per_b(kc), per_b(vct), _full(ovt.shape), _full(bct.shape),
        per_b(ks), per_b(vst), per_b(vst2), per_b(kw), per_b(vwt), _full(bt0.shape), _full(bt1.shape),
    ]
    scratch = [
        pltpu.VMEM((G, CMP_ROWS, QCOLS), F32),
        pltpu.VMEM((G, 2, 2 * LANE, QCOLS), BF16),
        pltpu.VMEM((G, 2 * LANE, QCOLS), BF16),
        pltpu.VMEM((G, 1, QCOLS), F32),
        pltpu.VMEM((G, V_ROWS, QCOLS), F32),
    ]
    scratch += [pltpu.VMEM((FAR_KEYS, QCOLS), F32)] * (G * FAR_BUFS)
    scratch += [pltpu.VMEM((1, QCOLS), F32)] * (G * FAR_BUFS)
    return pl.pallas_call(
        _nsa_kernel, grid=(B, nq), in_specs=in_specs,
        out_specs=pl.BlockSpec((1, Q_BLOCK, NSA_WIDTH), lambda b, i: (b, i, 0)),
        out_shape=jax.ShapeDtypeStruct((B, S, NSA_WIDTH), BF16),
        scratch_shapes=scratch, compiler_params=_params(2), name="sparse_attention",
    )(q, ngt, kc, vct, ovt, bct, ks, vst, vst2, kw, vwt, bt0, bt1)


def _merge_kernel(h_ref, a_ref, o_ref, wmg_ref, bmg_ref, wa_ref, wb_ref, wout_ref, lng_ref, lnb_ref, out_ref):
    tm = h_ref.shape[1]
    halves = [slice(k * tm // 2, (k + 1) * tm // 2) for k in range(2)]
    pre = []
    for rows in halves:
        mg = _dot(h_ref[0, rows].astype(BF16), wmg_ref[...]) + bmg_ref[...]
        pre.append((mg, _dot(a_ref[0, rows], wa_ref[...]), _dot(o_ref[0, rows], wb_ref[...])))
    ms = []
    for mg, pa, pb in pre:
        y = _sigmoid(mg[:, :D_MODEL]) * pa + _sigmoid(mg[:, D_MODEL:]) * pb
        ms.append(_dot(y.astype(BF16), wout_ref[...]))
    for rows, m in zip(halves, ms):
        out_ref[0, rows] = _layer_norm(ALPHA * h_ref[0, rows] + m, lng_ref[...], lnb_ref[...])


def _merge_call(h, a, o, p):
    B, S, _ = h.shape
    tm = 2 * ROW_TILE
    row = lambda w: pl.BlockSpec((1, tm, w), lambda b, i: (b, i, 0))
    ws = (p["w_mg"], p["b_mg"], p["w_proj_a"], p["w_proj_b"], p["w_out"], p["ln1_g"], p["ln1_b"])
    return pl.pallas_call(
        _merge_kernel, grid=(B, S // tm),
        in_specs=[row(D_MODEL), row(GM_WIDTH), row(NSA_WIDTH)] + [_full(a.shape) for a in ws],
        out_specs=row(D_MODEL), out_shape=jax.ShapeDtypeStruct((B, S, D_MODEL), F32),
        compiler_params=_params(2), name="merge_norm")(h, a, o, *ws)


def _ffn_kernel(h_ref, wup_ref, cw_ref, wdown_ref, lng_ref, lnb_ref, out_ref, carry_ref):
    tm = h_ref.shape[1]

    @pl.when(pl.program_id(1) == 0)
    def _():
        carry_ref[...] = jnp.zeros(carry_ref.shape, F32)

    h = h_ref[0]
    hb = h.astype(BF16)
    row8 = lax.broadcasted_iota(jnp.int32, (8, FF_CHUNK), 0)

    def up_proj(c):
        cols = lambda idx: slice(idx * FF_CHUNK, (idx + 1) * FF_CHUNK)
        return _dot(hb, wup_ref[:, cols(c)]), _dot(hb, wup_ref[:, cols(N_FF + c)])

    def conv(up, idx):
        prev = carry_ref[idx]
        carry_ref[idx] = up[tm - 8:tm]
        cw = cw_ref[:, idx * FF_CHUNK:(idx + 1) * FF_CHUNK]
        out = cw[3:4] + cw[2:3] * up
        for shift in (1, 2):
            rolled = pltpu.roll(up, shift, 0)
            head = jnp.where(row8 < shift, pltpu.roll(prev, shift, 0), rolled[0:8])
            out = out + cw[2 - shift:3 - shift] * jnp.concatenate([head, rolled[8:]], axis=0)
        return out

    f = jnp.zeros((tm, D_MODEL), F32)
    ups = [up_proj(c) for c in range(FF_AHEAD)]
    for c in range(N_FF):
        g_up, v_up = ups.pop(0)
        if c + FF_AHEAD < N_FF:
            ups.append(up_proj(c + FF_AHEAD))
        gate = conv(g_up, c)
        act = (gate * _sigmoid(gate) * conv(v_up, N_FF + c)).astype(BF16)
        f = f + _dot(act, wdown_ref[c])
    out_ref[0] = _layer_norm(ALPHA * h + f, lng_ref[...], lnb_ref[...])


def _ffn_call(h, p):
    B, S, _ = h.shape
    tm = ROW_TILE
    row = pl.BlockSpec((1, tm, D_MODEL), lambda b, i: (b, i, 0))
    ws = (p["ffn_w_up"], p["ffn_cw"], p["ffn_w_down"], p["ln2_g"], p["ln2_b"])
    return pl.pallas_call(
        _ffn_kernel, grid=(B, S // tm),
        in_specs=[row] + [_full(a.shape) for a in ws],
        out_specs=row, out_shape=jax.ShapeDtypeStruct((B, S, D_MODEL), F32),
        scratch_shapes=[pltpu.VMEM((2 * N_FF, 8, FF_CHUNK), F32)],
        compiler_params=_params(2), name="conv_ffn_norm")(h, *ws)


def _t5_bucket(n):
    max_exact = NUM_BUCKETS // 2
    log_ratio = jnp.log(jnp.maximum(n, 1).astype(F32) / max_exact) / math.log(MAX_DISTANCE / max_exact)
    large = jnp.minimum(max_exact + (log_ratio * (NUM_BUCKETS - max_exact)).astype(jnp.int32), NUM_BUCKETS - 1)
    return jnp.where(n < max_exact, n, large)


def _bias_tiles(rel_bias):
    G, H = NSA_KV_GROUPS, NSA_HPG
    tab = rel_bias[_t5_bucket(jnp.arange(2 * KEY_TILE, dtype=jnp.int32))] - rel_bias[NUM_BUCKETS - 1]
    tab = tab * LOG2E

    def tile(offset, n, row_step=1):
        d = np.arange(2 * n)
        dist = offset + np.where(d < n, d, d - 2 * n)
        w = jnp.where((dist >= 0)[:, None], tab[np.clip(dist, 0, 2 * KEY_TILE - 1)], NEG_INF)
        m = jnp.broadcast_to(w[None], (n, 2 * n, w.shape[1])).reshape(2 * n * n, -1)[:n * (2 * n - 1)]
        t = m.reshape(n, 2 * n - 1, -1)[::row_step, :Q_BLOCK]
        rows = t.shape[0]
        return t.reshape(rows, Q_BLOCK, G, H).transpose(2, 0, 3, 1).reshape(G, rows, H * Q_BLOCK)

    bt0 = tile(0, KEY_TILE)
    bt1 = tile(KEY_TILE, KEY_TILE)
    bct = tile(CMP_STRIDE * CMP_PAD - CMP_BLOCK + 1, CMP_STRIDE * CMP_BAND, CMP_STRIDE)
    return bt0, bt1, bct


def _overlap_t(seq):
    n_cmp = seq // CMP_STRIDE - 1
    n_blk = seq // SEL_BLOCK
    cs = np.arange(n_cmp)[:, None] * CMP_STRIDE
    ss = np.arange(n_blk)[None, :] * SEL_BLOCK
    ov = np.maximum(np.minimum(cs + CMP_BLOCK, ss + SEL_BLOCK) - np.maximum(cs, ss), 0) / CMP_BLOCK
    out = np.zeros((LANE, CMP_ROWS), np.float32)
    out[:n_blk, CMP_PAD:CMP_PAD + n_cmp] = ov.T
    return jnp.asarray(out, BF16)


def _pad_cols(w, width):
    return jnp.pad(w, ((0, 0), (0, width - w.shape[1])))


def _layer_params(l, w_in, b_in, gm_ln_g, gm_ln_b, gm_ws, gm_bs, cmp_pos, cmp_w1, cmp_b1, cmp_w2,
                  w_proj_a, w_proj_b, w_out, ln1_g, ln1_b, ffn_w_up, ffn_conv_w, ffn_conv_b, ffn_w_down,
                  ln2_g, ln2_b):
    G = NSA_KV_GROUPS
    w, b = w_in[l], b_in[l]
    o_q = 2 * GM_WIDTH
    o_kc = o_q + NSA_WIDTH
    o_vc, o_ks, o_vs, o_kw, o_vw = (o_kc + KV_WIDTH * k for k in range(1, 6))
    o_ng = o_vw + KV_WIDTH
    n_ng = NSA_HEADS * N_GATES
    o_mg = o_ng + n_ng
    seg = lambda a, o, n: a[..., o:o + n]

    wq = jnp.pad(seg(w, o_q, NSA_WIDTH).T.reshape(NSA_HEADS, HEAD_DIM, D_MODEL),
                 ((0, 0), (0, LANE - HEAD_DIM), (0, 0))).reshape(NSA_HEADS * LANE, D_MODEL)
    bq = jnp.pad(seg(b, o_q, NSA_WIDTH).reshape(NSA_HEADS, HEAD_DIM), ((0, 0), (0, LANE - HEAD_DIM)))
    bq = bq.at[:, HEAD_DIM].set(NEG_INF / Q_SCALE).reshape(NSA_HEADS * LANE)

    def k_groups(o):
        return [_pad_cols(seg(w, o + g * HEAD_DIM, HEAD_DIM), LANE) for g in range(G)], \
               [jnp.pad(seg(b, o + g * HEAD_DIM, HEAD_DIM), (0, LANE - HEAD_DIM)) for g in range(G)]

    wks, bks = k_groups(o_ks)
    wkw, bkw = k_groups(o_kw)
    w_r = jnp.concatenate([seg(w, o_kc, KV_WIDTH), seg(w, o_vc, KV_WIDTH), *wks, *wkw,
                           _pad_cols(seg(w, o_ng, n_ng), LANE)], axis=1)
    b_r = jnp.concatenate([seg(b, o_kc, KV_WIDTH), seg(b, o_vc, KV_WIDTH), *bks, *bkw,
                           jnp.pad(seg(b, o_ng, n_ng), (0, LANE - n_ng))])

    def v_rows(o):
        wt = jnp.pad(seg(w, o, HEAD_DIM).T, ((0, V_ROWS - HEAD_DIM), (0, 0)))
        bt = jnp.pad(seg(b, o, HEAD_DIM), (0, V_ROWS - HEAD_DIM)).at[HEAD_DIM].set(1.0)
        return wt, bt

    vparts = [v_rows(o + g * HEAD_DIM) for o in (o_vs, o_vw) for g in range(G)]
    w_t = jnp.concatenate([wq] + [wt for wt, _ in vparts], axis=0)
    b_t = jnp.concatenate([bq] + [bt for _, bt in vparts])
    row_scale = jnp.where(jnp.arange(b_t.shape[0]) < NSA_HEADS * LANE, Q_SCALE, 1.0)
    b_t = jnp.stack([b_t, row_scale], axis=1)

    causal = jnp.tril(jnp.ones((GM_CHUNK, GM_CHUNK), F32))
    gw = GM_WIDTH // GM_GROUPS
    bs = gm_bs[l]
    bsb = jnp.concatenate([jnp.broadcast_to(bs[0::2, :, None], (GM_GROUPS // 2, GM_CHUNK, gw)),
                           jnp.broadcast_to(bs[1::2, :, None], (GM_GROUPS // 2, GM_CHUNK, gw))], axis=2)

    w1 = cmp_w1[l].reshape(2, CMP_BLOCK, HEAD_DIM, CMP_HIDDEN)
    w1 = jnp.stack([jnp.pad(w1, ((0, 0), (0, 0), (g * HEAD_DIM, (G - 1 - g) * HEAD_DIM), (0, 0)))
                    for g in range(G)], axis=1)
    w2 = cmp_w2[l]
    cw = jnp.concatenate([ffn_conv_w[l], ffn_conv_b[l][None], jnp.zeros((8 - CONV_WIDTH - 1, 2 * D_FF), F32)])
    return {
        "w_uv": seg(w, 0, 2 * GM_WIDTH).astype(BF16), "b_uv": seg(b, 0, 2 * GM_WIDTH)[None],
        "gm_ln_g": gm_ln_g[l][None], "gm_ln_b": gm_ln_b[l][None],
        "gm_wc": (gm_ws[l] * causal).astype(BF16), "gm_bsb": bsb,
        "w_r": w_r.astype(BF16), "b_r": b_r[None], "w_t": w_t.astype(BF16), "b_t": b_t,
        "cmp_pos": jnp.tile(cmp_pos[l], (1, 1, G)), "cmp_w1": w1.astype(BF16), "cmp_b1": cmp_b1[l][:, None],
        "cmp_w2k": _pad_cols(w2[0], LANE).astype(BF16),
        "cmp_w2vt": jnp.pad(w2[1].T, ((0, V_ROWS - HEAD_DIM), (0, 0))).astype(BF16),
        "w_mg": seg(w, o_mg, 2 * D_MODEL).astype(BF16), "b_mg": seg(b, o_mg, 2 * D_MODEL)[None],
        "w_proj_a": w_proj_a[l].astype(BF16), "w_proj_b": w_proj_b[l].astype(BF16),
        "w_out": w_out[l].astype(BF16), "ln1_g": ln1_g[l][None], "ln1_b": ln1_b[l][None],
        "ffn_w_up": ffn_w_up[l].astype(BF16), "ffn_cw": cw,
        "ffn_w_down": ffn_w_down[l].reshape(N_FF, FF_CHUNK, D_MODEL).astype(BF16),
        "ln2_g": ln2_g[l][None], "ln2_b": ln2_b[l][None],
    }


def kernel(x, w_in, b_in, gm_ln_g, gm_ln_b, gm_ws, gm_bs, cmp_pos, cmp_w1, cmp_b1, cmp_w2, rel_bias,
           w_proj_a, w_proj_b, w_out, ln1_g, ln1_b, ffn_w_up, ffn_conv_w, ffn_conv_b, ffn_w_down,
           ln2_g, ln2_b):
    B, S, _ = x.shape
    assert S % FAR_KEYS == 0 and N_SEL <= S // SEL_BLOCK <= LANE
    assert S // CMP_STRIDE - 1 + CMP_PAD < CMP_ROWS
    bt0, bt1, bct = _bias_tiles(rel_bias)
    ovt = _overlap_t(S)
    h = x
    for l in range(DEPTH):
        p = _layer_params(l, w_in, b_in, gm_ln_g, gm_ln_b, gm_ws, gm_bs, cmp_pos, cmp_w1, cmp_b1, cmp_w2,
                          w_proj_a, w_proj_b, w_out, ln1_g, ln1_b, ffn_w_up, ffn_conv_w, ffn_conv_b,
                          ffn_w_down, ln2_g, ln2_b)
        a, q, kvc, ks, kw, vst, vst2, vwt, ngt = _proj_call(h, p)
        kc, vct = _compress_call(kvc, p)
        o = _nsa_call(q, ngt, kc, vct, ovt, bct, ks, vst, vst2, kw, vwt, bt0, bt1)
        h = _merge_call(h, a, o, p)
        h = _ffn_call(h, p)
    return h
```

```python
import itertools
import math

import jax
import jax.numpy as jnp
import numpy as np
from jax import lax
from jax.experimental import pallas as pl
from jax.experimental.pallas import tpu as pltpu

D_MODEL = 1024
DEPTH = 2
GM_WIDTH = D_MODEL // 2
GM_GROUPS = 8
GM_CHUNK = 128
NSA_HEADS = 8
NSA_KV_GROUPS = 2
NSA_HPG = NSA_HEADS // NSA_KV_GROUPS
HEAD_DIM = 64
NSA_WIDTH = NSA_HEADS * HEAD_DIM
KV_WIDTH = NSA_KV_GROUPS * HEAD_DIM
N_GATES = 3
CMP_BLOCK = 32
CMP_STRIDE = 16
CMP_HIDDEN = 256
SEL_BLOCK = 64
N_SEL = 16
WINDOW = 512
Q_BLOCK = 128
NUM_BUCKETS = 32
MAX_DISTANCE = 128
D_FF = 2816
CONV_WIDTH = 3
ALPHA = (2.0 * DEPTH) ** 0.25
LN_EPS = 1e-5
FORCED_SCORE = 1e6
NEG_INF = -1e30
M_INIT = -1e29
LOG2E = math.log2(math.e)
Q_SCALE = HEAD_DIM ** -0.5 * LOG2E

LANE = 128
QCOLS = NSA_HPG * Q_BLOCK
KEY_TILE = 128
FAR_TILES = 4
FAR_KEYS = FAR_TILES * KEY_TILE
FAR_AHEAD = 2
FAR_BUFS = 2
BAND_TILES = WINDOW // KEY_TILE + 1
PAD_TILES = BAND_TILES - 1
CMP_PAD = 8
CMP_ROWS = 640
CMP_BAND = 16
V_ROWS = 80
FF_CHUNK = 256
N_FF = D_FF // FF_CHUNK
FF_AHEAD = 2
ROW_TILE = 256
VMEM_LIMIT = 56 * 1024 * 1024

F32 = jnp.float32
BF16 = jnp.bfloat16
NT_DIMS = (((1,), (1,)), ((), ()))


def _dot(a, b):
    return jnp.dot(a, b, preferred_element_type=F32)


def _dot_nt(a, b):
    return lax.dot_general(a, b, NT_DIMS, preferred_element_type=F32)


def _layer_norm(x, g, b):
    mu = jnp.mean(x, axis=-1, keepdims=True)
    xc = x - mu
    var = jnp.mean(xc * xc, axis=-1, keepdims=True)
    return xc * lax.rsqrt(var + LN_EPS) * g + b


def _gelu_tanh(x):
    c = math.sqrt(2.0 / math.pi)
    return x * (0.5 * (1.0 + jnp.tanh(c * (x + 0.044715 * (x * x * x)))))


def _sigmoid(x):
    return 1.0 / (1.0 + jnp.exp(-x))


def _interleave(*stages):
    stages = list(stages)
    while stages:
        for stage in list(stages):
            try:
                next(stage)
            except StopIteration:
                stages.remove(stage)


def _params(n_axes):
    return pltpu.CompilerParams(dimension_semantics=("arbitrary",) * n_axes,
                                vmem_limit_bytes=VMEM_LIMIT)


def _full(shape):
    n = len(shape)
    return pl.BlockSpec(shape, lambda *_: (0,) * n)


def _proj_kernel(h_ref, wuv_ref, buv_ref, lng_ref, lnb_ref, wc_ref, bsb_ref, wr_ref, br_ref, wt_ref, bt_ref,
                 a_ref, q_ref, kvc_ref, ks_ref, kw_ref, vst_ref, vst2_ref, vwt_ref, ngt_ref):
    tm = h_ref.shape[1]
    G = NSA_KV_GROUPS
    hb = h_ref[0].astype(BF16)

    uv = _dot(hb, wuv_ref[...]) + buv_ref[...]
    r = _dot(hb, wr_ref[...]) + br_ref[...]
    tr = ((_dot_nt(wt_ref[...], hb) + bt_ref[:, 0:1]) * bt_ref[:, 1:2]).astype(BF16)

    u = _gelu_tanh(uv[:, :GM_WIDTH])
    v = _layer_norm(_gelu_tanh(uv[:, GM_WIDTH:]), lng_ref[...], lnb_ref[...]).astype(BF16)
    lane = lax.broadcasted_iota(jnp.int32, (GM_CHUNK, LANE), 1)
    group_w = GM_WIDTH // GM_GROUPS
    for ch in range(tm // GM_CHUNK):
        rows = slice(ch * GM_CHUNK, (ch + 1) * GM_CHUNK)
        for pr in range(GM_GROUPS // 2):
            cols = slice(pr * LANE, (pr + 1) * LANE)
            x = v[rows, cols]
            sv = jnp.where(lane < group_w, _dot(wc_ref[2 * pr], x), _dot(wc_ref[2 * pr + 1], x))
            a_ref[0, rows, cols] = (u[rows, cols] * (sv + bsb_ref[pr])).astype(BF16)

    for hh in range(NSA_HEADS):
        q_ref[0, hh] = tr[hh * LANE:(hh + 1) * LANE]
    vt = tr[NSA_HEADS * LANE:]

    kvc_ref[0, 0] = r[:, 0:LANE]
    kvc_ref[0, 1] = r[:, LANE:2 * LANE]
    token = pl.program_id(1) * tm + lax.broadcasted_iota(jnp.int32, (tm, LANE), 0)
    sel_block = lax.shift_right_logical(token, int(math.log2(SEL_BLOCK)))
    onehot = (lax.broadcasted_iota(jnp.int32, (tm, LANE), 1) == sel_block).astype(BF16)
    for g in range(G):
        ks_ref[0, g, :, 0:LANE] = onehot
        ks_ref[0, g, :, LANE:2 * LANE] = r[:, (2 + g) * LANE:(3 + g) * LANE].astype(BF16)
        kw_ref[0, g] = r[:, (4 + g) * LANE:(5 + g) * LANE].astype(BF16)
    ngt_ref[0] = r[:, 6 * LANE:7 * LANE].T

    for g in range(G):
        sel_rows = slice(g * V_ROWS, (g + 1) * V_ROWS)
        win_rows = slice((G + g) * V_ROWS, (G + g + 1) * V_ROWS)
        vst2_ref[0, g, 0] = vt[sel_rows]
        for t in range(tm // KEY_TILE):
            cols = slice(t * KEY_TILE, (t + 1) * KEY_TILE)
            vst_ref[0, g, t] = vt[sel_rows, cols]
            vwt_ref[0, g, t] = vt[win_rows, cols]


def _proj_call(h, p):
    B, S, _ = h.shape
    tm = 2 * KEY_TILE
    G = NSA_KV_GROUPS
    nt = S // KEY_TILE
    row = lambda w: pl.BlockSpec((1, tm, w), lambda b, i: (b, i, 0))
    lead = lambda n, w: pl.BlockSpec((1, n, tm, w), lambda b, i: (b, 0, i, 0))
    vtile = lambda n, w: pl.BlockSpec((1, G, n, V_ROWS, w), lambda b, i: (b, 0, i, 0, 0))
    out_shape = (
        jax.ShapeDtypeStruct((B, S, GM_WIDTH), BF16),
        jax.ShapeDtypeStruct((B, NSA_HEADS, LANE, S), BF16),
        jax.ShapeDtypeStruct((B, 2, S, LANE), F32),
        jax.ShapeDtypeStruct((B, G, S, 2 * LANE), BF16),
        jax.ShapeDtypeStruct((B, G, S, LANE), BF16),
        jax.ShapeDtypeStruct((B, G, nt, V_ROWS, KEY_TILE), BF16),
        jax.ShapeDtypeStruct((B, G, nt // 2, V_ROWS, 2 * KEY_TILE), BF16),
        jax.ShapeDtypeStruct((B, G, nt, V_ROWS, KEY_TILE), BF16),
        jax.ShapeDtypeStruct((B, LANE, S), F32),
    )
    out_specs = (row(GM_WIDTH), pl.BlockSpec((1, NSA_HEADS, LANE, tm), lambda b, i: (b, 0, 0, i)),
                 lead(2, LANE), lead(G, 2 * LANE), lead(G, LANE),
                 vtile(tm // KEY_TILE, KEY_TILE), vtile(1, 2 * KEY_TILE), vtile(tm // KEY_TILE, KEY_TILE),
                 pl.BlockSpec((1, LANE, tm), lambda b, i: (b, 0, i)))
    in_arrays = (h, p["w_uv"], p["b_uv"], p["gm_ln_g"], p["gm_ln_b"], p["gm_wc"], p["gm_bsb"],
                 p["w_r"], p["b_r"], p["w_t"], p["b_t"])
    in_specs = [row(D_MODEL)] + [_full(a.shape) for a in in_arrays[1:]]
    return pl.pallas_call(
        _proj_kernel, grid=(B, S // tm), in_specs=in_specs, out_specs=out_specs,
        out_shape=out_shape, compiler_params=_params(2), name="proj_gmlp")(*in_arrays)


def _compress_kernel(x_ref, pos_ref, w1_ref, b1_ref, w2k_ref, w2vt_ref, kc_ref, vct_ref):
    kv = pl.program_id(1)
    n = x_ref.shape[2] // CMP_STRIDE
    half = CMP_BLOCK // 2
    pad_row = (lax.broadcasted_iota(jnp.int32, (1, LANE), 1) == HEAD_DIM).astype(F32)
    for g in range(NSA_KV_GROUPS):
        first = jnp.zeros((n, CMP_HIDDEN), F32)
        second = jnp.zeros((n, CMP_HIDDEN), F32)
        for r in range(half):
            xr = x_ref[0, 0, pl.ds(r, n, stride=CMP_STRIDE), :]
            first = first + _dot((xr + pos_ref[0, r:r + 1, :]).astype(BF16), w1_ref[0, g, r])
            second = second + _dot((xr + pos_ref[0, half + r:half + r + 1, :]).astype(BF16),
                                   w1_ref[0, g, half + r])
        hid = first + pltpu.roll(second, n - 1, 0) + b1_ref[0]
        act = (hid * _sigmoid(hid)).astype(BF16)

        @pl.when(kv == 0)
        def _():
            res = _dot(act, w2k_ref[...])
            row = lax.broadcasted_iota(jnp.int32, res.shape, 0)
            res = jnp.where(row == n - 1, pad_row, res)
            kc_ref[0, g] = jnp.concatenate(
                [jnp.broadcast_to(pad_row, (CMP_PAD, LANE)), res,
                 jnp.broadcast_to(pad_row, (CMP_ROWS - CMP_PAD - n, LANE))], axis=0).astype(BF16)

        @pl.when(kv == 1)
        def _():
            vt = _dot_nt(w2vt_ref[...], act)
            col = lax.broadcasted_iota(jnp.int32, vt.shape, 1)
            vt = jnp.where(col == n - 1, 0.0, vt)
            vt = jnp.concatenate([vt, jnp.zeros((V_ROWS, CMP_ROWS - n), F32)], axis=1)
            vct_ref[0, g] = pltpu.roll(vt, CMP_PAD, 1).astype(BF16)


def _compress_call(kvc, p):
    B, _, S, _ = kvc.shape
    G = NSA_KV_GROUPS
    per_kv = lambda a: pl.BlockSpec((1,) + a.shape[1:], lambda b, k: (k,) + (0,) * (a.ndim - 1))
    return pl.pallas_call(
        _compress_kernel, grid=(B, 2),
        in_specs=[pl.BlockSpec((1, 1, S, LANE), lambda b, k: (b, k, 0, 0)),
                  per_kv(p["cmp_pos"]), per_kv(p["cmp_w1"]), per_kv(p["cmp_b1"]),
                  _full(p["cmp_w2k"].shape), _full(p["cmp_w2vt"].shape)],
        out_specs=(pl.BlockSpec((1, G, CMP_ROWS, LANE), lambda b, k: (b, 0, 0, 0)),
                   pl.BlockSpec((1, G, V_ROWS, CMP_ROWS), lambda b, k: (b, 0, 0, 0))),
        out_shape=(jax.ShapeDtypeStruct((B, G, CMP_ROWS, LANE), BF16),
                   jax.ShapeDtypeStruct((B, G, V_ROWS, CMP_ROWS), BF16)),
        compiler_params=_params(2), name="kv_compress",
    )(kvc, p["cmp_pos"], p["cmp_w1"], p["cmp_b1"], p["cmp_w2k"], p["cmp_w2vt"])


def _nsa_kernel(q_ref, ngt_ref, kc_ref, vct_ref, ovt_ref, bct_ref, ks_ref, vst_ref, vst2_ref, kw_ref, vwt_ref,
                 bt0_ref, bt1_ref, o_ref, sc_ref, rhsf_ref, rhst_ref, m_ref, acc_ref, *ring_refs):
    G = NSA_KV_GROUPS
    groups = range(G)
    sbuf_refs = [ring_refs[g * FAR_BUFS:(g + 1) * FAR_BUFS] for g in groups]
    mbuf_refs = [ring_refs[(G + g) * FAR_BUFS:(G + g + 1) * FAR_BUFS] for g in groups]
    i = pl.program_id(1)
    n_blk = LANE
    blk = lax.broadcasted_iota(jnp.int32, (n_blk, Q_BLOCK), 0)
    qi = lax.broadcasted_iota(jnp.int32, (n_blk, Q_BLOCK), 1)
    eye = (blk == qi).astype(BF16)
    qt = [jnp.concatenate([q_ref[0, g * NSA_HPG + h] for h in range(NSA_HPG)], axis=1) for g in groups]
    bt0 = [bt0_ref[g] for g in groups]
    bt1 = [bt1_ref[g] for g in groups]


    def band_logits(k_ref, g, rhs, extra, flag_col, tiles):
        width = k_ref.shape[3]
        before_start = (lax.broadcasted_iota(jnp.int32, (KEY_TILE, width), 1) == flag_col).astype(BF16)
        for t in range(BAND_TILES):
            tau = i - PAD_TILES + t
            row0 = pl.multiple_of(jnp.maximum(tau, 0) * KEY_TILE, KEY_TILE)
            k = k_ref[0, g, pl.ds(row0, KEY_TILE), :]
            if t < PAD_TILES:
                k = jnp.where(jnp.broadcast_to(tau, k.shape) >= 0, k, before_start)
            s = _dot(k, rhs)
            tiles.append(s if extra[t] is None else s + extra[t])
            yield

    def band_softmax(tiles, vt_ref, g, state, out):
        m_old, acc_old = state()
        m_new = m_old
        for s in tiles:
            m_new = jnp.maximum(m_new, jnp.max(s, axis=0, keepdims=True))
        acc = jnp.exp2(m_old - m_new) * acc_old
        for t, s in enumerate(tiles):
            vt = vt_ref[0, g, jnp.maximum(i - PAD_TILES + t, 0)]
            acc = acc + _dot(vt, jnp.exp2(s - m_new).astype(BF16))
            yield
        out.append(acc[0:HEAD_DIM] / acc[HEAD_DIM:HEAD_DIM + 1])

    band = pl.multiple_of(8 * i, 8)
    visible = 8 * i + CMP_BAND

    def compressed_stage(rows):
        def softmax(g):
            s = sc_ref[g, 0:rows]
            key = lax.broadcasted_iota(jnp.int32, s.shape, 0)
            s = jnp.where(key >= visible, NEG_INF, s)
            m = jnp.maximum(jnp.max(s, axis=0, keepdims=True), M_INIT)
            p = jnp.exp2(s - m)
            l = jnp.sum(p, axis=0, keepdims=True)
            r = jnp.where(l > 0.0, 1.0 / l, 0.0)
            oc = _dot(vct_ref[0, g, :, 0:rows], p.astype(BF16))[0:HEAD_DIM] * r
            pn = p * r
            psum = pn[:, 0:Q_BLOCK]
            for h in range(1, NSA_HPG):
                psum = psum + pn[:, h * Q_BLOCK:(h + 1) * Q_BLOCK]
            p_hi = psum.astype(BF16)
            p_lo = (psum - p_hi.astype(F32)).astype(BF16)
            ovt = ovt_ref[:, 0:rows]
            return oc, _dot(ovt, p_hi) + _dot(ovt, p_lo)

        def run():
            for g in groups:
                sc_ref[g, 0:rows] = _dot(kc_ref[0, g, 0:rows], qt[g])
                sc_ref[g, pl.ds(band, CMP_BAND), :] += bct_ref[g]
            return tuple(x for g in groups for x in softmax(g))
        return run

    row_options = tuple(range(LANE, CMP_ROWS + 1, LANE))
    flat = lax.switch((visible + LANE - 1) // LANE - 1, [compressed_stage(rows) for rows in row_options])
    oc, imp = flat[0::2], flat[1::2]

    cur = (Q_BLOCK // SEL_BLOCK) * i + (qi >= SEL_BLOCK).astype(jnp.int32)
    forced = (blk == 0) | (blk == cur) | (blk == cur - 1)
    taken = -3e38
    blk_f = blk.astype(F32)
    start = tuple((jnp.where(forced, taken, jnp.where(blk > cur, NEG_INF, imp[g])), forced.astype(F32))
                  for g in groups)

    def select_stage(rows):
        def pick(_, carry):
            out = []
            for val, sel in carry:
                mx = jnp.max(val, axis=0, keepdims=True)
                first = jnp.min(jnp.where(val == mx, blk_f[0:rows], 1e9), axis=0, keepdims=True)
                hit = blk_f[0:rows] == first
                out.append((jnp.where(hit, taken, val), jnp.where(hit, 1.0, sel)))
            return tuple(out)

        def run():
            head = lax.fori_loop(0, N_SEL - 3, pick, tuple((v[0:rows], s[0:rows]) for v, s in start))
            return tuple(jnp.concatenate([sel, jnp.zeros((n_blk - rows, Q_BLOCK), F32)], axis=0)
                         if rows < n_blk else sel for _, sel in head)
        return run

    row_step = n_blk // 4
    row_options = tuple(range(row_step, n_blk + 1, row_step))
    n_cand = (Q_BLOCK // SEL_BLOCK) * (i + 1)
    picked = lax.switch((n_cand + row_step - 1) // row_step - 1, [select_stage(rows) for rows in row_options])

    n_far = jnp.maximum(i - 1, 0) // FAR_TILES
    last_far = jnp.maximum(n_far - 1, 0)
    per_head = lambda x: jnp.concatenate([x.astype(BF16)] * NSA_HPG, axis=1)
    for g in groups:
        neg_far = jnp.where((picked[g] > 0.5) & (blk <= cur), 0.0, NEG_INF)
        neg_tail = jnp.where(blk >= n_far * (FAR_KEYS // SEL_BLOCK), neg_far, NEG_INF)
        rhsf_ref[g, 0, 0:LANE] = per_head(neg_far)
        rhsf_ref[g, 1, 0:LANE] = jnp.full((LANE, QCOLS), NEG_INF, BF16)
        rhst_ref[g, 0:LANE] = per_head(neg_tail)
        rhsf_ref[g, 0, LANE:2 * LANE] = qt[g]
        rhsf_ref[g, 1, LANE:2 * LANE] = qt[g]
        rhst_ref[g, LANE:2 * LANE] = qt[g]
        m_ref[g] = jnp.full(m_ref.shape[1:], M_INIT, F32)
        acc_ref[g] = jnp.zeros(acc_ref.shape[1:], F32)

    def far_logits(g, c):
        r0 = pl.multiple_of(FAR_KEYS * jnp.minimum(c, last_far), FAR_KEYS)
        rhs = rhsf_ref[g, (c >= n_far).astype(jnp.int32)]
        s = _dot(ks_ref[0, g, pl.ds(r0, FAR_KEYS), :], rhs)
        return s, jnp.max(s, axis=0, keepdims=True)

    def far_softmax_pieces(g, c, s_ref, mx_ref):
        tile0 = (FAR_TILES // 2) * jnp.minimum(c, last_far)
        m_old = m_ref[g]
        m_new = jnp.maximum(m_old, mx_ref[...])
        acc = jnp.exp2(m_old - m_new) * acc_ref[g]
        for t in range(FAR_TILES // 2):
            p = jnp.exp2(s_ref[2 * t * KEY_TILE:2 * (t + 1) * KEY_TILE] - m_new).astype(BF16)
            acc = acc + _dot(vst2_ref[0, g, tile0 + t], p)
            yield
        acc_ref[g] = acc
        m_ref[g] = m_new

    def far_softmax(g, c, s_ref, mx_ref):
        for _ in far_softmax_pieces(g, c, s_ref, mx_ref):
            pass

    bufs = [tuple(zip(sbuf_refs[g], mbuf_refs[g])) for g in groups]
    for u in range(FAR_AHEAD):
        for g in groups:
            bufs[g][u][0][...], bufs[g][u][1][...] = far_logits(g, u)

    n_trips = jnp.maximum(n_far - FAR_AHEAD + FAR_BUFS - 1, 0) // FAR_BUFS

    def far_ring(j, carry):
        for u in range(FAR_BUFS):
            c = FAR_BUFS * j + u
            ahead = [far_logits(g, c + FAR_AHEAD) for g in groups]
            for g in groups:
                far_softmax(g, c, *bufs[g][u])
            for g in groups:
                ahead_s, ahead_mx = bufs[g][(u + FAR_AHEAD) % FAR_BUFS]
                ahead_s[...], ahead_mx[...] = ahead[g]
        return carry

    lax.fori_loop(0, n_trips, far_ring, 0)

    kk = lax.broadcasted_iota(jnp.int32, (KEY_TILE, QCOLS), 0)
    qq = lax.broadcasted_iota(jnp.int32, (KEY_TILE, QCOLS), 1) & (Q_BLOCK - 1)
    edge = jnp.where(kk > qq, 0.0, NEG_INF)
    win_tiles, tail_tiles, ow, os_ = ([[] for _ in groups] for _ in range(4))
    fresh = lambda: (jnp.full(m_ref.shape[1:], M_INIT, F32), jnp.zeros(acc_ref.shape[1:], F32))
    _interleave(
        itertools.chain(*[band_logits(kw_ref, g, qt[g], [edge] + [None] * (BAND_TILES - 3) + [bt1[g], bt0[g]],
                                      HEAD_DIM, win_tiles[g]) for g in groups]),
        itertools.chain(*[far_softmax_pieces(g, FAR_BUFS * n_trips + u, *bufs[g][u])
                          for u in range(FAR_AHEAD) for g in groups]))
    _interleave(
        itertools.chain(*[band_logits(ks_ref, g, rhst_ref[g], [None] * (BAND_TILES - 2) + [bt1[g], bt0[g]],
                                      LANE + HEAD_DIM, tail_tiles[g]) for g in groups]),
        itertools.chain(*[band_softmax(win_tiles[g], vwt_ref, g, fresh, ow[g]) for g in groups]))
    _interleave(*[band_softmax(tail_tiles[g], vst_ref, g, lambda g=g: (m_ref[g], acc_ref[g]), os_[g])
                  for g in groups])
    ow = [x[0] for x in ow]
    os_ = [x[0] for x in os_]

    for g in groups:
        def gate(branch):
            rows = [ngt_ref[0, NSA_HPG * N_GATES * g + N_GATES * h + branch:
                            NSA_HPG * N_GATES * g + N_GATES * h + branch + 1, :] for h in range(NSA_HPG)]
            return _sigmoid(jnp.concatenate(rows, axis=1))

        o = (gate(0) * oc[g] + gate(1) * os_[g] + gate(2) * ow[g]).astype(BF16)
        heads = jnp.concatenate([o[:, h * Q_BLOCK:(h + 1) * Q_BLOCK] for h in range(NSA_HPG)], axis=0)
        o_ref[0, :, g * NSA_HPG * HEAD_DIM:(g + 1) * NSA_HPG * HEAD_DIM] = _dot_nt(eye, heads).astype(BF16)


def _nsa_call(q, ngt, kc, vct, ovt, bct, ks, vst, vst2, kw, vwt, bt0, bt1):
    B, _, _, S = q.shape
    G = NSA_KV_GROUPS
    nq = S // Q_BLOCK
    per_b = lambda a: pl.BlockSpec((1,) + a.shape[1:], lambda b, i: (b,) + (0,) * (a.ndim - 1),
                                   pipeline_mode=pl.Buffered(1))
    in_specs = [
        pl.BlockSpec((1, NSA_HEADS, LANE, Q_BLOCK), lambda b, i: (b, 0, 0, i)),
        pl.BlockSpec((1, LANE, Q_BLOCK), lambda b, i: (b, 0, i)),
        per_b(kc), per_b(vct), _full(ovt.shape), _full(bct.shape),
        per_b(ks), per_b(vst), per_b(vst2), per_b(kw), per_b(vwt), _full(bt0.shape), _full(bt1.shape),
    ]
    scratch = [
        pltpu.VMEM((G, CMP_ROWS, QCOLS), F32),
        pltpu.VMEM((G, 2, 2 * LANE, QCOLS), BF16),
        pltpu.VMEM((G, 2 * LANE, QCOLS), BF16),
        pltpu.VMEM((G, 1, QCOLS), F32),
        pltpu.VMEM((G, V_ROWS, QCOLS), F32),
    ]
    scratch += [pltpu.VMEM((FAR_KEYS, QCOLS), F32)] * (G * FAR_BUFS)
    scratch += [pltpu.VMEM((1, QCOLS), F32)] * (G * FAR_BUFS)
    return pl.pallas_call(
        _nsa_kernel, grid=(B, nq), in_specs=in_specs,
        out_specs=pl.BlockSpec((1, Q_BLOCK, NSA_WIDTH), lambda b, i: (b, i, 0)),
        out_shape=jax.ShapeDtypeStruct((B, S, NSA_WIDTH), BF16),
        scratch_shapes=scratch, compiler_params=_params(2), name="sparse_attention",
    )(q, ngt, kc, vct, ovt, bct, ks, vst, vst2, kw, vwt, bt0, bt1)


def _merge_kernel(h_ref, a_ref, o_ref, wmg_ref, bmg_ref, wa_ref, wb_ref, wout_ref, lng_ref, lnb_ref, out_ref):
    tm = h_ref.shape[1]
    halves = [slice(k * tm // 2, (k + 1) * tm // 2) for k in range(2)]
    pre = []
    for rows in halves:
        mg = _dot(h_ref[0, rows].astype(BF16), wmg_ref[...]) + bmg_ref[...]
        pre.append((mg, _dot(a_ref[0, rows], wa_ref[...]), _dot(o_ref[0, rows], wb_ref[...])))
    ms = []
    for mg, pa, pb in pre:
        y = _sigmoid(mg[:, :D_MODEL]) * pa + _sigmoid(mg[:, D_MODEL:]) * pb
        ms.append(_dot(y.astype(BF16), wout_ref[...]))
    for rows, m in zip(halves, ms):
        out_ref[0, rows] = _layer_norm(ALPHA * h_ref[0, rows] + m, lng_ref[...], lnb_ref[...])


def _merge_call(h, a, o, p):
    B, S, _ = h.shape
    tm = 2 * ROW_TILE
    row = lambda w: pl.BlockSpec((1, tm, w), lambda b, i: (b, i, 0))
    ws = (p["w_mg"], p["b_mg"], p["w_proj_a"], p["w_proj_b"], p["w_out"], p["ln1_g"], p["ln1_b"])
    return pl.pallas_call(
        _merge_kernel, grid=(B, S // tm),
        in_specs=[row(D_MODEL), row(GM_WIDTH), row(NSA_WIDTH)] + [_full(a.shape) for a in ws],
        out_specs=row(D_MODEL), out_shape=jax.ShapeDtypeStruct((B, S, D_MODEL), F32),
        compiler_params=_params(2), name="merge_norm")(h, a, o, *ws)


def _ffn_kernel(h_ref, wup_ref, cw_ref, wdown_ref, lng_ref, lnb_ref, out_ref, carry_ref):
    tm = h_ref.shape[1]

    @pl.when(pl.program_id(1) == 0)
    def _():
        carry_ref[...] = jnp.zeros(carry_ref.shape, F32)

    h = h_ref[0]
    hb = h.astype(BF16)
    row8 = lax.broadcasted_iota(jnp.int32, (8, FF_CHUNK), 0)

    def up_proj(c):
        cols = lambda idx: slice(idx * FF_CHUNK, (idx + 1) * FF_CHUNK)
        return _dot(hb, wup_ref[:, cols(c)]), _dot(hb, wup_ref[:, cols(N_FF + c)])

    def conv(up, idx):
        prev = carry_ref[idx]
        carry_ref[idx] = up[tm - 8:tm]
        cw = cw_ref[:, idx * FF_CHUNK:(idx + 1) * FF_CHUNK]
        out = cw[3:4] + cw[2:3] * up
        for shift in (1, 2):
            rolled = pltpu.roll(up, shift, 0)
            head = jnp.where(row8 < shift, pltpu.roll(prev, shift, 0), rolled[0:8])
            out = out + cw[2 - shift:3 - shift] * jnp.concatenate([head, rolled[8:]], axis=0)
        return out

    f = jnp.zeros((tm, D_MODEL), F32)
    ups = [up_proj(c) for c in range(FF_AHEAD)]
    for c in range(N_FF):
        g_up, v_up = ups.pop(0)
        if c + FF_AHEAD < N_FF:
            ups.append(up_proj(c + FF_AHEAD))
        gate = conv(g_up, c)
        act = (gate * _sigmoid(gate) * conv(v_up, N_FF + c)).astype(BF16)
        f = f + _dot(act, wdown_ref[c])
    out_ref[0] = _layer_norm(ALPHA * h + f, lng_ref[...], lnb_ref[...])


def _ffn_call(h, p):
    B, S, _ = h.shape
    tm = ROW_TILE
    row = pl.BlockSpec((1, tm, D_MODEL), lambda b, i: (b, i, 0))
    ws = (p["ffn_w_up"], p["ffn_cw"], p["ffn_w_down"], p["ln2_g"], p["ln2_b"])
    return pl.pallas_call(
        _ffn_kernel, grid=(B, S // tm),
        in_specs=[row] + [_full(a.shape) for a in ws],
        out_specs=row, out_shape=jax.ShapeDtypeStruct((B, S, D_MODEL), F32),
        scratch_shapes=[pltpu.VMEM((2 * N_FF, 8, FF_CHUNK), F32)],
        compiler_params=_params(2), name="conv_ffn_norm")(h, *ws)


def _t5_bucket(n):
    max_exact = NUM_BUCKETS // 2
    log_ratio = jnp.log(jnp.maximum(n, 1).astype(F32) / max_exact) / math.log(MAX_DISTANCE / max_exact)
    large = jnp.minimum(max_exact + (log_ratio * (NUM_BUCKETS - max_exact)).astype(jnp.int32), NUM_BUCKETS - 1)
    return jnp.where(n < max_exact, n, large)


def _bias_tiles(rel_bias):
    G, H = NSA_KV_GROUPS, NSA_HPG
    tab = rel_bias[_t5_bucket(jnp.arange(2 * KEY_TILE, dtype=jnp.int32))] - rel_bias[NUM_BUCKETS - 1]
    tab = tab * LOG2E

    def tile(offset, n, row_step=1):
        d = np.arange(2 * n)
        dist = offset + np.where(d < n, d, d - 2 * n)
        w = jnp.where((dist >= 0)[:, None], tab[np.clip(dist, 0, 2 * KEY_TILE - 1)], NEG_INF)
        m = jnp.broadcast_to(w[None], (n, 2 * n, w.shape[1])).reshape(2 * n * n, -1)[:n * (2 * n - 1)]
        t = m.reshape(n, 2 * n - 1, -1)[::row_step, :Q_BLOCK]
        rows = t.shape[0]
        return t.reshape(rows, Q_BLOCK, G, H).transpose(2, 0, 3, 1).reshape(G, rows, H * Q_BLOCK)

    bt0 = tile(0, KEY_TILE)
    bt1 = tile(KEY_TILE, KEY_TILE)
    bct = tile(CMP_STRIDE * CMP_PAD - CMP_BLOCK + 1, CMP_STRIDE * CMP_BAND, CMP_STRIDE)
    return bt0, bt1, bct


def _overlap_t(seq):
    n_cmp = seq // CMP_STRIDE - 1
    n_blk = seq // SEL_BLOCK
    cs = np.arange(n_cmp)[:, None] * CMP_STRIDE
    ss = np.arange(n_blk)[None, :] * SEL_BLOCK
    ov = np.maximum(np.minimum(cs + CMP_BLOCK, ss + SEL_BLOCK) - np.maximum(cs, ss), 0) / CMP_BLOCK
    out = np.zeros((LANE, CMP_ROWS), np.float32)
    out[:n_blk, CMP_PAD:CMP_PAD + n_cmp] = ov.T
    return jnp.asarray(out, BF16)


def _pad_cols(w, width):
    return jnp.pad(w, ((0, 0), (0, width - w.shape[1])))


def _layer_params(l, w_in, b_in, gm_ln_g, gm_ln_b, gm_ws, gm_bs, cmp_pos, cmp_w1, cmp_b1, cmp_w2,
                  w_proj_a, w_proj_b, w_out, ln1_g, ln1_b, ffn_w_up, ffn_conv_w, ffn_conv_b, ffn_w_down,
                  ln2_g, ln2_b):
    G = NSA_KV_GROUPS
    w, b = w_in[l], b_in[l]
    o_q = 2 * GM_WIDTH
    o_kc = o_q + NSA_WIDTH
    o_vc, o_ks, o_vs, o_kw, o_vw = (o_kc + KV_WIDTH * k for k in range(1, 6))
    o_ng = o_vw + KV_WIDTH
    n_ng = NSA_HEADS * N_GATES
    o_mg = o_ng + n_ng
    seg = lambda a, o, n: a[..., o:o + n]

    wq = jnp.pad(seg(w, o_q, NSA_WIDTH).T.reshape(NSA_HEADS, HEAD_DIM, D_MODEL),
                 ((0, 0), (0, LANE - HEAD_DIM), (0, 0))).reshape(NSA_HEADS * LANE, D_MODEL)
    bq = jnp.pad(seg(b, o_q, NSA_WIDTH).reshape(NSA_HEADS, HEAD_DIM), ((0, 0), (0, LANE - HEAD_DIM)))
    bq = bq.at[:, HEAD_DIM].set(NEG_INF / Q_SCALE).reshape(NSA_HEADS * LANE)

    def k_groups(o):
        return [_pad_cols(seg(w, o + g * HEAD_DIM, HEAD_DIM), LANE) for g in range(G)], \
               [jnp.pad(seg(b, o + g * HEAD_DIM, HEAD_DIM), (0, LANE - HEAD_DIM)) for g in range(G)]

    wks, bks = k_groups(o_ks)
    wkw, bkw = k_groups(o_kw)
    w_r = jnp.concatenate([seg(w, o_kc, KV_WIDTH), seg(w, o_vc, KV_WIDTH), *wks, *wkw,
                           _pad_cols(seg(w, o_ng, n_ng), LANE)], axis=1)
    b_r = jnp.concatenate([seg(b, o_kc, KV_WIDTH), seg(b, o_vc, KV_WIDTH), *bks, *bkw,
                           jnp.pad(seg(b, o_ng, n_ng), (0, LANE - n_ng))])

    def v_rows(o):
        wt = jnp.pad(seg(w, o, HEAD_DIM).T, ((0, V_ROWS - HEAD_DIM), (0, 0)))
        bt = jnp.pad(seg(b, o, HEAD_DIM), (0, V_ROWS - HEAD_DIM)).at[HEAD_DIM].set(1.0)
        return wt, bt

    vparts = [v_rows(o + g * HEAD_DIM) for o in (o_vs, o_vw) for g in range(G)]
    w_t = jnp.concatenate([wq] + [wt for wt, _ in vparts], axis=0)
    b_t = jnp.concatenate([bq] + [bt for _, bt in vparts])
    row_scale = jnp.where(jnp.arange(b_t.shape[0]) < NSA_HEADS * LANE, Q_SCALE, 1.0)
    b_t = jnp.stack([b_t, row_scale], axis=1)

    causal = jnp.tril(jnp.ones((GM_CHUNK, GM_CHUNK), F32))
    gw = GM_WIDTH // GM_GROUPS
    bs = gm_bs[l]
    bsb = jnp.concatenate([jnp.broadcast_to(bs[0::2, :, None], (GM_GROUPS // 2, GM_CHUNK, gw)),
                           jnp.broadcast_to(bs[1::2, :, None], (GM_GROUPS // 2, GM_CHUNK, gw))], axis=2)

    w1 = cmp_w1[l].reshape(2, CMP_BLOCK, HEAD_DIM, CMP_HIDDEN)
    w1 = jnp.stack([jnp.pad(w1, ((0, 0), (0, 0), (g * HEAD_DIM, (G - 1 - g) * HEAD_DIM), (0, 0)))
                    for g in range(G)], axis=1)
    w2 = cmp_w2[l]
    cw = jnp.concatenate([ffn_conv_w[l], ffn_conv_b[l][None], jnp.zeros((8 - CONV_WIDTH - 1, 2 * D_FF), F32)])
    return {
        "w_uv": seg(w, 0, 2 * GM_WIDTH).astype(BF16), "b_uv": seg(b, 0, 2 * GM_WIDTH)[None],
        "gm_ln_g": gm_ln_g[l][None], "gm_ln_b": gm_ln_b[l][None],
        "gm_wc": (gm_ws[l] * causal).astype(BF16), "gm_bsb": bsb,
        "w_r": w_r.astype(BF16), "b_r": b_r[None], "w_t": w_t.astype(BF16), "b_t": b_t,
        "cmp_pos": jnp.tile(cmp_pos[l], (1, 1, G)), "cmp_w1": w1.astype(BF16), "cmp_b1": cmp_b1[l][:, None],
        "cmp_w2k": _pad_cols(w2[0], LANE).astype(BF16),
        "cmp_w2vt": jnp.pad(w2[1].T, ((0, V_ROWS - HEAD_DIM), (0, 0))).astype(BF16),
        "w_mg": seg(w, o_mg, 2 * D_MODEL).astype(BF16), "b_mg": seg(b, o_mg, 2 * D_MODEL)[None],
        "w_proj_a": w_proj_a[l].astype(BF16), "w_proj_b": w_proj_b[l].astype(BF16),
        "w_out": w_out[l].astype(BF16), "ln1_g": ln1_g[l][None], "ln1_b": ln1_b[l][None],
        "ffn_w_up": ffn_w_up[l].astype(BF16), "ffn_cw": cw,
        "ffn_w_down": ffn_w_down[l].reshape(N_FF, FF_CHUNK, D_MODEL).astype(BF16),
        "ln2_g": ln2_g[l][None], "ln2_b": ln2_b[l][None],
    }


def kernel(x, w_in, b_in, gm_ln_g, gm_ln_b, gm_ws, gm_bs, cmp_pos, cmp_w1, cmp_b1, cmp_w2, rel_bias,
           w_proj_a, w_proj_b, w_out, ln1_g, ln1_b, ffn_w_up, ffn_conv_w, ffn_conv_b, ffn_w_down,
           ln2_g, ln2_b):
    B, S, _ = x.shape
    assert S % FAR_KEYS == 0 and N_SEL <= S // SEL_BLOCK <= LANE
    assert S // CMP_STRIDE - 1 + CMP_PAD < CMP_ROWS
    bt0, bt1, bct = _bias_tiles(rel_bias)
    ovt = _overlap_t(S)
    h = x
    for l in range(DEPTH):
        p = _layer_params(l, w_in, b_in, gm_ln_g, gm_ln_b, gm_ws, gm_bs, cmp_pos, cmp_w1, cmp_b1, cmp_w2,
                          w_proj_a, w_proj_b, w_out, ln1_g, ln1_b, ffn_w_up, ffn_conv_w, ffn_conv_b,
                          ffn_w_down, ln2_g, ln2_b)
        a, q, kvc, ks, kw, vst, vst2, vwt, ngt = _proj_call(h, p)
        kc, vct = _compress_call(kvc, p)
        o = _nsa_call(q, ngt, kc, vct, ovt, bct, ks, vst, vst2, kw, vwt, bt0, bt1)
        h = _merge_call(h, a, o, p)
        h = _ffn_call(h, p)
    return h
```

```python
import math

import jax
import jax.numpy as jnp
import numpy as np
from jax import lax
from jax.experimental import pallas as pl
from jax.experimental.pallas import tpu as pltpu

D_MODEL = 1024
DEPTH = 2
GM_WIDTH = D_MODEL // 2
GM_GROUPS = 8
GM_CHUNK = 128
NSA_HEADS = 8
NSA_KV_GROUPS = 2
NSA_HPG = NSA_HEADS // NSA_KV_GROUPS
HEAD_DIM = 64
NSA_WIDTH = NSA_HEADS * HEAD_DIM
KV_WIDTH = NSA_KV_GROUPS * HEAD_DIM
N_GATES = 3
CMP_BLOCK = 32
CMP_STRIDE = 16
CMP_HIDDEN = 256
SEL_BLOCK = 64
N_SEL = 16
WINDOW = 512
Q_BLOCK = 128
NUM_BUCKETS = 32
MAX_DISTANCE = 128
D_FF = 2816
CONV_WIDTH = 3
ALPHA = (2.0 * DEPTH) ** 0.25
LN_EPS = 1e-5
FORCED_SCORE = 1e6
NEG_INF = -1e30
M_INIT = -1e29
LOG2E = math.log2(math.e)
Q_SCALE = HEAD_DIM ** -0.5 * LOG2E

LANE = 128
QCOLS = NSA_HPG * Q_BLOCK
KEY_TILE = 128
FAR_TILES = 4
FAR_KEYS = FAR_TILES * KEY_TILE
FAR_AHEAD = 2
FAR_BUFS = 2
BAND_TILES = WINDOW // KEY_TILE + 1
PAD_TILES = BAND_TILES - 1
CMP_PAD = 8
CMP_ROWS = 640
CMP_BAND = 16
V_ROWS = 80
FF_CHUNK = 256
N_FF = D_FF // FF_CHUNK
FF_AHEAD = 2
ROW_TILE = 256
VMEM_LIMIT = 56 * 1024 * 1024

F32 = jnp.float32
BF16 = jnp.bfloat16
NT_DIMS = (((1,), (1,)), ((), ()))


def _dot(a, b):
    return jnp.dot(a, b, preferred_element_type=F32)


def _dot_nt(a, b):
    return lax.dot_general(a, b, NT_DIMS, preferred_element_type=F32)


def _layer_norm(x, g, b):
    mu = jnp.mean(x, axis=-1, keepdims=True)
    xc = x - mu
    var = jnp.mean(xc * xc, axis=-1, keepdims=True)
    return xc * lax.rsqrt(var + LN_EPS) * g + b


def _gelu_tanh(x):
    c = math.sqrt(2.0 / math.pi)
    return x * (0.5 * (1.0 + jnp.tanh(c * (x + 0.044715 * (x * x * x)))))


def _sigmoid(x):
    return 1.0 / (1.0 + jnp.exp(-x))


def _params(n_axes):
    return pltpu.CompilerParams(dimension_semantics=("arbitrary",) * n_axes,
                                vmem_limit_bytes=VMEM_LIMIT)


def _full(shape):
    n = len(shape)
    return pl.BlockSpec(shape, lambda *_: (0,) * n)


def _proj_kernel(h_ref, wuv_ref, buv_ref, lng_ref, lnb_ref, wc_ref, bsb_ref, wr_ref, br_ref, wt_ref, bt_ref,
                 a_ref, q_ref, kvc_ref, ks_ref, kw_ref, vst_ref, vst2_ref, vwt_ref, ngt_ref):
    tm = h_ref.shape[1]
    G = NSA_KV_GROUPS
    hb = h_ref[0].astype(BF16)

    uv = _dot(hb, wuv_ref[...]) + buv_ref[...]
    r = _dot(hb, wr_ref[...]) + br_ref[...]
    tr = ((_dot_nt(wt_ref[...], hb) + bt_ref[:, 0:1]) * bt_ref[:, 1:2]).astype(BF16)

    u = _gelu_tanh(uv[:, :GM_WIDTH])
    v = _layer_norm(_gelu_tanh(uv[:, GM_WIDTH:]), lng_ref[...], lnb_ref[...]).astype(BF16)
    lane = lax.broadcasted_iota(jnp.int32, (GM_CHUNK, LANE), 1)
    group_w = GM_WIDTH // GM_GROUPS
    for ch in range(tm // GM_CHUNK):
        rows = slice(ch * GM_CHUNK, (ch + 1) * GM_CHUNK)
        for pr in range(GM_GROUPS // 2):
            cols = slice(pr * LANE, (pr + 1) * LANE)
            x = v[rows, cols]
            sv = jnp.where(lane < group_w, _dot(wc_ref[2 * pr], x), _dot(wc_ref[2 * pr + 1], x))
            a_ref[0, rows, cols] = (u[rows, cols] * (sv + bsb_ref[pr])).astype(BF16)

    for hh in range(NSA_HEADS):
        q_ref[0, hh] = tr[hh * LANE:(hh + 1) * LANE]
    vt = tr[NSA_HEADS * LANE:]

    kvc_ref[0, 0] = r[:, 0:LANE]
    kvc_ref[0, 1] = r[:, LANE:2 * LANE]
    token = pl.program_id(1) * tm + lax.broadcasted_iota(jnp.int32, (tm, LANE), 0)
    sel_block = lax.shift_right_logical(token, int(math.log2(SEL_BLOCK)))
    onehot = (lax.broadcasted_iota(jnp.int32, (tm, LANE), 1) == sel_block).astype(BF16)
    for g in range(G):
        ks_ref[0, g, :, 0:LANE] = onehot
        ks_ref[0, g, :, LANE:2 * LANE] = r[:, (2 + g) * LANE:(3 + g) * LANE].astype(BF16)
        kw_ref[0, g] = r[:, (4 + g) * LANE:(5 + g) * LANE].astype(BF16)
    ngt_ref[0] = r[:, 6 * LANE:7 * LANE].T

    for g in range(G):
        sel_rows = slice(g * V_ROWS, (g + 1) * V_ROWS)
        win_rows = slice((G + g) * V_ROWS, (G + g + 1) * V_ROWS)
        vst2_ref[0, g, 0] = vt[sel_rows]
        for t in range(tm // KEY_TILE):
            cols = slice(t * KEY_TILE, (t + 1) * KEY_TILE)
            vst_ref[0, g, t] = vt[sel_rows, cols]
            vwt_ref[0, g, t] = vt[win_rows, cols]


def _proj_call(h, p):
    B, S, _ = h.shape
    tm = 2 * KEY_TILE
    G = NSA_KV_GROUPS
    nt = S // KEY_TILE
    row = lambda w: pl.BlockSpec((1, tm, w), lambda b, i: (b, i, 0))
    lead = lambda n, w: pl.BlockSpec((1, n, tm, w), lambda b, i: (b, 0, i, 0))
    vtile = lambda n, w: pl.BlockSpec((1, G, n, V_ROWS, w), lambda b, i: (b, 0, i, 0, 0))
    out_shape = (
        jax.ShapeDtypeStruct((B, S, GM_WIDTH), BF16),
        jax.ShapeDtypeStruct((B, NSA_HEADS, LANE, S), BF16),
        jax.ShapeDtypeStruct((B, 2, S, LANE), F32),
        jax.ShapeDtypeStruct((B, G, S, 2 * LANE), BF16),
        jax.ShapeDtypeStruct((B, G, S, LANE), BF16),
        jax.ShapeDtypeStruct((B, G, nt, V_ROWS, KEY_TILE), BF16),
        jax.ShapeDtypeStruct((B, G, nt // 2, V_ROWS, 2 * KEY_TILE), BF16),
        jax.ShapeDtypeStruct((B, G, nt, V_ROWS, KEY_TILE), BF16),
        jax.ShapeDtypeStruct((B, LANE, S), F32),
    )
    out_specs = (row(GM_WIDTH), pl.BlockSpec((1, NSA_HEADS, LANE, tm), lambda b, i: (b, 0, 0, i)),
                 lead(2, LANE), lead(G, 2 * LANE), lead(G, LANE),
                 vtile(tm // KEY_TILE, KEY_TILE), vtile(1, 2 * KEY_TILE), vtile(tm // KEY_TILE, KEY_TILE),
                 pl.BlockSpec((1, LANE, tm), lambda b, i: (b, 0, i)))
    in_arrays = (h, p["w_uv"], p["b_uv"], p["gm_ln_g"], p["gm_ln_b"], p["gm_wc"], p["gm_bsb"],
                 p["w_r"], p["b_r"], p["w_t"], p["b_t"])
    in_specs = [row(D_MODEL)] + [_full(a.shape) for a in in_arrays[1:]]
    return pl.pallas_call(
        _proj_kernel, grid=(B, S // tm), in_specs=in_specs, out_specs=out_specs,
        out_shape=out_shape, compiler_params=_params(2), name="proj_gmlp")(*in_arrays)


def _compress_kernel(x_ref, pos_ref, w1_ref, b1_ref, w2k_ref, w2vt_ref, kc_ref, vct_ref):
    kv = pl.program_id(1)
    n = x_ref.shape[2] // CMP_STRIDE
    half = CMP_BLOCK // 2
    pad_row = (lax.broadcasted_iota(jnp.int32, (1, LANE), 1) == HEAD_DIM).astype(F32)
    for g in range(NSA_KV_GROUPS):
        first = jnp.zeros((n, CMP_HIDDEN), F32)
        second = jnp.zeros((n, CMP_HIDDEN), F32)
        for r in range(half):
            xr = x_ref[0, 0, pl.ds(r, n, stride=CMP_STRIDE), :]
            first = first + _dot((xr + pos_ref[0, r:r + 1, :]).astype(BF16), w1_ref[0, g, r])
            second = second + _dot((xr + pos_ref[0, half + r:half + r + 1, :]).astype(BF16),
                                   w1_ref[0, g, half + r])
        hid = first + pltpu.roll(second, n - 1, 0) + b1_ref[0]
        act = (hid * _sigmoid(hid)).astype(BF16)

        @pl.when(kv == 0)
        def _():
            res = _dot(act, w2k_ref[...])
            row = lax.broadcasted_iota(jnp.int32, res.shape, 0)
            res = jnp.where(row == n - 1, pad_row, res)
            kc_ref[0, g] = jnp.concatenate(
                [jnp.broadcast_to(pad_row, (CMP_PAD, LANE)), res,
                 jnp.broadcast_to(pad_row, (CMP_ROWS - CMP_PAD - n, LANE))], axis=0).astype(BF16)

        @pl.when(kv == 1)
        def _():
            vt = _dot_nt(w2vt_ref[...], act)
            col = lax.broadcasted_iota(jnp.int32, vt.shape, 1)
            vt = jnp.where(col == n - 1, 0.0, vt)
            vt = jnp.concatenate([vt, jnp.zeros((V_ROWS, CMP_ROWS - n), F32)], axis=1)
            vct_ref[0, g] = pltpu.roll(vt, CMP_PAD, 1).astype(BF16)


def _compress_call(kvc, p):
    B, _, S, _ = kvc.shape
    G = NSA_KV_GROUPS
    per_kv = lambda a: pl.BlockSpec((1,) + a.shape[1:], lambda b, k: (k,) + (0,) * (a.ndim - 1))
    return pl.pallas_call(
        _compress_kernel, grid=(B, 2),
        in_specs=[pl.BlockSpec((1, 1, S, LANE), lambda b, k: (b, k, 0, 0)),
                  per_kv(p["cmp_pos"]), per_kv(p["cmp_w1"]), per_kv(p["cmp_b1"]),
                  _full(p["cmp_w2k"].shape), _full(p["cmp_w2vt"].shape)],
        out_specs=(pl.BlockSpec((1, G, CMP_ROWS, LANE), lambda b, k: (b, 0, 0, 0)),
                   pl.BlockSpec((1, G, V_ROWS, CMP_ROWS), lambda b, k: (b, 0, 0, 0))),
        out_shape=(jax.ShapeDtypeStruct((B, G, CMP_ROWS, LANE), BF16),
                   jax.ShapeDtypeStruct((B, G, V_ROWS, CMP_ROWS), BF16)),
        compiler_params=_params(2), name="kv_compress",
    )(kvc, p["cmp_pos"], p["cmp_w1"], p["cmp_b1"], p["cmp_w2k"], p["cmp_w2vt"])


def _nsa_kernel(q_ref, ngt_ref, kc_ref, vct_ref, ovt_ref, bct_ref, ks_ref, vst_ref, vst2_ref, kw_ref, vwt_ref,
                 bt0_ref, bt1_ref, o_ref, sc_ref, rhsf_ref, rhst_ref, m_ref, acc_ref, *ring_refs):
    G = NSA_KV_GROUPS
    groups = range(G)
    sbuf_refs = [ring_refs[g * FAR_BUFS:(g + 1) * FAR_BUFS] for g in groups]
    mbuf_refs = [ring_refs[(G + g) * FAR_BUFS:(G + g + 1) * FAR_BUFS] for g in groups]
    i = pl.program_id(1)
    n_blk = LANE
    blk = lax.broadcasted_iota(jnp.int32, (n_blk, Q_BLOCK), 0)
    qi = lax.broadcasted_iota(jnp.int32, (n_blk, Q_BLOCK), 1)
    eye = (blk == qi).astype(BF16)
    qt = [jnp.concatenate([q_ref[0, g * NSA_HPG + h] for h in range(NSA_HPG)], axis=1) for g in groups]
    bt0 = [bt0_ref[g] for g in groups]
    bt1 = [bt1_ref[g] for g in groups]

    def band_logits(k_ref, g, rhs, extra, flag_col, first=0):
        width = k_ref.shape[3]
        before_start = (lax.broadcasted_iota(jnp.int32, (KEY_TILE, width), 1) == flag_col).astype(BF16)
        tiles = []
        for t in range(first, BAND_TILES):
            tau = i - PAD_TILES + t
            row0 = pl.multiple_of(jnp.maximum(tau, 0) * KEY_TILE, KEY_TILE)
            k = k_ref[0, g, pl.ds(row0, KEY_TILE), :]
            if t < PAD_TILES:
                k = jnp.where(jnp.broadcast_to(tau, k.shape) >= 0, k, before_start)
            s = _dot(k, rhs)
            tiles.append(s if extra[t] is None else s + extra[t])
        return tiles

    def band_softmax(tiles, vt_ref, g, m_old, acc_old):
        m_new = m_old
        for s in tiles:
            m_new = jnp.maximum(m_new, jnp.max(s, axis=0, keepdims=True))
        acc = jnp.exp2(m_old - m_new) * acc_old
        for t, s in enumerate(tiles, BAND_TILES - len(tiles)):
            vt = vt_ref[0, g, jnp.maximum(i - PAD_TILES + t, 0)]
            acc = acc + _dot(vt, jnp.exp2(s - m_new).astype(BF16))
        return acc[0:HEAD_DIM] / acc[HEAD_DIM:HEAD_DIM + 1]

    band = pl.multiple_of(8 * i, 8)
    visible = 8 * i + CMP_BAND

    def compressed_stage(rows):
        def softmax(g):
            s = sc_ref[g, 0:rows]
            key = lax.broadcasted_iota(jnp.int32, s.shape, 0)
            s = jnp.where(key >= visible, NEG_INF, s)
            m = jnp.maximum(jnp.max(s, axis=0, keepdims=True), M_INIT)
            p = jnp.exp2(s - m)
            l = jnp.sum(p, axis=0, keepdims=True)
            r = jnp.where(l > 0.0, 1.0 / l, 0.0)
            oc = _dot(vct_ref[0, g, :, 0:rows], p.astype(BF16))[0:HEAD_DIM] * r
            pn = p * r
            psum = pn[:, 0:Q_BLOCK]
            for h in range(1, NSA_HPG):
                psum = psum + pn[:, h * Q_BLOCK:(h + 1) * Q_BLOCK]
            p_hi = psum.astype(BF16)
            p_lo = (psum - p_hi.astype(F32)).astype(BF16)
            ovt = ovt_ref[:, 0:rows]
            return oc, _dot(ovt, p_hi) + _dot(ovt, p_lo)

        def run():
            for g in groups:
                sc_ref[g, 0:rows] = _dot(kc_ref[0, g, 0:rows], qt[g])
                sc_ref[g, pl.ds(band, CMP_BAND), :] += bct_ref[g]
            return tuple(x for g in groups for x in softmax(g))
        return run

    row_options = tuple(range(LANE, CMP_ROWS + 1, LANE))
    flat = lax.switch((visible + LANE - 1) // LANE - 1, [compressed_stage(rows) for rows in row_options])
    oc, imp = flat[0::2], flat[1::2]

    cur = (Q_BLOCK // SEL_BLOCK) * i + (qi >= SEL_BLOCK).astype(jnp.int32)
    forced = (blk == 0) | (blk == cur) | (blk == cur - 1)
    taken = -3e38
    blk_f = blk.astype(F32)
    start = tuple((jnp.where(forced, taken, jnp.where(blk > cur, NEG_INF, imp[g])), forced.astype(F32))
                  for g in groups)

    def select_stage(rows):
        def pick(_, carry):
            out = []
            for val, sel in carry:
                mx = jnp.max(val, axis=0, keepdims=True)
                first = jnp.min(jnp.where(val == mx, blk_f[0:rows], 1e9), axis=0, keepdims=True)
                hit = blk_f[0:rows] == first
                out.append((jnp.where(hit, taken, val), jnp.where(hit, 1.0, sel)))
            return tuple(out)

        def run():
            head = lax.fori_loop(0, N_SEL - 3, pick, tuple((v[0:rows], s[0:rows]) for v, s in start))
            return tuple(jnp.concatenate([sel, jnp.zeros((n_blk - rows, Q_BLOCK), F32)], axis=0)
                         if rows < n_blk else sel for _, sel in head)
        return run

    row_step = n_blk // 4
    row_options = tuple(range(row_step, n_blk + 1, row_step))
    n_cand = (Q_BLOCK // SEL_BLOCK) * (i + 1)
    picked = lax.switch((n_cand + row_step - 1) // row_step - 1, [select_stage(rows) for rows in row_options])

    n_far = jnp.maximum(i - 1, 0) // FAR_TILES
    last_far = jnp.maximum(n_far - 1, 0)
    per_head = lambda x: jnp.concatenate([x.astype(BF16)] * NSA_HPG, axis=1)
    for g in groups:
        neg_far = jnp.where((picked[g] > 0.5) & (blk <= cur), 0.0, NEG_INF)
        neg_tail = jnp.where(blk >= n_far * (FAR_KEYS // SEL_BLOCK), neg_far, NEG_INF)
        rhsf_ref[g, 0, 0:LANE] = per_head(neg_far)
        rhsf_ref[g, 1, 0:LANE] = jnp.full((LANE, QCOLS), NEG_INF, BF16)
        rhst_ref[g, 0:LANE] = per_head(neg_tail)
        rhsf_ref[g, 0, LANE:2 * LANE] = qt[g]
        rhsf_ref[g, 1, LANE:2 * LANE] = qt[g]
        rhst_ref[g, LANE:2 * LANE] = qt[g]
        m_ref[g] = jnp.full(m_ref.shape[1:], M_INIT, F32)
        acc_ref[g] = jnp.zeros(acc_ref.shape[1:], F32)

    def far_logits(g, c):
        r0 = pl.multiple_of(FAR_KEYS * jnp.minimum(c, last_far), FAR_KEYS)
        rhs = rhsf_ref[g, (c >= n_far).astype(jnp.int32)]
        s = _dot(ks_ref[0, g, pl.ds(r0, FAR_KEYS), :], rhs)
        return s, jnp.max(s, axis=0, keepdims=True)

    def far_softmax(g, c, s_ref, mx_ref):
        tile0 = (FAR_TILES // 2) * jnp.minimum(c, last_far)
        m_old = m_ref[g]
        m_new = jnp.maximum(m_old, mx_ref[...])
        p = jnp.exp2(s_ref[...] - m_new).astype(BF16)
        pv = jnp.zeros(acc_ref.shape[1:], F32)
        for t in range(FAR_TILES // 2):
            pv = pv + _dot(vst2_ref[0, g, tile0 + t], p[2 * t * KEY_TILE:2 * (t + 1) * KEY_TILE])
        acc_ref[g] = jnp.exp2(m_old - m_new) * acc_ref[g] + pv
        m_ref[g] = m_new

    bufs = [tuple(zip(sbuf_refs[g], mbuf_refs[g])) for g in groups]
    for u in range(FAR_AHEAD):
        for g in groups:
            bufs[g][u][0][...], bufs[g][u][1][...] = far_logits(g, u)

    kk = lax.broadcasted_iota(jnp.int32, (KEY_TILE, QCOLS), 0)
    qq = lax.broadcasted_iota(jnp.int32, (KEY_TILE, QCOLS), 1) & (Q_BLOCK - 1)
    edge = jnp.where(kk > qq, 0.0, NEG_INF)
    win_tiles = [band_logits(kw_ref, g, qt[g], [edge] + [None] * (BAND_TILES - 3) + [bt1[g], bt0[g]], HEAD_DIM)
                 for g in groups]
    ow = [band_softmax(win_tiles[g], vwt_ref, g, jnp.full(m_ref.shape[1:], M_INIT, F32),
                       jnp.zeros(acc_ref.shape[1:], F32)) for g in groups]

    n_trips = jnp.maximum(n_far - FAR_AHEAD + FAR_BUFS - 1, 0) // FAR_BUFS

    def far_ring(j, carry):
        for u in range(FAR_BUFS):
            c = FAR_BUFS * j + u
            ahead = [far_logits(g, c + FAR_AHEAD) for g in groups]
            for g in groups:
                far_softmax(g, c, *bufs[g][u])
            for g in groups:
                ahead_s, ahead_mx = bufs[g][(u + FAR_AHEAD) % FAR_BUFS]
                ahead_s[...], ahead_mx[...] = ahead[g]
        return carry

    lax.fori_loop(0, n_trips, far_ring, 0)

    for u in range(FAR_AHEAD):
        for g in groups:
            far_softmax(g, FAR_BUFS * n_trips + u, *bufs[g][u])

    def tail_stage(n_tiles):
        def run():
            first = BAND_TILES - n_tiles
            tiles = [band_logits(ks_ref, g, rhst_ref[g], [None] * (BAND_TILES - 2) + [bt1[g], bt0[g]],
                                 LANE + HEAD_DIM, first) for g in groups]
            return tuple(band_softmax(tiles[g], vst_ref, g, m_ref[g], acc_ref[g]) for g in groups)
        return run

    n_tail = jnp.maximum(i + 1 - FAR_TILES * n_far, 2)
    os_ = lax.switch(n_tail - 2, [tail_stage(n) for n in range(2, BAND_TILES + 1)])

    for g in groups:
        def gate(branch):
            rows = [ngt_ref[0, NSA_HPG * N_GATES * g + N_GATES * h + branch:
                            NSA_HPG * N_GATES * g + N_GATES * h + branch + 1, :] for h in range(NSA_HPG)]
            return _sigmoid(jnp.concatenate(rows, axis=1))

        o = (gate(0) * oc[g] + gate(1) * os_[g] + gate(2) * ow[g]).astype(BF16)
        heads = jnp.concatenate([o[:, h * Q_BLOCK:(h + 1) * Q_BLOCK] for h in range(NSA_HPG)], axis=0)
        o_ref[0, :, g * NSA_HPG * HEAD_DIM:(g + 1) * NSA_HPG * HEAD_DIM] = _dot_nt(eye, heads).astype(BF16)


def _nsa_call(q, ngt, kc, vct, ovt, bct, ks, vst, vst2, kw, vwt, bt0, bt1):
    B, _, _, S = q.shape
    G = NSA_KV_GROUPS
    nq = S // Q_BLOCK
    per_b = lambda a: pl.BlockSpec((1,) + a.shape[1:], lambda b, i: (b,) + (0,) * (a.ndim - 1),
                                   pipeline_mode=pl.Buffered(1))
    in_specs = [
        pl.BlockSpec((1, NSA_HEADS, LANE, Q_BLOCK), lambda b, i: (b, 0, 0, i)),
        pl.BlockSpec((1, LANE, Q_BLOCK), lambda b, i: (b, 0, i)),
        per_b(kc), per_b(vct), _full(ovt.shape), _full(bct.shape),
        per_b(ks), per_b(vst), per_b(vst2), per_b(kw), per_b(vwt), _full(bt0.shape), _full(bt1.shape),
    ]
    scratch = [
        pltpu.VMEM((G, CMP_ROWS, QCOLS), F32),
        pltpu.VMEM((G, 2, 2 * LANE, QCOLS), BF16),
        pltpu.VMEM((G, 2 * LANE, QCOLS), BF16),
        pltpu.VMEM((G, 1, QCOLS), F32),
        pltpu.VMEM((G, V_ROWS, QCOLS), F32),
    ]
    scratch += [pltpu.VMEM((FAR_KEYS, QCOLS), F32)] * (G * FAR_BUFS)
    scratch += [pltpu.VMEM((1, QCOLS), F32)] * (G * FAR_BUFS)
    return pl.pallas_call(
        _nsa_kernel, grid=(B, nq), in_specs=in_specs,
        out_specs=pl.BlockSpec((1, Q_BLOCK, NSA_WIDTH), lambda b, i: (b, i, 0)),
        out_shape=jax.ShapeDtypeStruct((B, S, NSA_WIDTH), BF16),
        scratch_shapes=scratch, compiler_params=_params(2), name="sparse_attention",
    )(q, ngt, kc, vct, ovt, bct, ks, vst, vst2, kw, vwt, bt0, bt1)


def _merge_kernel(h_ref, a_ref, o_ref, wmg_ref, bmg_ref, wa_ref, wb_ref, wout_ref, lng_ref, lnb_ref, out_ref):
    tm = h_ref.shape[1]
    halves = [slice(k * tm // 2, (k + 1) * tm // 2) for k in range(2)]
    pre = []
    for rows in halves:
        mg = _dot(h_ref[0, rows].astype(BF16), wmg_ref[...]) + bmg_ref[...]
        pre.append((mg, _dot(a_ref[0, rows], wa_ref[...]), _dot(o_ref[0, rows], wb_ref[...])))
    ms = []
    for mg, pa, pb in pre:
        y = _sigmoid(mg[:, :D_MODEL]) * pa + _sigmoid(mg[:, D_MODEL:]) * pb
        ms.append(_dot(y.astype(BF16), wout_ref[...]))
    for rows, m in zip(halves, ms):
        out_ref[0, rows] = _layer_norm(ALPHA * h_ref[0, rows] + m, lng_ref[...], lnb_ref[...])


def _merge_call(h, a, o, p):
    B, S, _ = h.shape
    tm = 2 * ROW_TILE
    row = lambda w: pl.BlockSpec((1, tm, w), lambda b, i: (b, i, 0))
    ws = (p["w_mg"], p["b_mg"], p["w_proj_a"], p["w_proj_b"], p["w_out"], p["ln1_g"], p["ln1_b"])
    return pl.pallas_call(
        _merge_kernel, grid=(B, S // tm),
        in_specs=[row(D_MODEL), row(GM_WIDTH), row(NSA_WIDTH)] + [_full(a.shape) for a in ws],
        out_specs=row(D_MODEL), out_shape=jax.ShapeDtypeStruct((B, S, D_MODEL), F32),
        compiler_params=_params(2), name="merge_norm")(h, a, o, *ws)


def _ffn_kernel(h_ref, wup_ref, cw_ref, wdown_ref, lng_ref, lnb_ref, out_ref, carry_ref):
    tm = h_ref.shape[1]

    @pl.when(pl.program_id(1) == 0)
    def _():
        carry_ref[...] = jnp.zeros(carry_ref.shape, F32)

    h = h_ref[0]
    hb = h.astype(BF16)
    row8 = lax.broadcasted_iota(jnp.int32, (8, FF_CHUNK), 0)

    def up_proj(c):
        cols = lambda idx: slice(idx * FF_CHUNK, (idx + 1) * FF_CHUNK)
        return _dot(hb, wup_ref[:, cols(c)]), _dot(hb, wup_ref[:, cols(N_FF + c)])

    def conv(up, idx):
        prev = carry_ref[idx]
        carry_ref[idx] = up[tm - 8:tm]
        cw = cw_ref[:, idx * FF_CHUNK:(idx + 1) * FF_CHUNK]
        out = cw[3:4] + cw[2:3] * up
        for shift in (1, 2):
            rolled = pltpu.roll(up, shift, 0)
            head = jnp.where(row8 < shift, pltpu.roll(prev, shift, 0), rolled[0:8])
            out = out + cw[2 - shift:3 - shift] * jnp.concatenate([head, rolled[8:]], axis=0)
        return out

    f = jnp.zeros((tm, D_MODEL), F32)
    ups = [up_proj(c) for c in range(FF_AHEAD)]
    for c in range(N_FF):
        g_up, v_up = ups.pop(0)
        if c + FF_AHEAD < N_FF:
            ups.append(up_proj(c + FF_AHEAD))
        gate = conv(g_up, c)
        act = (gate * _sigmoid(gate) * conv(v_up, N_FF + c)).astype(BF16)
        f = f + _dot(act, wdown_ref[c])
    out_ref[0] = _layer_norm(ALPHA * h + f, lng_ref[...], lnb_ref[...])


def _ffn_call(h, p):
    B, S, _ = h.shape
    tm = ROW_TILE
    row = pl.BlockSpec((1, tm, D_MODEL), lambda b, i: (b, i, 0))
    ws = (p["ffn_w_up"], p["ffn_cw"], p["ffn_w_down"], p["ln2_g"], p["ln2_b"])
    return pl.pallas_call(
        _ffn_kernel, grid=(B, S // tm),
        in_specs=[row] + [_full(a.shape) for a in ws],
        out_specs=row, out_shape=jax.ShapeDtypeStruct((B, S, D_MODEL), F32),
        scratch_shapes=[pltpu.VMEM((2 * N_FF, 8, FF_CHUNK), F32)],
        compiler_params=_params(2), name="conv_ffn_norm")(h, *ws)


def _t5_bucket(n):
    max_exact = NUM_BUCKETS // 2
    log_ratio = jnp.log(jnp.maximum(n, 1).astype(F32) / max_exact) / math.log(MAX_DISTANCE / max_exact)
    large = jnp.minimum(max_exact + (log_ratio * (NUM_BUCKETS - max_exact)).astype(jnp.int32), NUM_BUCKETS - 1)
    return jnp.where(n < max_exact, n, large)


def _bias_tiles(rel_bias):
    G, H = NSA_KV_GROUPS, NSA_HPG
    tab = rel_bias[_t5_bucket(jnp.arange(2 * KEY_TILE, dtype=jnp.int32))] - rel_bias[NUM_BUCKETS - 1]
    tab = tab * LOG2E

    def tile(offset, n, row_step=1):
        d = np.arange(2 * n)
        dist = offset + np.where(d < n, d, d - 2 * n)
        w = jnp.where((dist >= 0)[:, None], tab[np.clip(dist, 0, 2 * KEY_TILE - 1)], NEG_INF)
        m = jnp.broadcast_to(w[None], (n, 2 * n, w.shape[1])).reshape(2 * n * n, -1)[:n * (2 * n - 1)]
        t = m.reshape(n, 2 * n - 1, -1)[::row_step, :Q_BLOCK]
        rows = t.shape[0]
        return t.reshape(rows, Q_BLOCK, G, H).transpose(2, 0, 3, 1).reshape(G, rows, H * Q_BLOCK)

    bt0 = tile(0, KEY_TILE)
    bt1 = tile(KEY_TILE, KEY_TILE)
    bct = tile(CMP_STRIDE * CMP_PAD - CMP_BLOCK + 1, CMP_STRIDE * CMP_BAND, CMP_STRIDE)
    return bt0, bt1, bct


def _overlap_t(seq):
    n_cmp = seq // CMP_STRIDE - 1
    n_blk = seq // SEL_BLOCK
    cs = np.arange(n_cmp)[:, None] * CMP_STRIDE
    ss = np.arange(n_blk)[None, :] * SEL_BLOCK
    ov = np.maximum(np.minimum(cs + CMP_BLOCK, ss + SEL_BLOCK) - np.maximum(cs, ss), 0) / CMP_BLOCK
    out = np.zeros((LANE, CMP_ROWS), np.float32)
    out[:n_blk, CMP_PAD:CMP_PAD + n_cmp] = ov.T
    return jnp.asarray(out, BF16)


def _pad_cols(w, width):
    return jnp.pad(w, ((0, 0), (0, width - w.shape[1])))


def _layer_params(l, w_in, b_in, gm_ln_g, gm_ln_b, gm_ws, gm_bs, cmp_pos, cmp_w1, cmp_b1, cmp_w2,
                  w_proj_a, w_proj_b, w_out, ln1_g, ln1_b, ffn_w_up, ffn_conv_w, ffn_conv_b, ffn_w_down,
                  ln2_g, ln2_b):
    G = NSA_KV_GROUPS
    w, b = w_in[l], b_in[l]
    o_q = 2 * GM_WIDTH
    o_kc = o_q + NSA_WIDTH
    o_vc, o_ks, o_vs, o_kw, o_vw = (o_kc + KV_WIDTH * k for k in range(1, 6))
    o_ng = o_vw + KV_WIDTH
    n_ng = NSA_HEADS * N_GATES
    o_mg = o_ng + n_ng
    seg = lambda a, o, n: a[..., o:o + n]

    wq = jnp.pad(seg(w, o_q, NSA_WIDTH).T.reshape(NSA_HEADS, HEAD_DIM, D_MODEL),
                 ((0, 0), (0, LANE - HEAD_DIM), (0, 0))).reshape(NSA_HEADS * LANE, D_MODEL)
    bq = jnp.pad(seg(b, o_q, NSA_WIDTH).reshape(NSA_HEADS, HEAD_DIM), ((0, 0), (0, LANE - HEAD_DIM)))
    bq = bq.at[:, HEAD_DIM].set(NEG_INF / Q_SCALE).reshape(NSA_HEADS * LANE)

    def k_groups(o):
        return [_pad_cols(seg(w, o + g * HEAD_DIM, HEAD_DIM), LANE) for g in range(G)], \
               [jnp.pad(seg(b, o + g * HEAD_DIM, HEAD_DIM), (0, LANE - HEAD_DIM)) for g in range(G)]

    wks, bks = k_groups(o_ks)
    wkw, bkw = k_groups(o_kw)
    w_r = jnp.concatenate([seg(w, o_kc, KV_WIDTH), seg(w, o_vc, KV_WIDTH), *wks, *wkw,
                           _pad_cols(seg(w, o_ng, n_ng), LANE)], axis=1)
    b_r = jnp.concatenate([seg(b, o_kc, KV_WIDTH), seg(b, o_vc, KV_WIDTH), *bks, *bkw,
                           jnp.pad(seg(b, o_ng, n_ng), (0, LANE - n_ng))])

    def v_rows(o):
        wt = jnp.pad(seg(w, o, HEAD_DIM).T, ((0, V_ROWS - HEAD_DIM), (0, 0)))
        bt = jnp.pad(seg(b, o, HEAD_DIM), (0, V_ROWS - HEAD_DIM)).at[HEAD_DIM].set(1.0)
        return wt, bt

    vparts = [v_rows(o + g * HEAD_DIM) for o in (o_vs, o_vw) for g in range(G)]
    w_t = jnp.concatenate([wq] + [wt for wt, _ in vparts], axis=0)
    b_t = jnp.concatenate([bq] + [bt for _, bt in vparts])
    row_scale = jnp.where(jnp.arange(b_t.shape[0]) < NSA_HEADS * LANE, Q_SCALE, 1.0)
    b_t = jnp.stack([b_t, row_scale], axis=1)

    causal = jnp.tril(jnp.ones((GM_CHUNK, GM_CHUNK), F32))
    gw = GM_WIDTH // GM_GROUPS
    bs = gm_bs[l]
    bsb = jnp.concatenate([jnp.broadcast_to(bs[0::2, :, None], (GM_GROUPS // 2, GM_CHUNK, gw)),
                           jnp.broadcast_to(bs[1::2, :, None], (GM_GROUPS // 2, GM_CHUNK, gw))], axis=2)

    w1 = cmp_w1[l].reshape(2, CMP_BLOCK, HEAD_DIM, CMP_HIDDEN)
    w1 = jnp.stack([jnp.pad(w1, ((0, 0), (0, 0), (g * HEAD_DIM, (G - 1 - g) * HEAD_DIM), (0, 0)))
                    for g in range(G)], axis=1)
    w2 = cmp_w2[l]
    cw = jnp.concatenate([ffn_conv_w[l], ffn_conv_b[l][None], jnp.zeros((8 - CONV_WIDTH - 1, 2 * D_FF), F32)])
    return {
        "w_uv": seg(w, 0, 2 * GM_WIDTH).astype(BF16), "b_uv": seg(b, 0, 2 * GM_WIDTH)[None],
        "gm_ln_g": gm_ln_g[l][None], "gm_ln_b": gm_ln_b[l][None],
        "gm_wc": (gm_ws[l] * causal).astype(BF16), "gm_bsb": bsb,
        "w_r": w_r.astype(BF16), "b_r": b_r[None], "w_t": w_t.astype(BF16), "b_t": b_t,
        "cmp_pos": jnp.tile(cmp_pos[l], (1, 1, G)), "cmp_w1": w1.astype(BF16), "cmp_b1": cmp_b1[l][:, None],
        "cmp_w2k": _pad_cols(w2[0], LANE).astype(BF16),
        "cmp_w2vt": jnp.pad(w2[1].T, ((0, V_ROWS - HEAD_DIM), (0, 0))).astype(BF16),
        "w_mg": seg(w, o_mg, 2 * D_MODEL).astype(BF16), "b_mg": seg(b, o_mg, 2 * D_MODEL)[None],
        "w_proj_a": w_proj_a[l].astype(BF16), "w_proj_b": w_proj_b[l].astype(BF16),
        "w_out": w_out[l].astype(BF16), "ln1_g": ln1_g[l][None], "ln1_b": ln1_b[l][None],
        "ffn_w_up": ffn_w_up[l].astype(BF16), "ffn_cw": cw,
        "ffn_w_down": ffn_w_down[l].reshape(N_FF, FF_CHUNK, D_MODEL).astype(BF16),
        "ln2_g": ln2_g[l][None], "ln2_b": ln2_b[l][None],
    }


def kernel(x, w_in, b_in, gm_ln_g, gm_ln_b, gm_ws, gm_bs, cmp_pos, cmp_w1, cmp_b1, cmp_w2, rel_bias,
           w_proj_a, w_proj_b, w_out, ln1_g, ln1_b, ffn_w_up, ffn_conv_w, ffn_conv_b, ffn_w_down,
           ln2_g, ln2_b):
    B, S, _ = x.shape
    assert S % FAR_KEYS == 0 and N_SEL <= S // SEL_BLOCK <= LANE
    assert S // CMP_STRIDE - 1 + CMP_PAD < CMP_ROWS
    bt0, bt1, bct = _bias_tiles(rel_bias)
    ovt = _overlap_t(S)
    h = x
    for l in range(DEPTH):
        p = _layer_params(l, w_in, b_in, gm_ln_g, gm_ln_b, gm_ws, gm_bs, cmp_pos, cmp_w1, cmp_b1, cmp_w2,
                          w_proj_a, w_proj_b, w_out, ln1_g, ln1_b, ffn_w_up, ffn_conv_w, ffn_conv_b,
                          ffn_w_down, ln2_g, ln2_b)
        a, q, kvc, ks, kw, vst, vst2, vwt, ngt = _proj_call(h, p)
        kc, vct = _compress_call(kvc, p)
        o = _nsa_call(q, ngt, kc, vct, ovt, bct, ks, vst, vst2, kw, vwt, bt0, bt1)
        h = _merge_call(h, a, o, p)
        h = _ffn_call(h, p)
    return h
```

```python
import math

import jax
import jax.numpy as jnp
import numpy as np
from jax import lax
from jax.experimental import pallas as pl
from jax.experimental.pallas import tpu as pltpu

D_MODEL = 1024
DEPTH = 2
GM_WIDTH = D_MODEL // 2
GM_GROUPS = 8
GM_CHUNK = 128
NSA_HEADS = 8
NSA_KV_GROUPS = 2
NSA_HPG = NSA_HEADS // NSA_KV_GROUPS
HEAD_DIM = 64
NSA_WIDTH = NSA_HEADS * HEAD_DIM
KV_WIDTH = NSA_KV_GROUPS * HEAD_DIM
N_GATES = 3
CMP_BLOCK = 32
CMP_STRIDE = 16
CMP_HIDDEN = 256
SEL_BLOCK = 64
N_SEL = 16
WINDOW = 512
Q_BLOCK = 128
NUM_BUCKETS = 32
MAX_DISTANCE = 128
D_FF = 2816
CONV_WIDTH = 3
ALPHA = (2.0 * DEPTH) ** 0.25
LN_EPS = 1e-5
FORCED_SCORE = 1e6
NEG_INF = -1e30
M_INIT = -1e29
LOG2E = math.log2(math.e)
Q_SCALE = HEAD_DIM ** -0.5 * LOG2E

LANE = 128
QCOLS = NSA_HPG * Q_BLOCK
KEY_TILE = 128
FAR_TILES = 4
FAR_KEYS = FAR_TILES * KEY_TILE
FAR_AHEAD = 2
FAR_BUFS = 2
BAND_TILES = WINDOW // KEY_TILE + 1
PAD_TILES = BAND_TILES - 1
CMP_PAD = 8
CMP_ROWS = 640
CMP_BAND = 16
V_ROWS = 80
FF_CHUNK = 256
N_FF = D_FF // FF_CHUNK
FF_AHEAD = 2
ROW_TILE = 256
VMEM_LIMIT = 56 * 1024 * 1024

F32 = jnp.float32
BF16 = jnp.bfloat16
NT_DIMS = (((1,), (1,)), ((), ()))


def _dot(a, b):
    return jnp.dot(a, b, preferred_element_type=F32)


def _dot_nt(a, b):
    return lax.dot_general(a, b, NT_DIMS, preferred_element_type=F32)


def _layer_norm(x, g, b):
    mu = jnp.mean(x, axis=-1, keepdims=True)
    xc = x - mu
    var = jnp.mean(xc * xc, axis=-1, keepdims=True)
    return xc * lax.rsqrt(var + LN_EPS) * g + b


def _gelu_tanh(x):
    c = math.sqrt(2.0 / math.pi)
    return x * (0.5 * (1.0 + jnp.tanh(c * (x + 0.044715 * (x * x * x)))))


def _sigmoid(x):
    return 1.0 / (1.0 + jnp.exp(-x))


def _params(n_axes):
    return pltpu.CompilerParams(dimension_semantics=("arbitrary",) * n_axes,
                                vmem_limit_bytes=VMEM_LIMIT)


def _full(shape):
    n = len(shape)
    return pl.BlockSpec(shape, lambda *_: (0,) * n)


def _proj_kernel(h_ref, wuv_ref, buv_ref, lng_ref, lnb_ref, wc_ref, bsb_ref, wr_ref, br_ref, wt_ref, bt_ref,
                 a_ref, q_ref, kvc_ref, ks_ref, kw_ref, vst_ref, vst2_ref, vwt_ref, ngt_ref):
    tm = h_ref.shape[1]
    G = NSA_KV_GROUPS
    hb = h_ref[0].astype(BF16)

    uv = _dot(hb, wuv_ref[...]) + buv_ref[...]
    r = _dot(hb, wr_ref[...]) + br_ref[...]
    tr = ((_dot_nt(wt_ref[...], hb) + bt_ref[:, 0:1]) * bt_ref[:, 1:2]).astype(BF16)

    u = _gelu_tanh(uv[:, :GM_WIDTH])
    v = _layer_norm(_gelu_tanh(uv[:, GM_WIDTH:]), lng_ref[...], lnb_ref[...]).astype(BF16)
    lane = lax.broadcasted_iota(jnp.int32, (GM_CHUNK, LANE), 1)
    group_w = GM_WIDTH // GM_GROUPS
    for ch in range(tm // GM_CHUNK):
        rows = slice(ch * GM_CHUNK, (ch + 1) * GM_CHUNK)
        for pr in range(GM_GROUPS // 2):
            cols = slice(pr * LANE, (pr + 1) * LANE)
            x = v[rows, cols]
            sv = jnp.where(lane < group_w, _dot(wc_ref[2 * pr], x), _dot(wc_ref[2 * pr + 1], x))
            a_ref[0, rows, cols] = (u[rows, cols] * (sv + bsb_ref[pr])).astype(BF16)

    for hh in range(NSA_HEADS):
        q_ref[0, hh] = tr[hh * LANE:(hh + 1) * LANE]
    vt = tr[NSA_HEADS * LANE:]

    kvc_ref[0, 0] = r[:, 0:LANE]
    kvc_ref[0, 1] = r[:, LANE:2 * LANE]
    token = pl.program_id(1) * tm + lax.broadcasted_iota(jnp.int32, (tm, LANE), 0)
    sel_block = lax.shift_right_logical(token, int(math.log2(SEL_BLOCK)))
    onehot = (lax.broadcasted_iota(jnp.int32, (tm, LANE), 1) == sel_block).astype(BF16)
    for g in range(G):
        ks_ref[0, g, :, 0:LANE] = onehot
        ks_ref[0, g, :, LANE:2 * LANE] = r[:, (2 + g) * LANE:(3 + g) * LANE].astype(BF16)
        kw_ref[0, g] = r[:, (4 + g) * LANE:(5 + g) * LANE].astype(BF16)
    ngt_ref[0] = r[:, 6 * LANE:7 * LANE].T

    for g in range(G):
        sel_rows = slice(g * V_ROWS, (g + 1) * V_ROWS)
        win_rows = slice((G + g) * V_ROWS, (G + g + 1) * V_ROWS)
        vst2_ref[0, g, 0] = vt[sel_rows]
        for t in range(tm // KEY_TILE):
            cols = slice(t * KEY_TILE, (t + 1) * KEY_TILE)
            vst_ref[0, g, t] = vt[sel_rows, cols]
            vwt_ref[0, g, t] = vt[win_rows, cols]


def _proj_call(h, p):
    B, S, _ = h.shape
    tm = 2 * KEY_TILE
    G = NSA_KV_GROUPS
    nt = S // KEY_TILE
    row = lambda w: pl.BlockSpec((1, tm, w), lambda b, i: (b, i, 0))
    lead = lambda n, w: pl.BlockSpec((1, n, tm, w), lambda b, i: (b, 0, i, 0))
    vtile = lambda n, w: pl.BlockSpec((1, G, n, V_ROWS, w), lambda b, i: (b, 0, i, 0, 0))
    out_shape = (
        jax.ShapeDtypeStruct((B, S, GM_WIDTH), BF16),
        jax.ShapeDtypeStruct((B, NSA_HEADS, LANE, S), BF16),
        jax.ShapeDtypeStruct((B, 2, S, LANE), F32),
        jax.ShapeDtypeStruct((B, G, S, 2 * LANE), BF16),
        jax.ShapeDtypeStruct((B, G, S, LANE), BF16),
        jax.ShapeDtypeStruct((B, G, nt, V_ROWS, KEY_TILE), BF16),
        jax.ShapeDtypeStruct((B, G, nt // 2, V_ROWS, 2 * KEY_TILE), BF16),
        jax.ShapeDtypeStruct((B, G, nt, V_ROWS, KEY_TILE), BF16),
        jax.ShapeDtypeStruct((B, LANE, S), F32),
    )
    out_specs = (row(GM_WIDTH), pl.BlockSpec((1, NSA_HEADS, LANE, tm), lambda b, i: (b, 0, 0, i)),
                 lead(2, LANE), lead(G, 2 * LANE), lead(G, LANE),
                 vtile(tm // KEY_TILE, KEY_TILE), vtile(1, 2 * KEY_TILE), vtile(tm // KEY_TILE, KEY_TILE),
                 pl.BlockSpec((1, LANE, tm), lambda b, i: (b, 0, i)))
    in_arrays = (h, p["w_uv"], p["b_uv"], p["gm_ln_g"], p["gm_ln_b"], p["gm_wc"], p["gm_bsb"],
                 p["w_r"], p["b_r"], p["w_t"], p["b_t"])
    in_specs = [row(D_MODEL)] + [_full(a.shape) for a in in_arrays[1:]]
    return pl.pallas_call(
        _proj_kernel, grid=(B, S // tm), in_specs=in_specs, out_specs=out_specs,
        out_shape=out_shape, compiler_params=_params(2), name="proj_gmlp")(*in_arrays)


def _compress_kernel(x_ref, pos_ref, w1_ref, b1_ref, w2k_ref, w2vt_ref, kc_ref, vct_ref):
    kv = pl.program_id(1)
    n = x_ref.shape[2] // CMP_STRIDE
    half = CMP_BLOCK // 2
    pad_row = (lax.broadcasted_iota(jnp.int32, (1, LANE), 1) == HEAD_DIM).astype(F32)
    for g in range(NSA_KV_GROUPS):
        first = jnp.zeros((n, CMP_HIDDEN), F32)
        second = jnp.zeros((n, CMP_HIDDEN), F32)
        for r in range(half):
            xr = x_ref[0, 0, pl.ds(r, n, stride=CMP_STRIDE), :]
            first = first + _dot((xr + pos_ref[0, r:r + 1, :]).astype(BF16), w1_ref[0, g, r])
            second = second + _dot((xr + pos_ref[0, half + r:half + r + 1, :]).astype(BF16),
                                   w1_ref[0, g, half + r])
        hid = first + pltpu.roll(second, n - 1, 0) + b1_ref[0]
        act = (hid * _sigmoid(hid)).astype(BF16)

        @pl.when(kv == 0)
        def _():
            res = _dot(act, w2k_ref[...])
            row = lax.broadcasted_iota(jnp.int32, res.shape, 0)
            res = jnp.where(row == n - 1, pad_row, res)
            kc_ref[0, g] = jnp.concatenate(
                [jnp.broadcast_to(pad_row, (CMP_PAD, LANE)), res,
                 jnp.broadcast_to(pad_row, (CMP_ROWS - CMP_PAD - n, LANE))], axis=0).astype(BF16)

        @pl.when(kv == 1)
        def _():
            vt = _dot_nt(w2vt_ref[...], act)
            col = lax.broadcasted_iota(jnp.int32, vt.shape, 1)
            vt = jnp.where(col == n - 1, 0.0, vt)
            vt = jnp.concatenate([vt, jnp.zeros((V_ROWS, CMP_ROWS - n), F32)], axis=1)
            vct_ref[0, g] = pltpu.roll(vt, CMP_PAD, 1).astype(BF16)


def _compress_call(kvc, p):
    B, _, S, _ = kvc.shape
    G = NSA_KV_GROUPS
    per_kv = lambda a: pl.BlockSpec((1,) + a.shape[1:], lambda b, k: (k,) + (0,) * (a.ndim - 1))
    return pl.pallas_call(
        _compress_kernel, grid=(B, 2),
        in_specs=[pl.BlockSpec((1, 1, S, LANE), lambda b, k: (b, k, 0, 0)),
                  per_kv(p["cmp_pos"]), per_kv(p["cmp_w1"]), per_kv(p["cmp_b1"]),
                  _full(p["cmp_w2k"].shape), _full(p["cmp_w2vt"].shape)],
        out_specs=(pl.BlockSpec((1, G, CMP_ROWS, LANE), lambda b, k: (b, 0, 0, 0)),
                   pl.BlockSpec((1, G, V_ROWS, CMP_ROWS), lambda b, k: (b, 0, 0, 0))),
        out_shape=(jax.ShapeDtypeStruct((B, G, CMP_ROWS, LANE), BF16),
                   jax.ShapeDtypeStruct((B, G, V_ROWS, CMP_ROWS), BF16)),
        compiler_params=_params(2), name="kv_compress",
    )(kvc, p["cmp_pos"], p["cmp_w1"], p["cmp_b1"], p["cmp_w2k"], p["cmp_w2vt"])


def _nsa_kernel(q_ref, ngt_ref, kc_ref, vct_ref, ovt_ref, bct_ref, ks_ref, vst_ref, vst2_ref, kw_ref, vwt_ref,
                 bt0_ref, bt1_ref, o_ref, sc_ref, rhst_ref, m_ref, acc_ref, negf_ref, *ring_refs):
    G = NSA_KV_GROUPS
    groups = range(G)
    sbuf_refs = [ring_refs[g * FAR_BUFS:(g + 1) * FAR_BUFS] for g in groups]
    mbuf_refs = [ring_refs[(G + g) * FAR_BUFS:(G + g + 1) * FAR_BUFS] for g in groups]
    i = pl.program_id(1)
    n_blk = LANE
    blk = lax.broadcasted_iota(jnp.int32, (n_blk, Q_BLOCK), 0)
    qi = lax.broadcasted_iota(jnp.int32, (n_blk, Q_BLOCK), 1)
    eye = (blk == qi).astype(BF16)
    qt = [jnp.concatenate([q_ref[0, g * NSA_HPG + h] for h in range(NSA_HPG)], axis=1) for g in groups]
    bt0 = [bt0_ref[g] for g in groups]
    bt1 = [bt1_ref[g] for g in groups]

    def band_logits(k_ref, g, rhs, extra, flag_col):
        width = k_ref.shape[3]
        before_start = (lax.broadcasted_iota(jnp.int32, (KEY_TILE, width), 1) == flag_col).astype(BF16)
        tiles = []
        for t in range(BAND_TILES):
            tau = i - PAD_TILES + t
            row0 = pl.multiple_of(jnp.maximum(tau, 0) * KEY_TILE, KEY_TILE)
            k = k_ref[0, g, pl.ds(row0, KEY_TILE), :]
            if t < PAD_TILES:
                k = jnp.where(jnp.broadcast_to(tau, k.shape) >= 0, k, before_start)
            s = _dot(k, rhs)
            tiles.append(s if extra[t] is None else s + extra[t])
        return tiles

    def band_softmax(tiles, vt_ref, g, m_old, acc_old):
        m_new = m_old
        for s in tiles:
            m_new = jnp.maximum(m_new, jnp.max(s, axis=0, keepdims=True))
        acc = jnp.exp2(m_old - m_new) * acc_old
        for t, s in enumerate(tiles):
            vt = vt_ref[0, g, jnp.maximum(i - PAD_TILES + t, 0)]
            acc = acc + _dot(vt, jnp.exp2(s - m_new).astype(BF16))
        return acc[0:HEAD_DIM] / acc[HEAD_DIM:HEAD_DIM + 1]

    band = pl.multiple_of(8 * i, 8)
    visible = 8 * i + CMP_BAND

    def compressed_stage(rows):
        def softmax(g):
            s = sc_ref[g, 0:rows]
            key = lax.broadcasted_iota(jnp.int32, s.shape, 0)
            s = jnp.where(key >= visible, NEG_INF, s)
            m = jnp.maximum(jnp.max(s, axis=0, keepdims=True), M_INIT)
            p = jnp.exp2(s - m)
            l = jnp.sum(p, axis=0, keepdims=True)
            r = jnp.where(l > 0.0, 1.0 / l, 0.0)
            oc = _dot(vct_ref[0, g, :, 0:rows], p.astype(BF16))[0:HEAD_DIM] * r
            pn = p * r
            psum = pn[:, 0:Q_BLOCK]
            for h in range(1, NSA_HPG):
                psum = psum + pn[:, h * Q_BLOCK:(h + 1) * Q_BLOCK]
            p_hi = psum.astype(BF16)
            p_lo = (psum - p_hi.astype(F32)).astype(BF16)
            ovt = ovt_ref[:, 0:rows]
            return oc, _dot(ovt, p_hi) + _dot(ovt, p_lo)

        def run():
            for g in groups:
                sc_ref[g, 0:rows] = _dot(kc_ref[0, g, 0:rows], qt[g])
                sc_ref[g, pl.ds(band, CMP_BAND), :] += bct_ref[g]
            for u in range(FAR_AHEAD):
                for g in groups:
                    bufs[g][u][0][...], bufs[g][u][1][...] = far_logits(g, u)
            return tuple(x for g in groups for x in softmax(g))
        return run

    n_far = jnp.maximum(i - 1, 0) // FAR_TILES
    last_far = jnp.maximum(n_far - 1, 0)
    blocks_per_step = FAR_KEYS // SEL_BLOCK
    bufs = [tuple(zip(sbuf_refs[g], mbuf_refs[g])) for g in groups]

    def far_logits(g, c):
        r0 = pl.multiple_of(FAR_KEYS * jnp.minimum(c, last_far), FAR_KEYS)
        s = _dot(ks_ref[0, g, pl.ds(r0, FAR_KEYS), LANE:2 * LANE], qt[g])
        return s, jnp.max(s.reshape(blocks_per_step, SEL_BLOCK, QCOLS), axis=1)

    row_options = tuple(range(LANE, CMP_ROWS + 1, LANE))
    flat = lax.switch((visible + LANE - 1) // LANE - 1, [compressed_stage(rows) for rows in row_options])
    oc, imp = flat[0::2], flat[1::2]

    cur = (Q_BLOCK // SEL_BLOCK) * i + (qi >= SEL_BLOCK).astype(jnp.int32)
    forced = (blk == 0) | (blk == cur) | (blk == cur - 1)
    taken = -3e38
    blk_f = blk.astype(F32)
    start = tuple((jnp.where(forced, taken, jnp.where(blk > cur, NEG_INF, imp[g])), forced.astype(F32))
                  for g in groups)

    def select_stage(rows):
        def pick(_, carry):
            out = []
            for val, sel in carry:
                mx = jnp.max(val, axis=0, keepdims=True)
                first = jnp.min(jnp.where(val == mx, blk_f[0:rows], 1e9), axis=0, keepdims=True)
                hit = blk_f[0:rows] == first
                out.append((jnp.where(hit, taken, val), jnp.where(hit, 1.0, sel)))
            return tuple(out)

        def run():
            head = lax.fori_loop(0, N_SEL - 3, pick, tuple((v[0:rows], s[0:rows]) for v, s in start))
            return tuple(jnp.concatenate([sel, jnp.zeros((n_blk - rows, Q_BLOCK), F32)], axis=0)
                         if rows < n_blk else sel for _, sel in head)
        return run

    row_step = n_blk // 4
    row_options = tuple(range(row_step, n_blk + 1, row_step))
    n_cand = (Q_BLOCK // SEL_BLOCK) * (i + 1)
    picked = lax.switch((n_cand + row_step - 1) // row_step - 1, [select_stage(rows) for rows in row_options])

    per_head = lambda x: jnp.concatenate([x] * NSA_HPG, axis=1)
    for g in groups:
        neg_far = jnp.where((picked[g] > 0.5) & (blk <= cur), 0.0, NEG_INF)
        neg_tail = jnp.where(blk >= n_far * blocks_per_step, neg_far, NEG_INF)
        negf_ref[g] = per_head(neg_far)
        rhst_ref[g, 0:LANE] = per_head(neg_tail.astype(BF16))
        rhst_ref[g, LANE:2 * LANE] = qt[g]
        m_ref[g] = jnp.full(m_ref.shape[1:], M_INIT, F32)
        acc_ref[g] = jnp.zeros(acc_ref.shape[1:], F32)

    def far_softmax(g, c, s_ref, bmax_ref):
        step = jnp.minimum(c, last_far)
        mask = negf_ref[g, pl.ds(pl.multiple_of(blocks_per_step * step, blocks_per_step), blocks_per_step), :]
        mask = jnp.where(c >= n_far, NEG_INF, mask)
        m_old = m_ref[g]
        m_new = jnp.maximum(m_old, jnp.max(bmax_ref[...] + mask, axis=0, keepdims=True))
        shift = mask - m_new
        p = jnp.concatenate([jnp.exp2(s_ref[b * SEL_BLOCK:(b + 1) * SEL_BLOCK] + shift[b:b + 1])
                             for b in range(blocks_per_step)], axis=0).astype(BF16)
        pv = jnp.zeros(acc_ref.shape[1:], F32)
        for t in range(FAR_TILES // 2):
            pv = pv + _dot(vst2_ref[0, g, (FAR_TILES // 2) * step + t], p[2 * t * KEY_TILE:2 * (t + 1) * KEY_TILE])
        acc_ref[g] = jnp.exp2(m_old - m_new) * acc_ref[g] + pv
        m_ref[g] = m_new

    kk = lax.broadcasted_iota(jnp.int32, (KEY_TILE, QCOLS), 0)
    qq = lax.broadcasted_iota(jnp.int32, (KEY_TILE, QCOLS), 1) & (Q_BLOCK - 1)
    edge = jnp.where(kk > qq, 0.0, NEG_INF)
    win_tiles = [band_logits(kw_ref, g, qt[g], [edge] + [None] * (BAND_TILES - 3) + [bt1[g], bt0[g]], HEAD_DIM)
                 for g in groups]
    ow = [band_softmax(win_tiles[g], vwt_ref, g, jnp.full(m_ref.shape[1:], M_INIT, F32),
                       jnp.zeros(acc_ref.shape[1:], F32)) for g in groups]

    n_trips = jnp.maximum(n_far - FAR_AHEAD + FAR_BUFS - 1, 0) // FAR_BUFS

    def far_ring(j, carry):
        for u in range(FAR_BUFS):
            c = FAR_BUFS * j + u
            ahead = [far_logits(g, c + FAR_AHEAD) for g in groups]
            for g in groups:
                far_softmax(g, c, *bufs[g][u])
            for g in groups:
                ahead_s, ahead_mx = bufs[g][(u + FAR_AHEAD) % FAR_BUFS]
                ahead_s[...], ahead_mx[...] = ahead[g]
        return carry

    lax.fori_loop(0, n_trips, far_ring, 0)

    tail_tiles = [band_logits(ks_ref, g, rhst_ref[g], [None] * (BAND_TILES - 2) + [bt1[g], bt0[g]],
                              LANE + HEAD_DIM) for g in groups]
    for u in range(FAR_AHEAD):
        for g in groups:
            far_softmax(g, FAR_BUFS * n_trips + u, *bufs[g][u])
    os_ = [band_softmax(tail_tiles[g], vst_ref, g, m_ref[g], acc_ref[g]) for g in groups]

    for g in groups:
        def gate(branch):
            rows = [ngt_ref[0, NSA_HPG * N_GATES * g + N_GATES * h + branch:
                            NSA_HPG * N_GATES * g + N_GATES * h + branch + 1, :] for h in range(NSA_HPG)]
            return _sigmoid(jnp.concatenate(rows, axis=1))

        o = (gate(0) * oc[g] + gate(1) * os_[g] + gate(2) * ow[g]).astype(BF16)
        heads = jnp.concatenate([o[:, h * Q_BLOCK:(h + 1) * Q_BLOCK] for h in range(NSA_HPG)], axis=0)
        o_ref[0, :, g * NSA_HPG * HEAD_DIM:(g + 1) * NSA_HPG * HEAD_DIM] = _dot_nt(eye, heads).astype(BF16)


def _nsa_call(q, ngt, kc, vct, ovt, bct, ks, vst, vst2, kw, vwt, bt0, bt1):
    B, _, _, S = q.shape
    G = NSA_KV_GROUPS
    nq = S // Q_BLOCK
    per_b = lambda a: pl.BlockSpec((1,) + a.shape[1:], lambda b, i: (b,) + (0,) * (a.ndim - 1),
                                   pipeline_mode=pl.Buffered(1))
    in_specs = [
        pl.BlockSpec((1, NSA_HEADS, LANE, Q_BLOCK), lambda b, i: (b, 0, 0, i)),
        pl.BlockSpec((1, LANE, Q_BLOCK), lambda b, i: (b, 0, i)),
        per_b(kc), per_b(vct), _full(ovt.shape), _full(bct.shape),
        per_b(ks), per_b(vst), per_b(vst2), per_b(kw), per_b(vwt), _full(bt0.shape), _full(bt1.shape),
    ]
    scratch = [
        pltpu.VMEM((G, CMP_ROWS, QCOLS), F32),
        pltpu.VMEM((G, 2 * LANE, QCOLS), BF16),
        pltpu.VMEM((G, 1, QCOLS), F32),
        pltpu.VMEM((G, V_ROWS, QCOLS), F32),
        pltpu.VMEM((G, LANE, QCOLS), F32),
    ]
    scratch += [pltpu.VMEM((FAR_KEYS, QCOLS), F32)] * (G * FAR_BUFS)
    scratch += [pltpu.VMEM((FAR_KEYS // SEL_BLOCK, QCOLS), F32)] * (G * FAR_BUFS)
    return pl.pallas_call(
        _nsa_kernel, grid=(B, nq), in_specs=in_specs,
        out_specs=pl.BlockSpec((1, Q_BLOCK, NSA_WIDTH), lambda b, i: (b, i, 0)),
        out_shape=jax.ShapeDtypeStruct((B, S, NSA_WIDTH), BF16),
        scratch_shapes=scratch, compiler_params=_params(2), name="sparse_attention",
    )(q, ngt, kc, vct, ovt, bct, ks, vst, vst2, kw, vwt, bt0, bt1)


def _merge_kernel(h_ref, a_ref, o_ref, wmg_ref, bmg_ref, wa_ref, wb_ref, wout_ref, lng_ref, lnb_ref, out_ref):
    tm = h_ref.shape[1]
    halves = [slice(k * tm // 2, (k + 1) * tm // 2) for k in range(2)]
    pre = []
    for rows in halves:
        mg = _dot(h_ref[0, rows].astype(BF16), wmg_ref[...]) + bmg_ref[...]
        pre.append((mg, _dot(a_ref[0, rows], wa_ref[...]), _dot(o_ref[0, rows], wb_ref[...])))
    ms = []
    for mg, pa, pb in pre:
        y = _sigmoid(mg[:, :D_MODEL]) * pa + _sigmoid(mg[:, D_MODEL:]) * pb
        ms.append(_dot(y.astype(BF16), wout_ref[...]))
    for rows, m in zip(halves, ms):
        out_ref[0, rows] = _layer_norm(ALPHA * h_ref[0, rows] + m, lng_ref[...], lnb_ref[...])


def _merge_call(h, a, o, p):
    B, S, _ = h.shape
    tm = 2 * ROW_TILE
    row = lambda w: pl.BlockSpec((1, tm, w), lambda b, i: (b, i, 0))
    ws = (p["w_mg"], p["b_mg"], p["w_proj_a"], p["w_proj_b"], p["w_out"], p["ln1_g"], p["ln1_b"])
    return pl.pallas_call(
        _merge_kernel, grid=(B, S // tm),
        in_specs=[row(D_MODEL), row(GM_WIDTH), row(NSA_WIDTH)] + [_full(a.shape) for a in ws],
        out_specs=row(D_MODEL), out_shape=jax.ShapeDtypeStruct((B, S, D_MODEL), F32),
        compiler_params=_params(2), name="merge_norm")(h, a, o, *ws)


def _ffn_kernel(h_ref, wup_ref, cw_ref, wdown_ref, lng_ref, lnb_ref, out_ref, carry_ref):
    tm = h_ref.shape[1]

    @pl.when(pl.program_id(1) == 0)
    def _():
        carry_ref[...] = jnp.zeros(carry_ref.shape, F32)

    h = h_ref[0]
    hb = h.astype(BF16)
    row8 = lax.broadcasted_iota(jnp.int32, (8, FF_CHUNK), 0)

    def up_proj(c):
        cols = lambda idx: slice(idx * FF_CHUNK, (idx + 1) * FF_CHUNK)
        return _dot(hb, wup_ref[:, cols(c)]), _dot(hb, wup_ref[:, cols(N_FF + c)])

    def conv(up, idx):
        prev = carry_ref[idx]
        carry_ref[idx] = up[tm - 8:tm]
        cw = cw_ref[:, idx * FF_CHUNK:(idx + 1) * FF_CHUNK]
        out = cw[3:4] + cw[2:3] * up
        for shift in (1, 2):
            rolled = pltpu.roll(up, shift, 0)
            head = jnp.where(row8 < shift, pltpu.roll(prev, shift, 0), rolled[0:8])
            out = out + cw[2 - shift:3 - shift] * jnp.concatenate([head, rolled[8:]], axis=0)
        return out

    f = jnp.zeros((tm, D_MODEL), F32)
    ups = [up_proj(c) for c in range(FF_AHEAD)]
    for c in range(N_FF):
        g_up, v_up = ups.pop(0)
        if c + FF_AHEAD < N_FF:
            ups.append(up_proj(c + FF_AHEAD))
        gate = conv(g_up, c)
        act = (gate * _sigmoid(gate) * conv(v_up, N_FF + c)).astype(BF16)
        f = f + _dot(act, wdown_ref[c])
    out_ref[0] = _layer_norm(ALPHA * h + f, lng_ref[...], lnb_ref[...])


def _ffn_call(h, p):
    B, S, _ = h.shape
    tm = ROW_TILE
    row = pl.BlockSpec((1, tm, D_MODEL), lambda b, i: (b, i, 0))
    ws = (p["ffn_w_up"], p["ffn_cw"], p["ffn_w_down"], p["ln2_g"], p["ln2_b"])
    return pl.pallas_call(
        _ffn_kernel, grid=(B, S // tm),
        in_specs=[row] + [_full(a.shape) for a in ws],
        out_specs=row, out_shape=jax.ShapeDtypeStruct((B, S, D_MODEL), F32),
        scratch_shapes=[pltpu.VMEM((2 * N_FF, 8, FF_CHUNK), F32)],
        compiler_params=_params(2), name="conv_ffn_norm")(h, *ws)


def _t5_bucket(n):
    max_exact = NUM_BUCKETS // 2
    log_ratio = jnp.log(jnp.maximum(n, 1).astype(F32) / max_exact) / math.log(MAX_DISTANCE / max_exact)
    large = jnp.minimum(max_exact + (log_ratio * (NUM_BUCKETS - max_exact)).astype(jnp.int32), NUM_BUCKETS - 1)
    return jnp.where(n < max_exact, n, large)


def _bias_tiles(rel_bias):
    G, H = NSA_KV_GROUPS, NSA_HPG
    tab = rel_bias[_t5_bucket(jnp.arange(2 * KEY_TILE, dtype=jnp.int32))] - rel_bias[NUM_BUCKETS - 1]
    tab = tab * LOG2E

    def tile(offset, n, row_step=1):
        d = np.arange(2 * n)
        dist = offset + np.where(d < n, d, d - 2 * n)
        w = jnp.where((dist >= 0)[:, None], tab[np.clip(dist, 0, 2 * KEY_TILE - 1)], NEG_INF)
        m = jnp.broadcast_to(w[None], (n, 2 * n, w.shape[1])).reshape(2 * n * n, -1)[:n * (2 * n - 1)]
        t = m.reshape(n, 2 * n - 1, -1)[::row_step, :Q_BLOCK]
        rows = t.shape[0]
        return t.reshape(rows, Q_BLOCK, G, H).transpose(2, 0, 3, 1).reshape(G, rows, H * Q_BLOCK)

    bt0 = tile(0, KEY_TILE)
    bt1 = tile(KEY_TILE, KEY_TILE)
    bct = tile(CMP_STRIDE * CMP_PAD - CMP_BLOCK + 1, CMP_STRIDE * CMP_BAND, CMP_STRIDE)
    return bt0, bt1, bct


def _overlap_t(seq):
    n_cmp = seq // CMP_STRIDE - 1
    n_blk = seq // SEL_BLOCK
    cs = np.arange(n_cmp)[:, None] * CMP_STRIDE
    ss = np.arange(n_blk)[None, :] * SEL_BLOCK
    ov = np.maximum(np.minimum(cs + CMP_BLOCK, ss + SEL_BLOCK) - np.maximum(cs, ss), 0) / CMP_BLOCK
    out = np.zeros((LANE, CMP_ROWS), np.float32)
    out[:n_blk, CMP_PAD:CMP_PAD + n_cmp] = ov.T
    return jnp.asarray(out, BF16)


def _pad_cols(w, width):
    return jnp.pad(w, ((0, 0), (0, width - w.shape[1])))


def _layer_params(l, w_in, b_in, gm_ln_g, gm_ln_b, gm_ws, gm_bs, cmp_pos, cmp_w1, cmp_b1, cmp_w2,
                  w_proj_a, w_proj_b, w_out, ln1_g, ln1_b, ffn_w_up, ffn_conv_w, ffn_conv_b, ffn_w_down,
                  ln2_g, ln2_b):
    G = NSA_KV_GROUPS
    w, b = w_in[l], b_in[l]
    o_q = 2 * GM_WIDTH
    o_kc = o_q + NSA_WIDTH
    o_vc, o_ks, o_vs, o_kw, o_vw = (o_kc + KV_WIDTH * k for k in range(1, 6))
    o_ng = o_vw + KV_WIDTH
    n_ng = NSA_HEADS * N_GATES
    o_mg = o_ng + n_ng
    seg = lambda a, o, n: a[..., o:o + n]

    wq = jnp.pad(seg(w, o_q, NSA_WIDTH).T.reshape(NSA_HEADS, HEAD_DIM, D_MODEL),
                 ((0, 0), (0, LANE - HEAD_DIM), (0, 0))).reshape(NSA_HEADS * LANE, D_MODEL)
    bq = jnp.pad(seg(b, o_q, NSA_WIDTH).reshape(NSA_HEADS, HEAD_DIM), ((0, 0), (0, LANE - HEAD_DIM)))
    bq = bq.at[:, HEAD_DIM].set(NEG_INF / Q_SCALE).reshape(NSA_HEADS * LANE)

    def k_groups(o):
        return [_pad_cols(seg(w, o + g * HEAD_DIM, HEAD_DIM), LANE) for g in range(G)], \
               [jnp.pad(seg(b, o + g * HEAD_DIM, HEAD_DIM), (0, LANE - HEAD_DIM)) for g in range(G)]

    wks, bks = k_groups(o_ks)
    wkw, bkw = k_groups(o_kw)
    w_r = jnp.concatenate([seg(w, o_kc, KV_WIDTH), seg(w, o_vc, KV_WIDTH), *wks, *wkw,
                           _pad_cols(seg(w, o_ng, n_ng), LANE)], axis=1)
    b_r = jnp.concatenate([seg(b, o_kc, KV_WIDTH), seg(b, o_vc, KV_WIDTH), *bks, *bkw,
                           jnp.pad(seg(b, o_ng, n_ng), (0, LANE - n_ng))])

    def v_rows(o):
        wt = jnp.pad(seg(w, o, HEAD_DIM).T, ((0, V_ROWS - HEAD_DIM), (0, 0)))
        bt = jnp.pad(seg(b, o, HEAD_DIM), (0, V_ROWS - HEAD_DIM)).at[HEAD_DIM].set(1.0)
        return wt, bt

    vparts = [v_rows(o + g * HEAD_DIM) for o in (o_vs, o_vw) for g in range(G)]
    w_t = jnp.concatenate([wq] + [wt for wt, _ in vparts], axis=0)
    b_t = jnp.concatenate([bq] + [bt for _, bt in vparts])
    row_scale = jnp.where(jnp.arange(b_t.shape[0]) < NSA_HEADS * LANE, Q_SCALE, 1.0)
    b_t = jnp.stack([b_t, row_scale], axis=1)

    causal = jnp.tril(jnp.ones((GM_CHUNK, GM_CHUNK), F32))
    gw = GM_WIDTH // GM_GROUPS
    bs = gm_bs[l]
    bsb = jnp.concatenate([jnp.broadcast_to(bs[0::2, :, None], (GM_GROUPS // 2, GM_CHUNK, gw)),
                           jnp.broadcast_to(bs[1::2, :, None], (GM_GROUPS // 2, GM_CHUNK, gw))], axis=2)

    w1 = cmp_w1[l].reshape(2, CMP_BLOCK, HEAD_DIM, CMP_HIDDEN)
    w1 = jnp.stack([jnp.pad(w1, ((0, 0), (0, 0), (g * HEAD_DIM, (G - 1 - g) * HEAD_DIM), (0, 0)))
                    for g in range(G)], axis=1)
    w2 = cmp_w2[l]
    cw = jnp.concatenate([ffn_conv_w[l], ffn_conv_b[l][None], jnp.zeros((8 - CONV_WIDTH - 1, 2 * D_FF), F32)])
    return {
        "w_uv": seg(w, 0, 2 * GM_WIDTH).astype(BF16), "b_uv": seg(b, 0, 2 * GM_WIDTH)[None],
        "gm_ln_g": gm_ln_g[l][None], "gm_ln_b": gm_ln_b[l][None],
        "gm_wc": (gm_ws[l] * causal).astype(BF16), "gm_bsb": bsb,
        "w_r": w_r.astype(BF16), "b_r": b_r[None], "w_t": w_t.astype(BF16), "b_t": b_t,
        "cmp_pos": jnp.tile(cmp_pos[l], (1, 1, G)), "cmp_w1": w1.astype(BF16), "cmp_b1": cmp_b1[l][:, None],
        "cmp_w2k": _pad_cols(w2[0], LANE).astype(BF16),
        "cmp_w2vt": jnp.pad(w2[1].T, ((0, V_ROWS - HEAD_DIM), (0, 0))).astype(BF16),
        "w_mg": seg(w, o_mg, 2 * D_MODEL).astype(BF16), "b_mg": seg(b, o_mg, 2 * D_MODEL)[None],
        "w_proj_a": w_proj_a[l].astype(BF16), "w_proj_b": w_proj_b[l].astype(BF16),
        "w_out": w_out[l].astype(BF16), "ln1_g": ln1_g[l][None], "ln1_b": ln1_b[l][None],
        "ffn_w_up": ffn_w_up[l].astype(BF16), "ffn_cw": cw,
        "ffn_w_down": ffn_w_down[l].reshape(N_FF, FF_CHUNK, D_MODEL).astype(BF16),
        "ln2_g": ln2_g[l][None], "ln2_b": ln2_b[l][None],
    }


def kernel(x, w_in, b_in, gm_ln_g, gm_ln_b, gm_ws, gm_bs, cmp_pos, cmp_w1, cmp_b1, cmp_w2, rel_bias,
           w_proj_a, w_proj_b, w_out, ln1_g, ln1_b, ffn_w_up, ffn_conv_w, ffn_conv_b, ffn_w_down,
           ln2_g, ln2_b):
    B, S, _ = x.shape
    assert S % FAR_KEYS == 0 and N_SEL <= S // SEL_BLOCK <= LANE
    assert S // CMP_STRIDE - 1 + CMP_PAD < CMP_ROWS
    bt0, bt1, bct = _bias_tiles(rel_bias)
    ovt = _overlap_t(S)
    h = x
    for l in range(DEPTH):
        p = _layer_params(l, w_in, b_in, gm_ln_g, gm_ln_b, gm_ws, gm_bs, cmp_pos, cmp_w1, cmp_b1, cmp_w2,
                          w_proj_a, w_proj_b, w_out, ln1_g, ln1_b, ffn_w_up, ffn_conv_w, ffn_conv_b,
                          ffn_w_down, ln2_g, ln2_b)
        a, q, kvc, ks, kw, vst, vst2, vwt, ngt = _proj_call(h, p)
        kc, vct = _compress_call(kvc, p)
        o = _nsa_call(q, ngt, kc, vct, ovt, bct, ks, vst, vst2, kw, vwt, bt0, bt1)
        h = _merge_call(h, a, o, p)
        h = _ffn_call(h, p)
    return h
```

```python
import math

import jax
import jax.numpy as jnp
import numpy as np
from jax import lax
from jax.experimental import pallas as pl
from jax.experimental.pallas import tpu as pltpu

D_MODEL = 1024
DEPTH = 2
GM_WIDTH = D_MODEL // 2
GM_GROUPS = 8
GM_CHUNK = 128
NSA_HEADS = 8
NSA_KV_GROUPS = 2
NSA_HPG = NSA_HEADS // NSA_KV_GROUPS
HEAD_DIM = 64
NSA_WIDTH = NSA_HEADS * HEAD_DIM
KV_WIDTH = NSA_KV_GROUPS * HEAD_DIM
N_GATES = 3
CMP_BLOCK = 32
CMP_STRIDE = 16
CMP_HIDDEN = 256
SEL_BLOCK = 64
N_SEL = 16
WINDOW = 512
Q_BLOCK = 128
NUM_BUCKETS = 32
MAX_DISTANCE = 128
D_FF = 2816
CONV_WIDTH = 3
ALPHA = (2.0 * DEPTH) ** 0.25
LN_EPS = 1e-5
FORCED_SCORE = 1e6
NEG_INF = -1e30
M_INIT = -1e29
LOG2E = math.log2(math.e)
Q_SCALE = HEAD_DIM ** -0.5 * LOG2E

LANE = 128
QCOLS = NSA_HPG * Q_BLOCK
KEY_TILE = 128
FAR_TILES = 4
FAR_KEYS = FAR_TILES * KEY_TILE
FAR_AHEAD = 2
FAR_BUFS = 2
BAND_TILES = WINDOW // KEY_TILE + 1
PAD_TILES = BAND_TILES - 1
CMP_PAD = 8
CMP_ROWS = 640
CMP_BAND = 16
V_ROWS = 80
FF_CHUNK = 256
N_FF = D_FF // FF_CHUNK
FF_AHEAD = 2
ROW_TILE = 256
VMEM_LIMIT = 56 * 1024 * 1024

F32 = jnp.float32
BF16 = jnp.bfloat16
NT_DIMS = (((1,), (1,)), ((), ()))


def _dot(a, b):
    return jnp.dot(a, b, preferred_element_type=F32)


def _dot_nt(a, b):
    return lax.dot_general(a, b, NT_DIMS, preferred_element_type=F32)


def _layer_norm(x, g, b):
    mu = jnp.mean(x, axis=-1, keepdims=True)
    xc = x - mu
    var = jnp.mean(xc * xc, axis=-1, keepdims=True)
    return xc * lax.rsqrt(var + LN_EPS) * g + b


def _gelu_tanh(x):
    c = math.sqrt(2.0 / math.pi)
    return x * (0.5 * (1.0 + jnp.tanh(c * (x + 0.044715 * (x * x * x)))))


def _sigmoid(x):
    return 1.0 / (1.0 + jnp.exp(-x))


def _params(n_axes):
    return pltpu.CompilerParams(dimension_semantics=("arbitrary",) * n_axes,
                                vmem_limit_bytes=VMEM_LIMIT)


def _full(shape):
    n = len(shape)
    return pl.BlockSpec(shape, lambda *_: (0,) * n)


def _proj_kernel(h_ref, wuv_ref, buv_ref, lng_ref, lnb_ref, wc_ref, bsb_ref, wr_ref, br_ref, wt_ref, bt_ref,
                 a_ref, q_ref, kvc_ref, ks_ref, kw_ref, vst_ref, vst2_ref, vwt_ref, ngt_ref):
    tm = h_ref.shape[1]
    G = NSA_KV_GROUPS
    hb = h_ref[0].astype(BF16)

    uv = _dot(hb, wuv_ref[...]) + buv_ref[...]
    r = _dot(hb, wr_ref[...]) + br_ref[...]
    tr = ((_dot_nt(wt_ref[...], hb) + bt_ref[:, 0:1]) * bt_ref[:, 1:2]).astype(BF16)

    u = _gelu_tanh(uv[:, :GM_WIDTH])
    v = _layer_norm(_gelu_tanh(uv[:, GM_WIDTH:]), lng_ref[...], lnb_ref[...]).astype(BF16)
    lane = lax.broadcasted_iota(jnp.int32, (GM_CHUNK, LANE), 1)
    group_w = GM_WIDTH // GM_GROUPS
    for ch in range(tm // GM_CHUNK):
        rows = slice(ch * GM_CHUNK, (ch + 1) * GM_CHUNK)
        for pr in range(GM_GROUPS // 2):
            cols = slice(pr * LANE, (pr + 1) * LANE)
            x = v[rows, cols]
            sv = jnp.where(lane < group_w, _dot(wc_ref[2 * pr], x), _dot(wc_ref[2 * pr + 1], x))
            a_ref[0, rows, cols] = (u[rows, cols] * (sv + bsb_ref[pr])).astype(BF16)

    for hh in range(NSA_HEADS):
        q_ref[0, hh] = tr[hh * LANE:(hh + 1) * LANE]
    vt = tr[NSA_HEADS * LANE:]

    kvc_ref[0, 0] = r[:, 0:LANE]
    kvc_ref[0, 1] = r[:, LANE:2 * LANE]
    token = pl.program_id(1) * tm + lax.broadcasted_iota(jnp.int32, (tm, LANE), 0)
    sel_block = lax.shift_right_logical(token, int(math.log2(SEL_BLOCK)))
    onehot = (lax.broadcasted_iota(jnp.int32, (tm, LANE), 1) == sel_block).astype(BF16)
    for g in range(G):
        ks_ref[0, g, :, 0:LANE] = onehot
        ks_ref[0, g, :, LANE:2 * LANE] = r[:, (2 + g) * LANE:(3 + g) * LANE].astype(BF16)
        kw_ref[0, g] = r[:, (4 + g) * LANE:(5 + g) * LANE].astype(BF16)
    ngt_ref[0] = r[:, 6 * LANE:7 * LANE].T

    for g in range(G):
        sel_rows = slice(g * V_ROWS, (g + 1) * V_ROWS)
        win_rows = slice((G + g) * V_ROWS, (G + g + 1) * V_ROWS)
        vst2_ref[0, g, 0] = vt[sel_rows]
        for t in range(tm // KEY_TILE):
            cols = slice(t * KEY_TILE, (t + 1) * KEY_TILE)
            vst_ref[0, g, t] = vt[sel_rows, cols]
            vwt_ref[0, g, t] = vt[win_rows, cols]


def _proj_call(h, p):
    B, S, _ = h.shape
    tm = 2 * KEY_TILE
    G = NSA_KV_GROUPS
    nt = S // KEY_TILE
    row = lambda w: pl.BlockSpec((1, tm, w), lambda b, i: (b, i, 0))
    lead = lambda n, w: pl.BlockSpec((1, n, tm, w), lambda b, i: (b, 0, i, 0))
    vtile = lambda n, w: pl.BlockSpec((1, G, n, V_ROWS, w), lambda b, i: (b, 0, i, 0, 0))
    out_shape = (
        jax.ShapeDtypeStruct((B, S, GM_WIDTH), BF16),
        jax.ShapeDtypeStruct((B, NSA_HEADS, LANE, S), BF16),
        jax.ShapeDtypeStruct((B, 2, S, LANE), F32),
        jax.ShapeDtypeStruct((B, G, S, 2 * LANE), BF16),
        jax.ShapeDtypeStruct((B, G, S, LANE), BF16),
        jax.ShapeDtypeStruct((B, G, nt, V_ROWS, KEY_TILE), BF16),
        jax.ShapeDtypeStruct((B, G, nt // 2, V_ROWS, 2 * KEY_TILE), BF16),
        jax.ShapeDtypeStruct((B, G, nt, V_ROWS, KEY_TILE), BF16),
        jax.ShapeDtypeStruct((B, LANE, S), F32),
    )
    out_specs = (row(GM_WIDTH), pl.BlockSpec((1, NSA_HEADS, LANE, tm), lambda b, i: (b, 0, 0, i)),
                 lead(2, LANE), lead(G, 2 * LANE), lead(G, LANE),
                 vtile(tm // KEY_TILE, KEY_TILE), vtile(1, 2 * KEY_TILE), vtile(tm // KEY_TILE, KEY_TILE),
                 pl.BlockSpec((1, LANE, tm), lambda b, i: (b, 0, i)))
    in_arrays = (h, p["w_uv"], p["b_uv"], p["gm_ln_g"], p["gm_ln_b"], p["gm_wc"], p["gm_bsb"],
                 p["w_r"], p["b_r"], p["w_t"], p["b_t"])
    in_specs = [row(D_MODEL)] + [_full(a.shape) for a in in_arrays[1:]]
    return pl.pallas_call(
        _proj_kernel, grid=(B, S // tm), in_specs=in_specs, out_specs=out_specs,
        out_shape=out_shape, compiler_params=_params(2), name="proj_gmlp")(*in_arrays)


def _compress_kernel(x_ref, pos_ref, w1_ref, b1_ref, w2k_ref, w2vt_ref, kc_ref, vct_ref):
    kv = pl.program_id(1)
    n = x_ref.shape[2] // CMP_STRIDE
    half = CMP_BLOCK // 2
    pad_row = (lax.broadcasted_iota(jnp.int32, (1, LANE), 1) == HEAD_DIM).astype(F32)
    for g in range(NSA_KV_GROUPS):
        first = jnp.zeros((n, CMP_HIDDEN), F32)
        second = jnp.zeros((n, CMP_HIDDEN), F32)
        for r in range(half):
            xr = x_ref[0, 0, pl.ds(r, n, stride=CMP_STRIDE), :]
            first = first + _dot((xr + pos_ref[0, r:r + 1, :]).astype(BF16), w1_ref[0, g, r])
            second = second + _dot((xr + pos_ref[0, half + r:half + r + 1, :]).astype(BF16),
                                   w1_ref[0, g, half + r])
        hid = first + pltpu.roll(second, n - 1, 0) + b1_ref[0]
        act = (hid * _sigmoid(hid)).astype(BF16)

        @pl.when(kv == 0)
        def _():
            res = _dot(act, w2k_ref[...])
            row = lax.broadcasted_iota(jnp.int32, res.shape, 0)
            res = jnp.where(row == n - 1, pad_row, res)
            kc_ref[0, g] = jnp.concatenate(
                [jnp.broadcast_to(pad_row, (CMP_PAD, LANE)), res,
                 jnp.broadcast_to(pad_row, (CMP_ROWS - CMP_PAD - n, LANE))], axis=0).astype(BF16)

        @pl.when(kv == 1)
        def _():
            vt = _dot_nt(w2vt_ref[...], act)
            col = lax.broadcasted_iota(jnp.int32, vt.shape, 1)
            vt = jnp.where(col == n - 1, 0.0, vt)
            vt = jnp.concatenate([vt, jnp.zeros((V_ROWS, CMP_ROWS - n), F32)], axis=1)
            vct_ref[0, g] = pltpu.roll(vt, CMP_PAD, 1).astype(BF16)


def _compress_call(kvc, p):
    B, _, S, _ = kvc.shape
    G = NSA_KV_GROUPS
    per_kv = lambda a: pl.BlockSpec((1,) + a.shape[1:], lambda b, k: (k,) + (0,) * (a.ndim - 1))
    return pl.pallas_call(
        _compress_kernel, grid=(B, 2),
        in_specs=[pl.BlockSpec((1, 1, S, LANE), lambda b, k: (b, k, 0, 0)),
                  per_kv(p["cmp_pos"]), per_kv(p["cmp_w1"]), per_kv(p["cmp_b1"]),
                  _full(p["cmp_w2k"].shape), _full(p["cmp_w2vt"].shape)],
        out_specs=(pl.BlockSpec((1, G, CMP_ROWS, LANE), lambda b, k: (b, 0, 0, 0)),
                   pl.BlockSpec((1, G, V_ROWS, CMP_ROWS), lambda b, k: (b, 0, 0, 0))),
        out_shape=(jax.ShapeDtypeStruct((B, G, CMP_ROWS, LANE), BF16),
                   jax.ShapeDtypeStruct((B, G, V_ROWS, CMP_ROWS), BF16)),
        compiler_params=_params(2), name="kv_compress",
    )(kvc, p["cmp_pos"], p["cmp_w1"], p["cmp_b1"], p["cmp_w2k"], p["cmp_w2vt"])


def _nsa_kernel(q_ref, ngt_ref, kc_ref, vct_ref, ovt_ref, bct_ref, ks_ref, vst_ref, vst2_ref, kw_ref, vwt_ref,
                 bt0_ref, bt1_ref, o_ref, sc_ref, rhst_ref, m_ref, acc_ref, negf_ref, *ring_refs):
    G = NSA_KV_GROUPS
    groups = range(G)
    sbuf_refs = [ring_refs[g * FAR_BUFS:(g + 1) * FAR_BUFS] for g in groups]
    mbuf_refs = [ring_refs[(G + g) * FAR_BUFS:(G + g + 1) * FAR_BUFS] for g in groups]
    i = pl.program_id(1)
    n_blk = LANE
    blk = lax.broadcasted_iota(jnp.int32, (n_blk, Q_BLOCK), 0)
    qi = lax.broadcasted_iota(jnp.int32, (n_blk, Q_BLOCK), 1)
    eye = (blk == qi).astype(BF16)
    qt = [jnp.concatenate([q_ref[0, g * NSA_HPG + h] for h in range(NSA_HPG)], axis=1) for g in groups]
    bt0 = [bt0_ref[g] for g in groups]
    bt1 = [bt1_ref[g] for g in groups]

    def band_logits(k_ref, g, rhs, extra, flag_col):
        width = k_ref.shape[3]
        before_start = (lax.broadcasted_iota(jnp.int32, (KEY_TILE, width), 1) == flag_col).astype(BF16)
        tiles = []
        for t in range(BAND_TILES):
            tau = i - PAD_TILES + t
            row0 = pl.multiple_of(jnp.maximum(tau, 0) * KEY_TILE, KEY_TILE)
            k = k_ref[0, g, pl.ds(row0, KEY_TILE), :]
            if t < PAD_TILES:
                k = jnp.where(jnp.broadcast_to(tau, k.shape) >= 0, k, before_start)
            s = _dot(k, rhs)
            tiles.append(s if extra[t] is None else s + extra[t])
        return tiles

    def band_softmax(tiles, vt_ref, g, m_old, acc_old):
        m_new = m_old
        for s in tiles:
            m_new = jnp.maximum(m_new, jnp.max(s, axis=0, keepdims=True))
        acc = jnp.exp2(m_old - m_new) * acc_old
        for t, s in enumerate(tiles):
            vt = vt_ref[0, g, jnp.maximum(i - PAD_TILES + t, 0)]
            acc = acc + _dot(vt, jnp.exp2(s - m_new).astype(BF16))
        return acc[0:HEAD_DIM] / acc[HEAD_DIM:HEAD_DIM + 1]

    band = pl.multiple_of(8 * i, 8)
    visible = 8 * i + CMP_BAND

    def compressed_stage(rows):
        def softmax(g):
            last = sc_ref[g, rows - LANE:rows]
            key = rows - LANE + lax.broadcasted_iota(jnp.int32, last.shape, 0)
            last = jnp.where(key >= visible, NEG_INF, last)
            s = jnp.concatenate([sc_ref[g, 0:rows - LANE], last], axis=0) if rows > LANE else last
            m = jnp.maximum(jnp.max(s, axis=0, keepdims=True), M_INIT)
            p = jnp.exp2(s - m)
            l = jnp.sum(p, axis=0, keepdims=True)
            r = jnp.where(l > 0.0, 1.0 / l, 0.0)
            oc = _dot(vct_ref[0, g, :, 0:rows], p.astype(BF16))[0:HEAD_DIM] * r
            pn = p * r
            psum = pn[:, 0:Q_BLOCK]
            for h in range(1, NSA_HPG):
                psum = psum + pn[:, h * Q_BLOCK:(h + 1) * Q_BLOCK]
            p_hi = psum.astype(BF16)
            p_lo = (psum - p_hi.astype(F32)).astype(BF16)
            ovt = ovt_ref[:, 0:rows]
            return oc, _dot(ovt, p_hi) + _dot(ovt, p_lo)

        def run():
            for g in groups:
                sc_ref[g, 0:rows] = _dot(kc_ref[0, g, 0:rows], qt[g])
                sc_ref[g, pl.ds(band, CMP_BAND), :] += bct_ref[g]
            for u in range(FAR_AHEAD):
                for g in groups:
                    bufs[g][u][0][...], bufs[g][u][1][...] = far_logits(g, u)
            return tuple(x for g in groups for x in softmax(g))
        return run

    n_far = jnp.maximum(i - 1, 0) // FAR_TILES
    last_far = jnp.maximum(n_far - 1, 0)
    blocks_per_step = FAR_KEYS // SEL_BLOCK
    bufs = [tuple(zip(sbuf_refs[g], mbuf_refs[g])) for g in groups]

    def far_logits(g, c):
        r0 = pl.multiple_of(FAR_KEYS * jnp.minimum(c, last_far), FAR_KEYS)
        s = _dot(ks_ref[0, g, pl.ds(r0, FAR_KEYS), LANE:2 * LANE], qt[g])
        return s, jnp.max(s.reshape(blocks_per_step, SEL_BLOCK, QCOLS), axis=1)

    row_options = tuple(range(LANE, CMP_ROWS + 1, LANE))
    flat = lax.switch((visible + LANE - 1) // LANE - 1, [compressed_stage(rows) for rows in row_options])
    oc, imp = flat[0::2], flat[1::2]

    cur = (Q_BLOCK // SEL_BLOCK) * i + (qi >= SEL_BLOCK).astype(jnp.int32)
    forced = (blk == 0) | (blk == cur) | (blk == cur - 1)
    taken = -2.0 ** 126
    blk_f = blk.astype(F32)
    start = tuple(jnp.where(forced, taken, jnp.where(blk > cur, NEG_INF, imp[g])) for g in groups)

    def select_stage(rows):
        def pick(_, vals):
            out = []
            for val in vals:
                mx = jnp.max(val, axis=0, keepdims=True)
                first = jnp.min(jnp.where(val == mx, blk_f[0:rows], 1e9), axis=0, keepdims=True)
                out.append(jnp.where(blk_f[0:rows] == first, taken, val))
            return tuple(out)

        def run():
            vals = lax.fori_loop(0, N_SEL - 3, pick, tuple(v[0:rows] for v in start))
            sels = [(val < 2 * NEG_INF).astype(F32) for val in vals]
            return tuple(jnp.concatenate([sel, jnp.zeros((n_blk - rows, Q_BLOCK), F32)], axis=0)
                         if rows < n_blk else sel for sel in sels)
        return run

    row_step = n_blk // 4
    row_options = tuple(range(row_step, n_blk + 1, row_step))
    n_cand = (Q_BLOCK // SEL_BLOCK) * (i + 1)
    picked = lax.switch((n_cand + row_step - 1) // row_step - 1, [select_stage(rows) for rows in row_options])

    per_head = lambda x: jnp.concatenate([x] * NSA_HPG, axis=1)
    for g in groups:
        neg_far = jnp.where((picked[g] > 0.5) & (blk <= cur), 0.0, NEG_INF)
        neg_tail = jnp.where(blk >= n_far * blocks_per_step, neg_far, NEG_INF)
        negf_ref[g] = per_head(neg_far)
        rhst_ref[g, 0:LANE] = per_head(neg_tail.astype(BF16))
        rhst_ref[g, LANE:2 * LANE] = qt[g]
        m_ref[g] = jnp.full(m_ref.shape[1:], M_INIT, F32)
        acc_ref[g] = jnp.zeros(acc_ref.shape[1:], F32)

    def far_softmax(g, c, s_ref, bmax_ref):
        step = jnp.minimum(c, last_far)
        mask = negf_ref[g, pl.ds(pl.multiple_of(blocks_per_step * step, blocks_per_step), blocks_per_step), :]
        mask = jnp.where(c >= n_far, NEG_INF, mask)
        m_old = m_ref[g]
        m_new = jnp.maximum(m_old, jnp.max(bmax_ref[...] + mask, axis=0, keepdims=True))
        shift = mask - m_new
        p = jnp.concatenate([jnp.exp2(s_ref[b * SEL_BLOCK:(b + 1) * SEL_BLOCK] + shift[b:b + 1])
                             for b in range(blocks_per_step)], axis=0).astype(BF16)
        pv = jnp.zeros(acc_ref.shape[1:], F32)
        for t in range(FAR_TILES // 2):
            pv = pv + _dot(vst2_ref[0, g, (FAR_TILES // 2) * step + t], p[2 * t * KEY_TILE:2 * (t + 1) * KEY_TILE])
        acc_ref[g] = jnp.exp2(m_old - m_new) * acc_ref[g] + pv
        m_ref[g] = m_new

    kk = lax.broadcasted_iota(jnp.int32, (KEY_TILE, QCOLS), 0)
    qq = lax.broadcasted_iota(jnp.int32, (KEY_TILE, QCOLS), 1) & (Q_BLOCK - 1)
    edge = jnp.where(kk > qq, 0.0, NEG_INF)
    win_tiles = [band_logits(kw_ref, g, qt[g], [edge] + [None] * (BAND_TILES - 3) + [bt1[g], bt0[g]], HEAD_DIM)
                 for g in groups]
    ow = [band_softmax(win_tiles[g], vwt_ref, g, jnp.full(m_ref.shape[1:], M_INIT, F32),
                       jnp.zeros(acc_ref.shape[1:], F32)) for g in groups]

    n_trips = jnp.maximum(n_far - FAR_AHEAD + FAR_BUFS - 1, 0) // FAR_BUFS

    def far_ring(j, carry):
        for u in range(FAR_BUFS):
            c = FAR_BUFS * j + u
            ahead = [far_logits(g, c + FAR_AHEAD) for g in groups]
            for g in groups:
                far_softmax(g, c, *bufs[g][u])
            for g in groups:
                ahead_s, ahead_mx = bufs[g][(u + FAR_AHEAD) % FAR_BUFS]
                ahead_s[...], ahead_mx[...] = ahead[g]
        return carry

    lax.fori_loop(0, n_trips, far_ring, 0)

    tail_tiles = [band_logits(ks_ref, g, rhst_ref[g], [None] * (BAND_TILES - 2) + [bt1[g], bt0[g]],
                              LANE + HEAD_DIM) for g in groups]
    for u in range(FAR_AHEAD):
        for g in groups:
            far_softmax(g, FAR_BUFS * n_trips + u, *bufs[g][u])
    os_ = [band_softmax(tail_tiles[g], vst_ref, g, m_ref[g], acc_ref[g]) for g in groups]

    for g in groups:
        def gate(branch):
            rows = [ngt_ref[0, NSA_HPG * N_GATES * g + N_GATES * h + branch:
                            NSA_HPG * N_GATES * g + N_GATES * h + branch + 1, :] for h in range(NSA_HPG)]
            return _sigmoid(jnp.concatenate(rows, axis=1))

        o = (gate(0) * oc[g] + gate(1) * os_[g] + gate(2) * ow[g]).astype(BF16)
        heads = jnp.concatenate([o[:, h * Q_BLOCK:(h + 1) * Q_BLOCK] for h in range(NSA_HPG)], axis=0)
        o_ref[0, :, g * NSA_HPG * HEAD_DIM:(g + 1) * NSA_HPG * HEAD_DIM] = _dot_nt(eye, heads).astype(BF16)


def _nsa_call(q, ngt, kc, vct, ovt, bct, ks, vst, vst2, kw, vwt, bt0, bt1):
    B, _, _, S = q.shape
    G = NSA_KV_GROUPS
    nq = S // Q_BLOCK
    per_b = lambda a: pl.BlockSpec((1,) + a.shape[1:], lambda b, i: (b,) + (0,) * (a.ndim - 1),
                                   pipeline_mode=pl.Buffered(1))
    in_specs = [
        pl.BlockSpec((1, NSA_HEADS, LANE, Q_BLOCK), lambda b, i: (b, 0, 0, i)),
        pl.BlockSpec((1, LANE, Q_BLOCK), lambda b, i: (b, 0, i)),
        per_b(kc), per_b(vct), _full(ovt.shape), _full(bct.shape),
        per_b(ks), per_b(vst), per_b(vst2), per_b(kw), per_b(vwt), _full(bt0.shape), _full(bt1.shape),
    ]
    scratch = [
        pltpu.VMEM((G, CMP_ROWS, QCOLS), F32),
        pltpu.VMEM((G, 2 * LANE, QCOLS), BF16),
        pltpu.VMEM((G, 1, QCOLS), F32),
        pltpu.VMEM((G, V_ROWS, QCOLS), F32),
        pltpu.VMEM((G, LANE, QCOLS), F32),
    ]
    scratch += [pltpu.VMEM((FAR_KEYS, QCOLS), F32)] * (G * FAR_BUFS)
    scratch += [pltpu.VMEM((FAR_KEYS // SEL_BLOCK, QCOLS), F32)] * (G * FAR_BUFS)
    return pl.pallas_call(
        _nsa_kernel, grid=(B, nq), in_specs=in_specs,
        out_specs=pl.BlockSpec((1, Q_BLOCK, NSA_WIDTH), lambda b, i: (b, i, 0)),
        out_shape=jax.ShapeDtypeStruct((B, S, NSA_WIDTH), BF16),
        scratch_shapes=scratch, compiler_params=_params(2), name="sparse_attention",
    )(q, ngt, kc, vct, ovt, bct, ks, vst, vst2, kw, vwt, bt0, bt1)


def _merge_kernel(h_ref, a_ref, o_ref, wmg_ref, bmg_ref, wa_ref, wb_ref, wout_ref, lng_ref, lnb_ref, out_ref):
    tm = h_ref.shape[1]
    halves = [slice(k * tm // 2, (k + 1) * tm // 2) for k in range(2)]
    pre = []
    for rows in halves:
        mg = _dot(h_ref[0, rows].astype(BF16), wmg_ref[...]) + bmg_ref[...]
        pre.append((mg, _dot(a_ref[0, rows], wa_ref[...]), _dot(o_ref[0, rows], wb_ref[...])))
    ms = []
    for mg, pa, pb in pre:
        y = _sigmoid(mg[:, :D_MODEL]) * pa + _sigmoid(mg[:, D_MODEL:]) * pb
        ms.append(_dot(y.astype(BF16), wout_ref[...]))
    for rows, m in zip(halves, ms):
        out_ref[0, rows] = _layer_norm(ALPHA * h_ref[0, rows] + m, lng_ref[...], lnb_ref[...])


def _merge_call(h, a, o, p):
    B, S, _ = h.shape
    tm = 2 * ROW_TILE
    row = lambda w: pl.BlockSpec((1, tm, w), lambda b, i: (b, i, 0))
    ws = (p["w_mg"], p["b_mg"], p["w_proj_a"], p["w_proj_b"], p["w_out"], p["ln1_g"], p["ln1_b"])
    return pl.pallas_call(
        _merge_kernel, grid=(B, S // tm),
        in_specs=[row(D_MODEL), row(GM_WIDTH), row(NSA_WIDTH)] + [_full(a.shape) for a in ws],
        out_specs=row(D_MODEL), out_shape=jax.ShapeDtypeStruct((B, S, D_MODEL), F32),
        compiler_params=_params(2), name="merge_norm")(h, a, o, *ws)


def _ffn_kernel(h_ref, wup_ref, cw_ref, wdown_ref, lng_ref, lnb_ref, out_ref, carry_ref):
    tm = h_ref.shape[1]

    @pl.when(pl.program_id(1) == 0)
    def _():
        carry_ref[...] = jnp.zeros(carry_ref.shape, F32)

    h = h_ref[0]
    hb = h.astype(BF16)
    row8 = lax.broadcasted_iota(jnp.int32, (8, FF_CHUNK), 0)

    def up_proj(c):
        cols = lambda idx: slice(idx * FF_CHUNK, (idx + 1) * FF_CHUNK)
        return _dot(hb, wup_ref[:, cols(c)]), _dot(hb, wup_ref[:, cols(N_FF + c)])

    def conv(up, idx):
        prev = carry_ref[idx]
        carry_ref[idx] = up[tm - 8:tm]
        cw = cw_ref[:, idx * FF_CHUNK:(idx + 1) * FF_CHUNK]
        out = cw[3:4] + cw[2:3] * up
        for shift in (1, 2):
            rolled = pltpu.roll(up, shift, 0)
            head = jnp.where(row8 < shift, pltpu.roll(prev, shift, 0), rolled[0:8])
            out = out + cw[2 - shift:3 - shift] * jnp.concatenate([head, rolled[8:]], axis=0)
        return out

    f = jnp.zeros((tm, D_MODEL), F32)
    ups = [up_proj(c) for c in range(FF_AHEAD)]
    for c in range(N_FF):
        g_up, v_up = ups.pop(0)
        if c + FF_AHEAD < N_FF:
            ups.append(up_proj(c + FF_AHEAD))
        gate = conv(g_up, c)
        act = (gate * _sigmoid(gate) * conv(v_up, N_FF + c)).astype(BF16)
        f = f + _dot(act, wdown_ref[c])
    out_ref[0] = _layer_norm(ALPHA * h + f, lng_ref[...], lnb_ref[...])


def _ffn_call(h, p):
    B, S, _ = h.shape
    tm = ROW_TILE
    row = pl.BlockSpec((1, tm, D_MODEL), lambda b, i: (b, i, 0))
    ws = (p["ffn_w_up"], p["ffn_cw"], p["ffn_w_down"], p["ln2_g"], p["ln2_b"])
    return pl.pallas_call(
        _ffn_kernel, grid=(B, S // tm),
        in_specs=[row] + [_full(a.shape) for a in ws],
        out_specs=row, out_shape=jax.ShapeDtypeStruct((B, S, D_MODEL), F32),
        scratch_shapes=[pltpu.VMEM((2 * N_FF, 8, FF_CHUNK), F32)],
        compiler_params=_params(2), name="conv_ffn_norm")(h, *ws)


def _t5_bucket(n):
    max_exact = NUM_BUCKETS // 2
    log_ratio = jnp.log(jnp.maximum(n, 1).astype(F32) / max_exact) / math.log(MAX_DISTANCE / max_exact)
    large = jnp.minimum(max_exact + (log_ratio * (NUM_BUCKETS - max_exact)).astype(jnp.int32), NUM_BUCKETS - 1)
    return jnp.where(n < max_exact, n, large)


def _bias_tiles(rel_bias):
    G, H = NSA_KV_GROUPS, NSA_HPG
    tab = rel_bias[_t5_bucket(jnp.arange(2 * KEY_TILE, dtype=jnp.int32))] - rel_bias[NUM_BUCKETS - 1]
    tab = tab * LOG2E

    def tile(offset, n, row_step=1):
        d = np.arange(2 * n)
        dist = offset + np.where(d < n, d, d - 2 * n)
        w = jnp.where((dist >= 0)[:, None], tab[np.clip(dist, 0, 2 * KEY_TILE - 1)], NEG_INF)
        m = jnp.broadcast_to(w[None], (n, 2 * n, w.shape[1])).reshape(2 * n * n, -1)[:n * (2 * n - 1)]
        t = m.reshape(n, 2 * n - 1, -1)[::row_step, :Q_BLOCK]
        rows = t.shape[0]
        return t.reshape(rows, Q_BLOCK, G, H).transpose(2, 0, 3, 1).reshape(G, rows, H * Q_BLOCK)

    bt0 = tile(0, KEY_TILE)
    bt1 = tile(KEY_TILE, KEY_TILE)
    bct = tile(CMP_STRIDE * CMP_PAD - CMP_BLOCK + 1, CMP_STRIDE * CMP_BAND, CMP_STRIDE)
    return bt0, bt1, bct


def _overlap_t(seq):
    n_cmp = seq // CMP_STRIDE - 1
    n_blk = seq // SEL_BLOCK
    cs = np.arange(n_cmp)[:, None] * CMP_STRIDE
    ss = np.arange(n_blk)[None, :] * SEL_BLOCK
    ov = np.maximum(np.minimum(cs + CMP_BLOCK, ss + SEL_BLOCK) - np.maximum(cs, ss), 0) / CMP_BLOCK
    out = np.zeros((LANE, CMP_ROWS), np.float32)
    out[:n_blk, CMP_PAD:CMP_PAD + n_cmp] = ov.T
    return jnp.asarray(out, BF16)


def _pad_cols(w, width):
    return jnp.pad(w, ((0, 0), (0, width - w.shape[1])))


def _layer_params(l, w_in, b_in, gm_ln_g, gm_ln_b, gm_ws, gm_bs, cmp_pos, cmp_w1, cmp_b1, cmp_w2,
                  w_proj_a, w_proj_b, w_out, ln1_g, ln1_b, ffn_w_up, ffn_conv_w, ffn_conv_b, ffn_w_down,
                  ln2_g, ln2_b):
    G = NSA_KV_GROUPS
    w, b = w_in[l], b_in[l]
    o_q = 2 * GM_WIDTH
    o_kc = o_q + NSA_WIDTH
    o_vc, o_ks, o_vs, o_kw, o_vw = (o_kc + KV_WIDTH * k for k in range(1, 6))
    o_ng = o_vw + KV_WIDTH
    n_ng = NSA_HEADS * N_GATES
    o_mg = o_ng + n_ng
    seg = lambda a, o, n: a[..., o:o + n]

    wq = jnp.pad(seg(w, o_q, NSA_WIDTH).T.reshape(NSA_HEADS, HEAD_DIM, D_MODEL),
                 ((0, 0), (0, LANE - HEAD_DIM), (0, 0))).reshape(NSA_HEADS * LANE, D_MODEL)
    bq = jnp.pad(seg(b, o_q, NSA_WIDTH).reshape(NSA_HEADS, HEAD_DIM), ((0, 0), (0, LANE - HEAD_DIM)))
    bq = bq.at[:, HEAD_DIM].set(NEG_INF / Q_SCALE).reshape(NSA_HEADS * LANE)

    def k_groups(o):
        return [_pad_cols(seg(w, o + g * HEAD_DIM, HEAD_DIM), LANE) for g in range(G)], \
               [jnp.pad(seg(b, o + g * HEAD_DIM, HEAD_DIM), (0, LANE - HEAD_DIM)) for g in range(G)]

    wks, bks = k_groups(o_ks)
    wkw, bkw = k_groups(o_kw)
    w_r = jnp.concatenate([seg(w, o_kc, KV_WIDTH), seg(w, o_vc, KV_WIDTH), *wks, *wkw,
                           _pad_cols(seg(w, o_ng, n_ng), LANE)], axis=1)
    b_r = jnp.concatenate([seg(b, o_kc, KV_WIDTH), seg(b, o_vc, KV_WIDTH), *bks, *bkw,
                           jnp.pad(seg(b, o_ng, n_ng), (0, LANE - n_ng))])

    def v_rows(o):
        wt = jnp.pad(seg(w, o, HEAD_DIM).T, ((0, V_ROWS - HEAD_DIM), (0, 0)))
        bt = jnp.pad(seg(b, o, HEAD_DIM), (0, V_ROWS - HEAD_DIM)).at[HEAD_DIM].set(1.0)
        return wt, bt

    vparts = [v_rows(o + g * HEAD_DIM) for o in (o_vs, o_vw) for g in range(G)]
    w_t = jnp.concatenate([wq] + [wt for wt, _ in vparts], axis=0)
    b_t = jnp.concatenate([bq] + [bt for _, bt in vparts])
    row_scale = jnp.where(jnp.arange(b_t.shape[0]) < NSA_HEADS * LANE, Q_SCALE, 1.0)
    b_t = jnp.stack([b_t, row_scale], axis=1)

    causal = jnp.tril(jnp.ones((GM_CHUNK, GM_CHUNK), F32))
    gw = GM_WIDTH // GM_GROUPS
    bs = gm_bs[l]
    bsb = jnp.concatenate([jnp.broadcast_to(bs[0::2, :, None], (GM_GROUPS // 2, GM_CHUNK, gw)),
                           jnp.broadcast_to(bs[1::2, :, None], (GM_GROUPS // 2, GM_CHUNK, gw))], axis=2)

    w1 = cmp_w1[l].reshape(2, CMP_BLOCK, HEAD_DIM, CMP_HIDDEN)
    w1 = jnp.stack([jnp.pad(w1, ((0, 0), (0, 0), (g * HEAD_DIM, (G - 1 - g) * HEAD_DIM), (0, 0)))
                    for g in range(G)], axis=1)
    w2 = cmp_w2[l]
    cw = jnp.concatenate([ffn_conv_w[l], ffn_conv_b[l][None], jnp.zeros((8 - CONV_WIDTH - 1, 2 * D_FF), F32)])
    return {
        "w_uv": seg(w, 0, 2 * GM_WIDTH).astype(BF16), "b_uv": seg(b, 0, 2 * GM_WIDTH)[None],
        "gm_ln_g": gm_ln_g[l][None], "gm_ln_b": gm_ln_b[l][None],
        "gm_wc": (gm_ws[l] * causal).astype(BF16), "gm_bsb": bsb,
        "w_r": w_r.astype(BF16), "b_r": b_r[None], "w_t": w_t.astype(BF16), "b_t": b_t,
        "cmp_pos": jnp.tile(cmp_pos[l], (1, 1, G)), "cmp_w1": w1.astype(BF16), "cmp_b1": cmp_b1[l][:, None],
        "cmp_w2k": _pad_cols(w2[0], LANE).astype(BF16),
        "cmp_w2vt": jnp.pad(w2[1].T, ((0, V_ROWS - HEAD_DIM), (0, 0))).astype(BF16),
        "w_mg": seg(w, o_mg, 2 * D_MODEL).astype(BF16), "b_mg": seg(b, o_mg, 2 * D_MODEL)[None],
        "w_proj_a": w_proj_a[l].astype(BF16), "w_proj_b": w_proj_b[l].astype(BF16),
        "w_out": w_out[l].astype(BF16), "ln1_g": ln1_g[l][None], "ln1_b": ln1_b[l][None],
        "ffn_w_up": ffn_w_up[l].astype(BF16), "ffn_cw": cw,
        "ffn_w_down": ffn_w_down[l].reshape(N_FF, FF_CHUNK, D_MODEL).astype(BF16),
        "ln2_g": ln2_g[l][None], "ln2_b": ln2_b[l][None],
    }


def kernel(x, w_in, b_in, gm_ln_g, gm_ln_b, gm_ws, gm_bs, cmp_pos, cmp_w1, cmp_b1, cmp_w2, rel_bias,
           w_proj_a, w_proj_b, w_out, ln1_g, ln1_b, ffn_w_up, ffn_conv_w, ffn_conv_b, ffn_w_down,
           ln2_g, ln2_b):
    B, S, _ = x.shape
    assert S % FAR_KEYS == 0 and N_SEL <= S // SEL_BLOCK <= LANE
    assert S // CMP_STRIDE - 1 + CMP_PAD < CMP_ROWS
    bt0, bt1, bct = _bias_tiles(rel_bias)
    ovt = _overlap_t(S)
    h = x
    for l in range(DEPTH):
        p = _layer_params(l, w_in, b_in, gm_ln_g, gm_ln_b, gm_ws, gm_bs, cmp_pos, cmp_w1, cmp_b1, cmp_w2,
                          w_proj_a, w_proj_b, w_out, ln1_g, ln1_b, ffn_w_up, ffn_conv_w, ffn_conv_b,
                          ffn_w_down, ln2_g, ln2_b)
        a, q, kvc, ks, kw, vst, vst2, vwt, ngt = _proj_call(h, p)
        kc, vct = _compress_call(kvc, p)
        o = _nsa_call(q, ngt, kc, vct, ovt, bct, ks, vst, vst2, kw, vwt, bt0, bt1)
        h = _merge_call(h, a, o, p)
        h = _ffn_call(h, p)
    return h
```

```python
import math

import jax
import jax.numpy as jnp
import numpy as np
from jax import lax
from jax.experimental import pallas as pl
from jax.experimental.pallas import tpu as pltpu

D_MODEL = 1024
DEPTH = 2
GM_WIDTH = D_MODEL // 2
GM_GROUPS = 8
GM_CHUNK = 128
NSA_HEADS = 8
NSA_KV_GROUPS = 2
NSA_HPG = NSA_HEADS // NSA_KV_GROUPS
HEAD_DIM = 64
NSA_WIDTH = NSA_HEADS * HEAD_DIM
KV_WIDTH = NSA_KV_GROUPS * HEAD_DIM
N_GATES = 3
CMP_BLOCK = 32
CMP_STRIDE = 16
CMP_HIDDEN = 256
SEL_BLOCK = 64
N_SEL = 16
WINDOW = 512
Q_BLOCK = 128
NUM_BUCKETS = 32
MAX_DISTANCE = 128
D_FF = 2816
CONV_WIDTH = 3
ALPHA = (2.0 * DEPTH) ** 0.25
LN_EPS = 1e-5
FORCED_SCORE = 1e6
NEG_INF = -1e30
M_INIT = -1e29
LOG2E = math.log2(math.e)
Q_SCALE = HEAD_DIM ** -0.5 * LOG2E

LANE = 128
QCOLS = NSA_HPG * Q_BLOCK
KEY_TILE = 128
FAR_TILES = 4
FAR_KEYS = FAR_TILES * KEY_TILE
FAR_AHEAD = 2
FAR_BUFS = 2
BAND_TILES = WINDOW // KEY_TILE + 1
PAD_TILES = BAND_TILES - 1
CMP_PAD = 8
CMP_ROWS = 640
CMP_BAND = 16
V_ROWS = 80
FF_CHUNK = 256
N_FF = D_FF // FF_CHUNK
FF_AHEAD = 2
ROW_TILE = 256
VMEM_LIMIT = 56 * 1024 * 1024

F32 = jnp.float32
BF16 = jnp.bfloat16
NT_DIMS = (((1,), (1,)), ((), ()))


def _dot(a, b):
    return jnp.dot(a, b, preferred_element_type=F32)


def _dot_nt(a, b):
    return lax.dot_general(a, b, NT_DIMS, preferred_element_type=F32)


def _layer_norm(x, g, b):
    mu = jnp.mean(x, axis=-1, keepdims=True)
    xc = x - mu
    var = jnp.mean(xc * xc, axis=-1, keepdims=True)
    return xc * lax.rsqrt(var + LN_EPS) * g + b


def _gelu_tanh(x):
    c = math.sqrt(2.0 / math.pi)
    return x * (0.5 * (1.0 + jnp.tanh(c * (x + 0.044715 * (x * x * x)))))


def _sigmoid(x):
    return 1.0 / (1.0 + jnp.exp(-x))


def _params(n_axes):
    return pltpu.CompilerParams(dimension_semantics=("arbitrary",) * n_axes,
                                vmem_limit_bytes=VMEM_LIMIT)


def _full(shape):
    n = len(shape)
    return pl.BlockSpec(shape, lambda *_: (0,) * n)


def _proj_kernel(h_ref, wuv_ref, buv_ref, lng_ref, lnb_ref, wc_ref, bsb_ref, wr_ref, br_ref, wt_ref, bt_ref,
                 a_ref, q_ref, kvc_ref, ks_ref, kw_ref, vst_ref, vst2_ref, vwt_ref, ngt_ref):
    tm = h_ref.shape[1]
    G = NSA_KV_GROUPS
    hb = h_ref[0].astype(BF16)

    uv = _dot(hb, wuv_ref[...]) + buv_ref[...]
    r = _dot(hb, wr_ref[...]) + br_ref[...]
    tr = ((_dot_nt(wt_ref[...], hb) + bt_ref[:, 0:1]) * bt_ref[:, 1:2]).astype(BF16)

    u = _gelu_tanh(uv[:, :GM_WIDTH])
    v = _layer_norm(_gelu_tanh(uv[:, GM_WIDTH:]), lng_ref[...], lnb_ref[...]).astype(BF16)
    lane = lax.broadcasted_iota(jnp.int32, (GM_CHUNK, LANE), 1)
    group_w = GM_WIDTH // GM_GROUPS
    for ch in range(tm // GM_CHUNK):
        rows = slice(ch * GM_CHUNK, (ch + 1) * GM_CHUNK)
        for pr in range(GM_GROUPS // 2):
            cols = slice(pr * LANE, (pr + 1) * LANE)
            x = v[rows, cols]
            sv = jnp.where(lane < group_w, _dot(wc_ref[2 * pr], x), _dot(wc_ref[2 * pr + 1], x))
            a_ref[0, rows, cols] = (u[rows, cols] * (sv + bsb_ref[pr])).astype(BF16)

    for hh in range(NSA_HEADS):
        q_ref[0, hh] = tr[hh * LANE:(hh + 1) * LANE]
    vt = tr[NSA_HEADS * LANE:]

    kvc_ref[0, 0] = r[:, 0:LANE]
    kvc_ref[0, 1] = r[:, LANE:2 * LANE]
    token = pl.program_id(1) * tm + lax.broadcasted_iota(jnp.int32, (tm, LANE), 0)
    sel_block = lax.shift_right_logical(token, int(math.log2(SEL_BLOCK)))
    onehot = (lax.broadcasted_iota(jnp.int32, (tm, LANE), 1) == sel_block).astype(BF16)
    for g in range(G):
        ks_ref[0, g, :, 0:LANE] = onehot
        ks_ref[0, g, :, LANE:2 * LANE] = r[:, (2 + g) * LANE:(3 + g) * LANE].astype(BF16)
        kw_ref[0, g] = r[:, (4 + g) * LANE:(5 + g) * LANE].astype(BF16)
    ngt_ref[0] = r[:, 6 * LANE:7 * LANE].T

    for g in range(G):
        sel_rows = slice(g * V_ROWS, (g + 1) * V_ROWS)
        win_rows = slice((G + g) * V_ROWS, (G + g + 1) * V_ROWS)
        vst2_ref[0, g, 0] = vt[sel_rows]
        for t in range(tm // KEY_TILE):
            cols = slice(t * KEY_TILE, (t + 1) * KEY_TILE)
            vst_ref[0, g, t] = vt[sel_rows, cols]
            vwt_ref[0, g, t] = vt[win_rows, cols]


def _proj_call(h, p):
    B, S, _ = h.shape
    tm = 2 * KEY_TILE
    G = NSA_KV_GROUPS
    nt = S // KEY_TILE
    row = lambda w: pl.BlockSpec((1, tm, w), lambda b, i: (b, i, 0))
    lead = lambda n, w: pl.BlockSpec((1, n, tm, w), lambda b, i: (b, 0, i, 0))
    vtile = lambda n, w: pl.BlockSpec((1, G, n, V_ROWS, w), lambda b, i: (b, 0, i, 0, 0))
    out_shape = (
        jax.ShapeDtypeStruct((B, S, GM_WIDTH), BF16),
        jax.ShapeDtypeStruct((B, NSA_HEADS, LANE, S), BF16),
        jax.ShapeDtypeStruct((B, 2, S, LANE), F32),
        jax.ShapeDtypeStruct((B, G, S, 2 * LANE), BF16),
        jax.ShapeDtypeStruct((B, G, S, LANE), BF16),
        jax.ShapeDtypeStruct((B, G, nt, V_ROWS, KEY_TILE), BF16),
        jax.ShapeDtypeStruct((B, G, nt // 2, V_ROWS, 2 * KEY_TILE), BF16),
        jax.ShapeDtypeStruct((B, G, nt, V_ROWS, KEY_TILE), BF16),
        jax.ShapeDtypeStruct((B, LANE, S), F32),
    )
    out_specs = (row(GM_WIDTH), pl.BlockSpec((1, NSA_HEADS, LANE, tm), lambda b, i: (b, 0, 0, i)),
                 lead(2, LANE), lead(G, 2 * LANE), lead(G, LANE),
                 vtile(tm // KEY_TILE, KEY_TILE), vtile(1, 2 * KEY_TILE), vtile(tm // KEY_TILE, KEY_TILE),
                 pl.BlockSpec((1, LANE, tm), lambda b, i: (b, 0, i)))
    in_arrays = (h, p["w_uv"], p["b_uv"], p["gm_ln_g"], p["gm_ln_b"], p["gm_wc"], p["gm_bsb"],
                 p["w_r"], p["b_r"], p["w_t"], p["b_t"])
    in_specs = [row(D_MODEL)] + [_full(a.shape) for a in in_arrays[1:]]
    return pl.pallas_call(
        _proj_kernel, grid=(B, S // tm), in_specs=in_specs, out_specs=out_specs,
        out_shape=out_shape, compiler_params=_params(2), name="proj_gmlp")(*in_arrays)


def _compress_kernel(x_ref, pos_ref, w1_ref, b1_ref, w2k_ref, w2vt_ref, kc_ref, vct_ref):
    kv = pl.program_id(1)
    n = x_ref.shape[2] // CMP_STRIDE
    half = CMP_BLOCK // 2
    pad_row = (lax.broadcasted_iota(jnp.int32, (1, LANE), 1) == HEAD_DIM).astype(F32)
    for g in range(NSA_KV_GROUPS):
        first = jnp.zeros((n, CMP_HIDDEN), F32)
        second = jnp.zeros((n, CMP_HIDDEN), F32)
        for r in range(half):
            xr = x_ref[0, 0, pl.ds(r, n, stride=CMP_STRIDE), :]
            first = first + _dot((xr + pos_ref[0, r:r + 1, :]).astype(BF16), w1_ref[0, g, r])
            second = second + _dot((xr + pos_ref[0, half + r:half + r + 1, :]).astype(BF16),
                                   w1_ref[0, g, half + r])
        hid = first + pltpu.roll(second, n - 1, 0) + b1_ref[0]
        act = (hid * _sigmoid(hid)).astype(BF16)

        @pl.when(kv == 0)
        def _():
            res = _dot(act, w2k_ref[...])
            row = lax.broadcasted_iota(jnp.int32, res.shape, 0)
            res = jnp.where(row == n - 1, pad_row, res)
            kc_ref[0, g] = jnp.concatenate(
                [jnp.broadcast_to(pad_row, (CMP_PAD, LANE)), res,
                 jnp.broadcast_to(pad_row, (CMP_ROWS - CMP_PAD - n, LANE))], axis=0).astype(BF16)

        @pl.when(kv == 1)
        def _():
            vt = _dot_nt(w2vt_ref[...], act)
            col = lax.broadcasted_iota(jnp.int32, vt.shape, 1)
            vt = jnp.where(col == n - 1, 0.0, vt)
            vt = jnp.concatenate([vt, jnp.zeros((V_ROWS, CMP_ROWS - n), F32)], axis=1)
            vct_ref[0, g] = pltpu.roll(vt, CMP_PAD, 1).astype(BF16)


def _compress_call(kvc, p):
    B, _, S, _ = kvc.shape
    G = NSA_KV_GROUPS
    per_kv = lambda a: pl.BlockSpec((1,) + a.shape[1:], lambda b, k: (k,) + (0,) * (a.ndim - 1))
    return pl.pallas_call(
        _compress_kernel, grid=(B, 2),
        in_specs=[pl.BlockSpec((1, 1, S, LANE), lambda b, k: (b, k, 0, 0)),
                  per_kv(p["cmp_pos"]), per_kv(p["cmp_w1"]), per_kv(p["cmp_b1"]),
                  _full(p["cmp_w2k"].shape), _full(p["cmp_w2vt"].shape)],
        out_specs=(pl.BlockSpec((1, G, CMP_ROWS, LANE), lambda b, k: (b, 0, 0, 0)),
                   pl.BlockSpec((1, G, V_ROWS, CMP_ROWS), lambda b, k: (b, 0, 0, 0))),
        out_shape=(jax.ShapeDtypeStruct((B, G, CMP_ROWS, LANE), BF16),
                   jax.ShapeDtypeStruct((B, G, V_ROWS, CMP_ROWS), BF16)),
        compiler_params=_params(2), name="kv_compress",
    )(kvc, p["cmp_pos"], p["cmp_w1"], p["cmp_b1"], p["cmp_w2k"], p["cmp_w2vt"])


def _nsa_kernel(q_ref, ngt_ref, kc_ref, vct_ref, ovt_ref, bct_ref, ks_ref, vst_ref, vst2_ref, kw_ref, vwt_ref,
                 bt0_ref, bt1_ref, o_ref, sc_ref, rhst_ref, m_ref, acc_ref, negf_ref, *ring_refs):
    G = NSA_KV_GROUPS
    groups = range(G)
    sbuf_refs = [ring_refs[g * FAR_BUFS:(g + 1) * FAR_BUFS] for g in groups]
    mbuf_refs = [ring_refs[(G + g) * FAR_BUFS:(G + g + 1) * FAR_BUFS] for g in groups]
    i = pl.program_id(1)
    n_blk = LANE
    blk = lax.broadcasted_iota(jnp.int32, (n_blk, Q_BLOCK), 0)
    qi = lax.broadcasted_iota(jnp.int32, (n_blk, Q_BLOCK), 1)
    eye = (blk == qi).astype(BF16)
    qt = [jnp.concatenate([q_ref[0, g * NSA_HPG + h] for h in range(NSA_HPG)], axis=1) for g in groups]
    bt0 = [bt0_ref[g] for g in groups]
    bt1 = [bt1_ref[g] for g in groups]

    def band_logits(k_ref, g, rhs, extra, flag_col):
        width = k_ref.shape[3]
        before_start = (lax.broadcasted_iota(jnp.int32, (KEY_TILE, width), 1) == flag_col).astype(BF16)
        tiles = []
        for t in range(BAND_TILES):
            tau = i - PAD_TILES + t
            row0 = pl.multiple_of(jnp.maximum(tau, 0) * KEY_TILE, KEY_TILE)
            k = k_ref[0, g, pl.ds(row0, KEY_TILE), :]
            if t < PAD_TILES:
                k = jnp.where(jnp.broadcast_to(tau, k.shape) >= 0, k, before_start)
            s = _dot(k, rhs)
            tiles.append(s if extra[t] is None else s + extra[t])
        return tiles

    def band_softmax(tiles, vt_ref, g, m_old, acc_old):
        m_new = m_old
        for s in tiles:
            m_new = jnp.maximum(m_new, jnp.max(s, axis=0, keepdims=True))
        acc = jnp.exp2(m_old - m_new) * acc_old
        for t, s in enumerate(tiles):
            vt = vt_ref[0, g, jnp.maximum(i - PAD_TILES + t, 0)]
            acc = acc + _dot(vt, jnp.exp2(s - m_new).astype(BF16))
        return acc[0:HEAD_DIM] / acc[HEAD_DIM:HEAD_DIM + 1]

    band = pl.multiple_of(8 * i, 8)
    visible = 8 * i + CMP_BAND

    def compressed_stage(rows):
        def softmax(g):
            last = sc_ref[g, rows - LANE:rows]
            key = rows - LANE + lax.broadcasted_iota(jnp.int32, last.shape, 0)
            last = jnp.where(key >= visible, NEG_INF, last)
            s = jnp.concatenate([sc_ref[g, 0:rows - LANE], last], axis=0) if rows > LANE else last
            m = jnp.maximum(jnp.max(s, axis=0, keepdims=True), M_INIT)
            p = jnp.exp2(s - m)
            l = jnp.sum(p, axis=0, keepdims=True)
            r = jnp.where(l > 0.0, 1.0 / l, 0.0)
            oc = _dot(vct_ref[0, g, :, 0:rows], p.astype(BF16))[0:HEAD_DIM] * r
            pn = p * r
            psum = pn[:, 0:Q_BLOCK]
            for h in range(1, NSA_HPG):
                psum = psum + pn[:, h * Q_BLOCK:(h + 1) * Q_BLOCK]
            p_hi = psum.astype(BF16)
            p_lo = (psum - p_hi.astype(F32)).astype(BF16)
            ovt = ovt_ref[:, 0:rows]
            return oc, _dot(ovt, p_hi) + _dot(ovt, p_lo)

        def run():
            for g in groups:
                sc_ref[g, 0:rows] = _dot(kc_ref[0, g, 0:rows], qt[g])
                sc_ref[g, pl.ds(band, CMP_BAND), :] += bct_ref[g]
            for u in range(FAR_AHEAD):
                for g in groups:
                    bufs[g][u][0][...], bufs[g][u][1][...] = far_logits(g, u)
            return tuple(x for g in groups for x in softmax(g))
        return run

    n_far = jnp.maximum(i - 1, 0) // FAR_TILES
    last_far = jnp.maximum(n_far - 1, 0)
    blocks_per_step = FAR_KEYS // SEL_BLOCK
    bufs = [tuple(zip(sbuf_refs[g], mbuf_refs[g])) for g in groups]

    def far_logits(g, c):
        r0 = pl.multiple_of(FAR_KEYS * jnp.minimum(c, last_far), FAR_KEYS)
        s = _dot(ks_ref[0, g, pl.ds(r0, FAR_KEYS), LANE:2 * LANE], qt[g])
        return s, jnp.max(s.reshape(blocks_per_step, SEL_BLOCK, QCOLS), axis=1)

    row_options = tuple(range(LANE, CMP_ROWS + 1, LANE))
    flat = lax.switch((visible + LANE - 1) // LANE - 1, [compressed_stage(rows) for rows in row_options])
    oc, imp = flat[0::2], flat[1::2]

    cur = (Q_BLOCK // SEL_BLOCK) * i + (qi >= SEL_BLOCK).astype(jnp.int32)
    forced = (blk == 0) | (blk == cur) | (blk == cur - 1)
    taken = -2.0 ** 126
    blk_f = blk.astype(F32)
    start = tuple(jnp.where(forced, taken, jnp.where(blk > cur, NEG_INF, imp[g])) for g in groups)

    def select_stage(rows):
        def pick(_, vals):
            out = []
            for val in vals:
                mx = jnp.max(val, axis=0, keepdims=True)
                first = jnp.min(jnp.where(val == mx, blk_f[0:rows], 1e9), axis=0, keepdims=True)
                out.append(jnp.where(blk_f[0:rows] == first, taken, val))
            return tuple(out)

        def run():
            vals = lax.fori_loop(0, N_SEL - 3, pick, tuple(v[0:rows] for v in start))
            sels = [(val < 2 * NEG_INF).astype(F32) for val in vals]
            return tuple(jnp.concatenate([sel, jnp.zeros((n_blk - rows, Q_BLOCK), F32)], axis=0)
                         if rows < n_blk else sel for sel in sels)
        return run

    row_step = n_blk // 4
    row_options = tuple(range(row_step, n_blk + 1, row_step))
    n_cand = (Q_BLOCK // SEL_BLOCK) * (i + 1)
    picked = lax.switch((n_cand + row_step - 1) // row_step - 1, [select_stage(rows) for rows in row_options])

    per_head = lambda x: jnp.concatenate([x] * NSA_HPG, axis=1)
    for g in groups:
        neg_far = jnp.where((picked[g] > 0.5) & (blk <= cur), 0.0, NEG_INF)
        neg_tail = jnp.where(blk >= n_far * blocks_per_step, neg_far, NEG_INF)
        negf_ref[g] = per_head(neg_far)
        rhst_ref[g, 0:LANE] = per_head(neg_tail.astype(BF16))
        rhst_ref[g, LANE:2 * LANE] = qt[g]
        m_ref[g] = jnp.full(m_ref.shape[1:], M_INIT, F32)
        acc_ref[g] = jnp.zeros(acc_ref.shape[1:], F32)

    def far_softmax(g, c, s_ref, bmax_ref):
        step = jnp.minimum(c, last_far)
        mask = negf_ref[g, pl.ds(pl.multiple_of(blocks_per_step * step, blocks_per_step), blocks_per_step), :]
        mask = jnp.where(c >= n_far, NEG_INF, mask)
        m_old = m_ref[g]
        m_new = jnp.maximum(m_old, jnp.max(bmax_ref[...] + mask, axis=0, keepdims=True))
        shift = mask - m_new
        p = jnp.concatenate([jnp.exp2(s_ref[b * SEL_BLOCK:(b + 1) * SEL_BLOCK] + shift[b:b + 1])
                             for b in range(blocks_per_step)], axis=0).astype(BF16)
        pv = jnp.zeros(acc_ref.shape[1:], F32)
        for t in range(FAR_TILES // 2):
            pv = pv + _dot(vst2_ref[0, g, (FAR_TILES // 2) * step + t], p[2 * t * KEY_TILE:2 * (t + 1) * KEY_TILE])
        acc_ref[g] = jnp.exp2(m_old - m_new) * acc_ref[g] + pv
        m_ref[g] = m_new

    kk = lax.broadcasted_iota(jnp.int32, (KEY_TILE, QCOLS), 0)
    qq = lax.broadcasted_iota(jnp.int32, (KEY_TILE, QCOLS), 1) & (Q_BLOCK - 1)
    edge = jnp.where(kk > qq, 0.0, NEG_INF)
    win_tiles = [band_logits(kw_ref, g, qt[g], [edge] + [None] * (BAND_TILES - 3) + [bt1[g], bt0[g]], HEAD_DIM)
                 for g in groups]
    ow = [band_softmax(win_tiles[g], vwt_ref, g, jnp.full(m_ref.shape[1:], M_INIT, F32),
                       jnp.zeros(acc_ref.shape[1:], F32)) for g in groups]

    n_trips = jnp.maximum(n_far - FAR_AHEAD + FAR_BUFS - 1, 0) // FAR_BUFS

    def far_ring(j, carry):
        for u in range(FAR_BUFS):
            c = FAR_BUFS * j + u
            ahead = [far_logits(g, c + FAR_AHEAD) for g in groups]
            for g in groups:
                far_softmax(g, c, *bufs[g][u])
            for g in groups:
                ahead_s, ahead_mx = bufs[g][(u + FAR_AHEAD) % FAR_BUFS]
                ahead_s[...], ahead_mx[...] = ahead[g]
        return carry

    def far_ring_pair(j, carry):
        return far_ring(2 * j + 1, far_ring(2 * j, carry))

    lax.fori_loop(0, n_trips // 2, far_ring_pair, 0)
    lax.fori_loop(2 * (n_trips // 2), n_trips, far_ring, 0)

    tail_tiles = [band_logits(ks_ref, g, rhst_ref[g], [None] * (BAND_TILES - 2) + [bt1[g], bt0[g]],
                              LANE + HEAD_DIM) for g in groups]
    for u in range(FAR_AHEAD):
        for g in groups:
            far_softmax(g, FAR_BUFS * n_trips + u, *bufs[g][u])
    os_ = [band_softmax(tail_tiles[g], vst_ref, g, m_ref[g], acc_ref[g]) for g in groups]

    for g in groups:
        def gate(branch):
            rows = [ngt_ref[0, NSA_HPG * N_GATES * g + N_GATES * h + branch:
                            NSA_HPG * N_GATES * g + N_GATES * h + branch + 1, :] for h in range(NSA_HPG)]
            return _sigmoid(jnp.concatenate(rows, axis=1))

        o = (gate(0) * oc[g] + gate(1) * os_[g] + gate(2) * ow[g]).astype(BF16)
        heads = jnp.concatenate([o[:, h * Q_BLOCK:(h + 1) * Q_BLOCK] for h in range(NSA_HPG)], axis=0)
        o_ref[0, :, g * NSA_HPG * HEAD_DIM:(g + 1) * NSA_HPG * HEAD_DIM] = _dot_nt(eye, heads).astype(BF16)


def _nsa_call(q, ngt, kc, vct, ovt, bct, ks, vst, vst2, kw, vwt, bt0, bt1):
    B, _, _, S = q.shape
    G = NSA_KV_GROUPS
    nq = S // Q_BLOCK
    per_b = lambda a: pl.BlockSpec((1,) + a.shape[1:], lambda b, i: (b,) + (0,) * (a.ndim - 1),
                                   pipeline_mode=pl.Buffered(1))
    in_specs = [
        pl.BlockSpec((1, NSA_HEADS, LANE, Q_BLOCK), lambda b, i: (b, 0, 0, i)),
        pl.BlockSpec((1, LANE, Q_BLOCK), lambda b, i: (b, 0, i)),
        per_b(kc), per_b(vct), _full(ovt.shape), _full(bct.shape),
        per_b(ks), per_b(vst), per_b(vst2), per_b(kw), per_b(vwt), _full(bt0.shape), _full(bt1.shape),
    ]
    scratch = [
        pltpu.VMEM((G, CMP_ROWS, QCOLS), F32),
        pltpu.VMEM((G, 2 * LANE, QCOLS), BF16),
        pltpu.VMEM((G, 1, QCOLS), F32),
        pltpu.VMEM((G, V_ROWS, QCOLS), F32),
        pltpu.VMEM((G, LANE, QCOLS), F32),
    ]
    scratch += [pltpu.VMEM((FAR_KEYS, QCOLS), F32)] * (G * FAR_BUFS)
    scratch += [pltpu.VMEM((FAR_KEYS // SEL_BLOCK, QCOLS), F32)] * (G * FAR_BUFS)
    return pl.pallas_call(
        _nsa_kernel, grid=(B, nq), in_specs=in_specs,
        out_specs=pl.BlockSpec((1, Q_BLOCK, NSA_WIDTH), lambda b, i: (b, i, 0)),
        out_shape=jax.ShapeDtypeStruct((B, S, NSA_WIDTH), BF16),
        scratch_shapes=scratch, compiler_params=_params(2), name="sparse_attention",
    )(q, ngt, kc, vct, ovt, bct, ks, vst, vst2, kw, vwt, bt0, bt1)


def _merge_kernel(h_ref, a_ref, o_ref, wmg_ref, bmg_ref, wa_ref, wb_ref, wout_ref, lng_ref, lnb_ref, out_ref):
    tm = h_ref.shape[1]
    halves = [slice(k * ROW_TILE, (k + 1) * ROW_TILE) for k in range(tm // ROW_TILE)]
    pre = []
    for rows in halves:
        mg = _dot(h_ref[0, rows].astype(BF16), wmg_ref[...]) + bmg_ref[...]
        pre.append((mg, _dot(a_ref[0, rows], wa_ref[...]), _dot(o_ref[0, rows], wb_ref[...])))
    ms = []
    for mg, pa, pb in pre:
        y = _sigmoid(mg[:, :D_MODEL]) * pa + _sigmoid(mg[:, D_MODEL:]) * pb
        ms.append(_dot(y.astype(BF16), wout_ref[...]))
    for rows, m in zip(halves, ms):
        out_ref[0, rows] = _layer_norm(ALPHA * h_ref[0, rows] + m, lng_ref[...], lnb_ref[...])


def _merge_call(h, a, o, p):
    B, S, _ = h.shape
    tm = 4 * ROW_TILE
    row = lambda w: pl.BlockSpec((1, tm, w), lambda b, i: (b, i, 0))
    ws = (p["w_mg"], p["b_mg"], p["w_proj_a"], p["w_proj_b"], p["w_out"], p["ln1_g"], p["ln1_b"])
    return pl.pallas_call(
        _merge_kernel, grid=(B, S // tm),
        in_specs=[row(D_MODEL), row(GM_WIDTH), row(NSA_WIDTH)] + [_full(a.shape) for a in ws],
        out_specs=row(D_MODEL), out_shape=jax.ShapeDtypeStruct((B, S, D_MODEL), F32),
        compiler_params=_params(2), name="merge_norm")(h, a, o, *ws)


def _ffn_kernel(h_ref, wup_ref, cw_ref, wdown_ref, lng_ref, lnb_ref, out_ref, carry_ref):
    tm = h_ref.shape[1]

    @pl.when(pl.program_id(1) == 0)
    def _():
        carry_ref[...] = jnp.zeros(carry_ref.shape, F32)

    h = h_ref[0]
    hb = h.astype(BF16)
    row8 = lax.broadcasted_iota(jnp.int32, (8, FF_CHUNK), 0)

    def up_proj(c):
        cols = lambda idx: slice(idx * FF_CHUNK, (idx + 1) * FF_CHUNK)
        return _dot(hb, wup_ref[:, cols(c)]), _dot(hb, wup_ref[:, cols(N_FF + c)])

    def conv(up, idx):
        prev = carry_ref[idx]
        carry_ref[idx] = up[tm - 8:tm]
        cw = cw_ref[:, idx * FF_CHUNK:(idx + 1) * FF_CHUNK]
        out = cw[3:4] + cw[2:3] * up
        for shift in (1, 2):
            rolled = pltpu.roll(up, shift, 0)
            head = jnp.where(row8 < shift, pltpu.roll(prev, shift, 0), rolled[0:8])
            out = out + cw[2 - shift:3 - shift] * jnp.concatenate([head, rolled[8:]], axis=0)
        return out

    f = jnp.zeros((tm, D_MODEL), F32)
    ups = [up_proj(c) for c in range(FF_AHEAD)]
    for c in range(N_FF):
        g_up, v_up = ups.pop(0)
        if c + FF_AHEAD < N_FF:
            ups.append(up_proj(c + FF_AHEAD))
        gate = conv(g_up, c)
        act = (gate * _sigmoid(gate) * conv(v_up, N_FF + c)).astype(BF16)
        f = f + _dot(act, wdown_ref[c])
    out_ref[0] = _layer_norm(ALPHA * h + f, lng_ref[...], lnb_ref[...])


def _ffn_call(h, p):
    B, S, _ = h.shape
    tm = ROW_TILE
    row = pl.BlockSpec((1, tm, D_MODEL), lambda b, i: (b, i, 0))
    ws = (p["ffn_w_up"], p["ffn_cw"], p["ffn_w_down"], p["ln2_g"], p["ln2_b"])
    return pl.pallas_call(
        _ffn_kernel, grid=(B, S // tm),
        in_specs=[row] + [_full(a.shape) for a in ws],
        out_specs=row, out_shape=jax.ShapeDtypeStruct((B, S, D_MODEL), F32),
        scratch_shapes=[pltpu.VMEM((2 * N_FF, 8, FF_CHUNK), F32)],
        compiler_params=_params(2), name="conv_ffn_norm")(h, *ws)


def _t5_bucket(n):
    max_exact = NUM_BUCKETS // 2
    log_ratio = jnp.log(jnp.maximum(n, 1).astype(F32) / max_exact) / math.log(MAX_DISTANCE / max_exact)
    large = jnp.minimum(max_exact + (log_ratio * (NUM_BUCKETS - max_exact)).astype(jnp.int32), NUM_BUCKETS - 1)
    return jnp.where(n < max_exact, n, large)


def _bias_tiles(rel_bias):
    G, H = NSA_KV_GROUPS, NSA_HPG
    tab = rel_bias[_t5_bucket(jnp.arange(2 * KEY_TILE, dtype=jnp.int32))] - rel_bias[NUM_BUCKETS - 1]
    tab = tab * LOG2E

    def tile(offset, n, row_step=1):
        d = np.arange(2 * n)
        dist = offset + np.where(d < n, d, d - 2 * n)
        w = jnp.where((dist >= 0)[:, None], tab[np.clip(dist, 0, 2 * KEY_TILE - 1)], NEG_INF)
        m = jnp.broadcast_to(w[None], (n, 2 * n, w.shape[1])).reshape(2 * n * n, -1)[:n * (2 * n - 1)]
        t = m.reshape(n, 2 * n - 1, -1)[::row_step, :Q_BLOCK]
        rows = t.shape[0]
        return t.reshape(rows, Q_BLOCK, G, H).transpose(2, 0, 3, 1).reshape(G, rows, H * Q_BLOCK)

    bt0 = tile(0, KEY_TILE)
    bt1 = tile(KEY_TILE, KEY_TILE)
    bct = tile(CMP_STRIDE * CMP_PAD - CMP_BLOCK + 1, CMP_STRIDE * CMP_BAND, CMP_STRIDE)
    return bt0, bt1, bct


def _overlap_t(seq):
    n_cmp = seq // CMP_STRIDE - 1
    n_blk = seq // SEL_BLOCK
    cs = np.arange(n_cmp)[:, None] * CMP_STRIDE
    ss = np.arange(n_blk)[None, :] * SEL_BLOCK
    ov = np.maximum(np.minimum(cs + CMP_BLOCK, ss + SEL_BLOCK) - np.maximum(cs, ss), 0) / CMP_BLOCK
    out = np.zeros((LANE, CMP_ROWS), np.float32)
    out[:n_blk, CMP_PAD:CMP_PAD + n_cmp] = ov.T
    return jnp.asarray(out, BF16)


def _pad_cols(w, width):
    return jnp.pad(w, ((0, 0), (0, width - w.shape[1])))


def _layer_params(l, w_in, b_in, gm_ln_g, gm_ln_b, gm_ws, gm_bs, cmp_pos, cmp_w1, cmp_b1, cmp_w2,
                  w_proj_a, w_proj_b, w_out, ln1_g, ln1_b, ffn_w_up, ffn_conv_w, ffn_conv_b, ffn_w_down,
                  ln2_g, ln2_b):
    G = NSA_KV_GROUPS
    w, b = w_in[l], b_in[l]
    o_q = 2 * GM_WIDTH
    o_kc = o_q + NSA_WIDTH
    o_vc, o_ks, o_vs, o_kw, o_vw = (o_kc + KV_WIDTH * k for k in range(1, 6))
    o_ng = o_vw + KV_WIDTH
    n_ng = NSA_HEADS * N_GATES
    o_mg = o_ng + n_ng
    seg = lambda a, o, n: a[..., o:o + n]

    wq = jnp.pad(seg(w, o_q, NSA_WIDTH).T.reshape(NSA_HEADS, HEAD_DIM, D_MODEL),
                 ((0, 0), (0, LANE - HEAD_DIM), (0, 0))).reshape(NSA_HEADS * LANE, D_MODEL)
    bq = jnp.pad(seg(b, o_q, NSA_WIDTH).reshape(NSA_HEADS, HEAD_DIM), ((0, 0), (0, LANE - HEAD_DIM)))
    bq = bq.at[:, HEAD_DIM].set(NEG_INF / Q_SCALE).reshape(NSA_HEADS * LANE)

    def k_groups(o):
        return [_pad_cols(seg(w, o + g * HEAD_DIM, HEAD_DIM), LANE) for g in range(G)], \
               [jnp.pad(seg(b, o + g * HEAD_DIM, HEAD_DIM), (0, LANE - HEAD_DIM)) for g in range(G)]

    wks, bks = k_groups(o_ks)
    wkw, bkw = k_groups(o_kw)
    w_r = jnp.concatenate([seg(w, o_kc, KV_WIDTH), seg(w, o_vc, KV_WIDTH), *wks, *wkw,
                           _pad_cols(seg(w, o_ng, n_ng), LANE)], axis=1)
    b_r = jnp.concatenate([seg(b, o_kc, KV_WIDTH), seg(b, o_vc, KV_WIDTH), *bks, *bkw,
                           jnp.pad(seg(b, o_ng, n_ng), (0, LANE - n_ng))])

    def v_rows(o):
        wt = jnp.pad(seg(w, o, HEAD_DIM).T, ((0, V_ROWS - HEAD_DIM), (0, 0)))
        bt = jnp.pad(seg(b, o, HEAD_DIM), (0, V_ROWS - HEAD_DIM)).at[HEAD_DIM].set(1.0)
        return wt, bt

    vparts = [v_rows(o + g * HEAD_DIM) for o in (o_vs, o_vw) for g in range(G)]
    w_t = jnp.concatenate([wq] + [wt for wt, _ in vparts], axis=0)
    b_t = jnp.concatenate([bq] + [bt for _, bt in vparts])
    row_scale = jnp.where(jnp.arange(b_t.shape[0]) < NSA_HEADS * LANE, Q_SCALE, 1.0)
    b_t = jnp.stack([b_t, row_scale], axis=1)

    causal = jnp.tril(jnp.ones((GM_CHUNK, GM_CHUNK), F32))
    gw = GM_WIDTH // GM_GROUPS
    bs = gm_bs[l]
    bsb = jnp.concatenate([jnp.broadcast_to(bs[0::2, :, None], (GM_GROUPS // 2, GM_CHUNK, gw)),
                           jnp.broadcast_to(bs[1::2, :, None], (GM_GROUPS // 2, GM_CHUNK, gw))], axis=2)

    w1 = cmp_w1[l].reshape(2, CMP_BLOCK, HEAD_DIM, CMP_HIDDEN)
    w1 = jnp.stack([jnp.pad(w1, ((0, 0), (0, 0), (g * HEAD_DIM, (G - 1 - g) * HEAD_DIM), (0, 0)))
                    for g in range(G)], axis=1)
    w2 = cmp_w2[l]
    cw = jnp.concatenate([ffn_conv_w[l], ffn_conv_b[l][None], jnp.zeros((8 - CONV_WIDTH - 1, 2 * D_FF), F32)])
    return {
        "w_uv": seg(w, 0, 2 * GM_WIDTH).astype(BF16), "b_uv": seg(b, 0, 2 * GM_WIDTH)[None],
        "gm_ln_g": gm_ln_g[l][None], "gm_ln_b": gm_ln_b[l][None],
        "gm_wc": (gm_ws[l] * causal).astype(BF16), "gm_bsb": bsb,
        "w_r": w_r.astype(BF16), "b_r": b_r[None], "w_t": w_t.astype(BF16), "b_t": b_t,
        "cmp_pos": jnp.tile(cmp_pos[l], (1, 1, G)), "cmp_w1": w1.astype(BF16), "cmp_b1": cmp_b1[l][:, None],
        "cmp_w2k": _pad_cols(w2[0], LANE).astype(BF16),
        "cmp_w2vt": jnp.pad(w2[1].T, ((0, V_ROWS - HEAD_DIM), (0, 0))).astype(BF16),
        "w_mg": seg(w, o_mg, 2 * D_MODEL).astype(BF16), "b_mg": seg(b, o_mg, 2 * D_MODEL)[None],
        "w_proj_a": w_proj_a[l].astype(BF16), "w_proj_b": w_proj_b[l].astype(BF16),
        "w_out": w_out[l].astype(BF16), "ln1_g": ln1_g[l][None], "ln1_b": ln1_b[l][None],
        "ffn_w_up": ffn_w_up[l].astype(BF16), "ffn_cw": cw,
        "ffn_w_down": ffn_w_down[l].reshape(N_FF, FF_CHUNK, D_MODEL).astype(BF16),
        "ln2_g": ln2_g[l][None], "ln2_b": ln2_b[l][None],
    }


def kernel(x, w_in, b_in, gm_ln_g, gm_ln_b, gm_ws, gm_bs, cmp_pos, cmp_w1, cmp_b1, cmp_w2, rel_bias,
           w_proj_a, w_proj_b, w_out, ln1_g, ln1_b, ffn_w_up, ffn_conv_w, ffn_conv_b, ffn_w_down,
           ln2_g, ln2_b):
    B, S, _ = x.shape
    assert S % FAR_KEYS == 0 and N_SEL <= S // SEL_BLOCK <= LANE
    assert S // CMP_STRIDE - 1 + CMP_PAD < CMP_ROWS
    bt0, bt1, bct = _bias_tiles(rel_bias)
    ovt = _overlap_t(S)
    h = x
    for l in range(DEPTH):
        p = _layer_params(l, w_in, b_in, gm_ln_g, gm_ln_b, gm_ws, gm_bs, cmp_pos, cmp_w1, cmp_b1, cmp_w2,
                          w_proj_a, w_proj_b, w_out, ln1_g, ln1_b, ffn_w_up, ffn_conv_w, ffn_conv_b,
                          ffn_w_down, ln2_g, ln2_b)
        a, q, kvc, ks, kw, vst, vst2, vwt, ngt = _proj_call(h, p)
        kc, vct = _compress_call(kvc, p)
        o = _nsa_call(q, ngt, kc, vct, ovt, bct, ks, vst, vst2, kw, vwt, bt0, bt1)
        h = _merge_call(h, a, o, p)
        h = _ffn_call(h, p)
    return h
```

```python
import math

import jax
import jax.numpy as jnp
import numpy as np
from jax import lax
from jax.experimental import pallas as pl
from jax.experimental.pallas import tpu as pltpu

D_MODEL = 1024
DEPTH = 2
GM_WIDTH = D_MODEL // 2
GM_GROUPS = 8
GM_CHUNK = 128
NSA_HEADS = 8
NSA_KV_GROUPS = 2
NSA_HPG = NSA_HEADS // NSA_KV_GROUPS
HEAD_DIM = 64
NSA_WIDTH = NSA_HEADS * HEAD_DIM
KV_WIDTH = NSA_KV_GROUPS * HEAD_DIM
N_GATES = 3
CMP_BLOCK = 32
CMP_STRIDE = 16
CMP_HIDDEN = 256
SEL_BLOCK = 64
N_SEL = 16
WINDOW = 512
Q_BLOCK = 128
NUM_BUCKETS = 32
MAX_DISTANCE = 128
D_FF = 2816
CONV_WIDTH = 3
ALPHA = (2.0 * DEPTH) ** 0.25
LN_EPS = 1e-5
FORCED_SCORE = 1e6
NEG_INF = -1e30
M_INIT = -1e29
LOG2E = math.log2(math.e)
Q_SCALE = HEAD_DIM ** -0.5 * LOG2E

LANE = 128
QCOLS = NSA_HPG * Q_BLOCK
KEY_TILE = 128
FAR_TILES = 4
FAR_KEYS = FAR_TILES * KEY_TILE
FAR_AHEAD = 2
FAR_BUFS = 2
BAND_TILES = WINDOW // KEY_TILE + 1
PAD_TILES = BAND_TILES - 1
CMP_PAD = 8
CMP_ROWS = 640
CMP_BAND = 16
V_ROWS = 80
FF_CHUNK = 256
N_FF = D_FF // FF_CHUNK
FF_AHEAD = 2
ROW_TILE = 256
VMEM_LIMIT = 56 * 1024 * 1024

F32 = jnp.float32
BF16 = jnp.bfloat16
NT_DIMS = (((1,), (1,)), ((), ()))


def _dot(a, b):
    return jnp.dot(a, b, preferred_element_type=F32)


def _dot_nt(a, b):
    return lax.dot_general(a, b, NT_DIMS, preferred_element_type=F32)


def _layer_norm(x, g, b):
    mu = jnp.mean(x, axis=-1, keepdims=True)
    xc = x - mu
    var = jnp.mean(xc * xc, axis=-1, keepdims=True)
    return xc * lax.rsqrt(var + LN_EPS) * g + b


def _gelu_tanh(x):
    c = math.sqrt(2.0 / math.pi)
    return x * (0.5 * (1.0 + jnp.tanh(c * (x + 0.044715 * (x * x * x)))))


def _sigmoid(x):
    return 1.0 / (1.0 + jnp.exp(-x))


def _params(n_axes):
    return pltpu.CompilerParams(dimension_semantics=("arbitrary",) * n_axes,
                                vmem_limit_bytes=VMEM_LIMIT)


def _full(shape):
    n = len(shape)
    return pl.BlockSpec(shape, lambda *_: (0,) * n)


def _proj_kernel(h_ref, wuv_ref, buv_ref, lng_ref, lnb_ref, wc_ref, bsb_ref, wr_ref, br_ref, wt_ref, bt_ref,
                 a_ref, q_ref, kvc_ref, ks_ref, kw_ref, vst_ref, vst2_ref, vwt_ref, ngt_ref):
    tm = h_ref.shape[1]
    G = NSA_KV_GROUPS
    hb = h_ref[0].astype(BF16)

    uv = _dot(hb, wuv_ref[...]) + buv_ref[...]
    r = _dot(hb, wr_ref[...]) + br_ref[...]
    tr = ((_dot_nt(wt_ref[...], hb) + bt_ref[:, 0:1]) * bt_ref[:, 1:2]).astype(BF16)

    u = _gelu_tanh(uv[:, :GM_WIDTH])
    v = _layer_norm(_gelu_tanh(uv[:, GM_WIDTH:]), lng_ref[...], lnb_ref[...]).astype(BF16)
    lane = lax.broadcasted_iota(jnp.int32, (GM_CHUNK, LANE), 1)
    group_w = GM_WIDTH // GM_GROUPS
    for ch in range(tm // GM_CHUNK):
        rows = slice(ch * GM_CHUNK, (ch + 1) * GM_CHUNK)
        for pr in range(GM_GROUPS // 2):
            cols = slice(pr * LANE, (pr + 1) * LANE)
            x = v[rows, cols]
            sv = jnp.where(lane < group_w, _dot(wc_ref[2 * pr], x), _dot(wc_ref[2 * pr + 1], x))
            a_ref[0, rows, cols] = (u[rows, cols] * (sv + bsb_ref[pr])).astype(BF16)

    for hh in range(NSA_HEADS):
        q_ref[0, hh] = tr[hh * LANE:(hh + 1) * LANE]
    vt = tr[NSA_HEADS * LANE:]

    kvc_ref[0, 0] = r[:, 0:LANE]
    kvc_ref[0, 1] = r[:, LANE:2 * LANE]
    token = pl.program_id(1) * tm + lax.broadcasted_iota(jnp.int32, (tm, LANE), 0)
    sel_block = lax.shift_right_logical(token, int(math.log2(SEL_BLOCK)))
    onehot = (lax.broadcasted_iota(jnp.int32, (tm, LANE), 1) == sel_block).astype(BF16)
    for g in range(G):
        ks_ref[0, g, :, 0:LANE] = onehot
        ks_ref[0, g, :, LANE:2 * LANE] = r[:, (2 + g) * LANE:(3 + g) * LANE].astype(BF16)
        kw_ref[0, g] = r[:, (4 + g) * LANE:(5 + g) * LANE].astype(BF16)
    ngt_ref[0] = r[:, 6 * LANE:7 * LANE].T

    for g in range(G):
        sel_rows = slice(g * V_ROWS, (g + 1) * V_ROWS)
        win_rows = slice((G + g) * V_ROWS, (G + g + 1) * V_ROWS)
        vst2_ref[0, g, 0] = vt[sel_rows]
        for t in range(tm // KEY_TILE):
            cols = slice(t * KEY_TILE, (t + 1) * KEY_TILE)
            vst_ref[0, g, t] = vt[sel_rows, cols]
            vwt_ref[0, g, t] = vt[win_rows, cols]


def _proj_call(h, p):
    B, S, _ = h.shape
    tm = 2 * KEY_TILE
    G = NSA_KV_GROUPS
    nt = S // KEY_TILE
    row = lambda w: pl.BlockSpec((1, tm, w), lambda b, i: (b, i, 0))
    lead = lambda n, w: pl.BlockSpec((1, n, tm, w), lambda b, i: (b, 0, i, 0))
    vtile = lambda n, w: pl.BlockSpec((1, G, n, V_ROWS, w), lambda b, i: (b, 0, i, 0, 0))
    out_shape = (
        jax.ShapeDtypeStruct((B, S, GM_WIDTH), BF16),
        jax.ShapeDtypeStruct((B, NSA_HEADS, LANE, S), BF16),
        jax.ShapeDtypeStruct((B, 2, S, LANE), F32),
        jax.ShapeDtypeStruct((B, G, S, 2 * LANE), BF16),
        jax.ShapeDtypeStruct((B, G, S, LANE), BF16),
        jax.ShapeDtypeStruct((B, G, nt, V_ROWS, KEY_TILE), BF16),
        jax.ShapeDtypeStruct((B, G, nt // 2, V_ROWS, 2 * KEY_TILE), BF16),
        jax.ShapeDtypeStruct((B, G, nt, V_ROWS, KEY_TILE), BF16),
        jax.ShapeDtypeStruct((B, LANE, S), F32),
    )
    out_specs = (row(GM_WIDTH), pl.BlockSpec((1, NSA_HEADS, LANE, tm), lambda b, i: (b, 0, 0, i)),
                 lead(2, LANE), lead(G, 2 * LANE), lead(G, LANE),
                 vtile(tm // KEY_TILE, KEY_TILE), vtile(1, 2 * KEY_TILE), vtile(tm // KEY_TILE, KEY_TILE),
                 pl.BlockSpec((1, LANE, tm), lambda b, i: (b, 0, i)))
    in_arrays = (h, p["w_uv"], p["b_uv"], p["gm_ln_g"], p["gm_ln_b"], p["gm_wc"], p["gm_bsb"],
                 p["w_r"], p["b_r"], p["w_t"], p["b_t"])
    in_specs = [row(D_MODEL)] + [_full(a.shape) for a in in_arrays[1:]]
    return pl.pallas_call(
        _proj_kernel, grid=(B, S // tm), in_specs=in_specs, out_specs=out_specs,
        out_shape=out_shape, compiler_params=_params(2), name="proj_gmlp")(*in_arrays)


def _compress_kernel(x_ref, pos_ref, w1_ref, b1_ref, w2k_ref, w2vt_ref, kc_ref, vct_ref):
    kv = pl.program_id(1)
    n = x_ref.shape[2] // CMP_STRIDE
    half = CMP_BLOCK // 2
    pad_row = (lax.broadcasted_iota(jnp.int32, (1, LANE), 1) == HEAD_DIM).astype(F32)
    for g in range(NSA_KV_GROUPS):
        first = jnp.zeros((n, CMP_HIDDEN), F32)
        second = jnp.zeros((n, CMP_HIDDEN), F32)
        for r in range(half):
            xr = x_ref[0, 0, pl.ds(r, n, stride=CMP_STRIDE), :]
            first = first + _dot((xr + pos_ref[0, r:r + 1, :]).astype(BF16), w1_ref[0, g, r])
            second = second + _dot((xr + pos_ref[0, half + r:half + r + 1, :]).astype(BF16),
                                   w1_ref[0, g, half + r])
        hid = first + pltpu.roll(second, n - 1, 0) + b1_ref[0]
        act = (hid * _sigmoid(hid)).astype(BF16)

        @pl.when(kv == 0)
        def _():
            res = _dot(act, w2k_ref[...])
            row = lax.broadcasted_iota(jnp.int32, res.shape, 0)
            res = jnp.where(row == n - 1, pad_row, res)
            kc_ref[0, g] = jnp.concatenate(
                [jnp.broadcast_to(pad_row, (CMP_PAD, LANE)), res,
                 jnp.broadcast_to(pad_row, (CMP_ROWS - CMP_PAD - n, LANE))], axis=0).astype(BF16)

        @pl.when(kv == 1)
        def _():
            vt = _dot_nt(w2vt_ref[...], act)
            col = lax.broadcasted_iota(jnp.int32, vt.shape, 1)
            vt = jnp.where(col == n - 1, 0.0, vt)
            vt = jnp.concatenate([vt, jnp.zeros((V_ROWS, CMP_ROWS - n), F32)], axis=1)
            vct_ref[0, g] = pltpu.roll(vt, CMP_PAD, 1).astype(BF16)


def _compress_call(kvc, p):
    B, _, S, _ = kvc.shape
    G = NSA_KV_GROUPS
    per_kv = lambda a: pl.BlockSpec((1,) + a.shape[1:], lambda b, k: (k,) + (0,) * (a.ndim - 1))
    return pl.pallas_call(
        _compress_kernel, grid=(B, 2),
        in_specs=[pl.BlockSpec((1, 1, S, LANE), lambda b, k: (b, k, 0, 0)),
                  per_kv(p["cmp_pos"]), per_kv(p["cmp_w1"]), per_kv(p["cmp_b1"]),
                  _full(p["cmp_w2k"].shape), _full(p["cmp_w2vt"].shape)],
        out_specs=(pl.BlockSpec((1, G, CMP_ROWS, LANE), lambda b, k: (b, 0, 0, 0)),
                   pl.BlockSpec((1, G, V_ROWS, CMP_ROWS), lambda b, k: (b, 0, 0, 0))),
        out_shape=(jax.ShapeDtypeStruct((B, G, CMP_ROWS, LANE), BF16),
                   jax.ShapeDtypeStruct((B, G, V_ROWS, CMP_ROWS), BF16)),
        compiler_params=_params(2), name="kv_compress",
    )(kvc, p["cmp_pos"], p["cmp_w1"], p["cmp_b1"], p["cmp_w2k"], p["cmp_w2vt"])


def _nsa_kernel(q_ref, ngt_ref, kc_ref, vct_ref, ovt_ref, bct_ref, ks_ref, vst_ref, vst2_ref, kw_ref, vwt_ref,
                 bt0_ref, bt1_ref, o_ref, sc_ref, rhst_ref, m_ref, acc_ref, negf_ref, *ring_refs):
    G = NSA_KV_GROUPS
    groups = range(G)
    sbuf_refs = [ring_refs[g * FAR_BUFS:(g + 1) * FAR_BUFS] for g in groups]
    mbuf_refs = [ring_refs[(G + g) * FAR_BUFS:(G + g + 1) * FAR_BUFS] for g in groups]
    i = pl.program_id(1)
    n_blk = LANE
    blk = lax.broadcasted_iota(jnp.int32, (n_blk, Q_BLOCK), 0)
    qi = lax.broadcasted_iota(jnp.int32, (n_blk, Q_BLOCK), 1)
    eye = (blk == qi).astype(BF16)
    qt = [jnp.concatenate([q_ref[0, g * NSA_HPG + h] for h in range(NSA_HPG)], axis=1) for g in groups]
    bt0 = [bt0_ref[g] for g in groups]
    bt1 = [bt1_ref[g] for g in groups]

    def band_logits(k_ref, g, rhs, extra, flag_col):
        width = k_ref.shape[3]
        before_start = (lax.broadcasted_iota(jnp.int32, (KEY_TILE, width), 1) == flag_col).astype(BF16)
        tiles = []
        for t in range(BAND_TILES):
            tau = i - PAD_TILES + t
            row0 = pl.multiple_of(jnp.maximum(tau, 0) * KEY_TILE, KEY_TILE)
            k = k_ref[0, g, pl.ds(row0, KEY_TILE), :]
            if t < PAD_TILES:
                k = jnp.where(jnp.broadcast_to(tau, k.shape) >= 0, k, before_start)
            s = _dot(k, rhs)
            tiles.append(s if extra[t] is None else s + extra[t])
        return tiles

    def band_softmax(tiles, vt_ref, g, m_old, acc_old):
        m_new = m_old
        for s in tiles:
            m_new = jnp.maximum(m_new, jnp.max(s, axis=0, keepdims=True))
        acc = jnp.exp2(m_old - m_new) * acc_old
        for t, s in enumerate(tiles):
            vt = vt_ref[0, g, jnp.maximum(i - PAD_TILES + t, 0)]
            acc = acc + _dot(vt, jnp.exp2(s - m_new).astype(BF16))
        return acc[0:HEAD_DIM] / acc[HEAD_DIM:HEAD_DIM + 1]

    band = pl.multiple_of(8 * i, 8)
    visible = 8 * i + CMP_BAND

    def compressed_stage(rows):
        def softmax(g):
            last = sc_ref[g, rows - LANE:rows]
            key = rows - LANE + lax.broadcasted_iota(jnp.int32, last.shape, 0)
            last = jnp.where(key >= visible, NEG_INF, last)
            s = jnp.concatenate([sc_ref[g, 0:rows - LANE], last], axis=0) if rows > LANE else last
            m = jnp.maximum(jnp.max(s, axis=0, keepdims=True), M_INIT)
            p = jnp.exp2(s - m)
            l = jnp.sum(p, axis=0, keepdims=True)
            r = jnp.where(l > 0.0, 1.0 / l, 0.0)
            oc = _dot(vct_ref[0, g, :, 0:rows], p.astype(BF16))[0:HEAD_DIM] * r
            pn = p * r
            psum = pn[:, 0:Q_BLOCK]
            for h in range(1, NSA_HPG):
                psum = psum + pn[:, h * Q_BLOCK:(h + 1) * Q_BLOCK]
            p_hi = psum.astype(BF16)
            p_lo = (psum - p_hi.astype(F32)).astype(BF16)
            ovt = ovt_ref[:, 0:rows]
            return oc, _dot(ovt, p_hi) + _dot(ovt, p_lo)

        def run():
            for g in groups:
                sc_ref[g, 0:rows] = _dot(kc_ref[0, g, 0:rows], qt[g])
                sc_ref[g, pl.ds(band, CMP_BAND), :] += bct_ref[g]
            for u in range(FAR_AHEAD):
                for g in groups:
                    bufs[g][u][0][...], bufs[g][u][1][...] = far_logits(g, u)
            return tuple(x for g in groups for x in softmax(g))
        return run

    n_far = jnp.maximum(i - 1, 0) // FAR_TILES
    last_far = jnp.maximum(n_far - 1, 0)
    blocks_per_step = FAR_KEYS // SEL_BLOCK
    bufs = [tuple(zip(sbuf_refs[g], mbuf_refs[g])) for g in groups]

    def far_logits(g, c):
        r0 = pl.multiple_of(FAR_KEYS * jnp.minimum(c, last_far), FAR_KEYS)
        s = _dot(ks_ref[0, g, pl.ds(r0, FAR_KEYS), LANE:2 * LANE], qt[g])
        return s, jnp.max(s.reshape(blocks_per_step, SEL_BLOCK, QCOLS), axis=1)

    row_options = tuple(range(LANE, CMP_ROWS + 1, LANE))
    flat = lax.switch((visible + LANE - 1) // LANE - 1, [compressed_stage(rows) for rows in row_options])
    oc, imp = flat[0::2], flat[1::2]

    cur = (Q_BLOCK // SEL_BLOCK) * i + (qi >= SEL_BLOCK).astype(jnp.int32)
    forced = (blk == 0) | (blk == cur) | (blk == cur - 1)
    taken = -2.0 ** 126
    blk_f = blk.astype(F32)
    start = tuple(jnp.where(forced, taken, jnp.where(blk > cur, NEG_INF, imp[g])) for g in groups)

    def select_stage(rows):
        def pick(_, vals):
            out = []
            for val in vals:
                mx = jnp.max(val, axis=0, keepdims=True)
                first = jnp.min(jnp.where(val == mx, blk_f[0:rows], 1e9), axis=0, keepdims=True)
                out.append(jnp.where(blk_f[0:rows] == first, taken, val))
            return tuple(out)

        def run():
            vals = lax.fori_loop(0, N_SEL - 3, pick, tuple(v[0:rows] for v in start))
            sels = [(val < 2 * NEG_INF).astype(F32) for val in vals]
            return tuple(jnp.concatenate([sel, jnp.zeros((n_blk - rows, Q_BLOCK), F32)], axis=0)
                         if rows < n_blk else sel for sel in sels)
        return run

    row_step = n_blk // 4
    row_options = tuple(range(row_step, n_blk + 1, row_step))
    n_cand = (Q_BLOCK // SEL_BLOCK) * (i + 1)
    picked = lax.switch((n_cand + row_step - 1) // row_step - 1, [select_stage(rows) for rows in row_options])

    per_head = lambda x: jnp.concatenate([x] * NSA_HPG, axis=1)
    for g in groups:
        neg_far = jnp.where((picked[g] > 0.5) & (blk <= cur), 0.0, NEG_INF)
        neg_tail = jnp.where(blk >= n_far * blocks_per_step, neg_far, NEG_INF)
        negf_ref[g] = per_head(neg_far)
        rhst_ref[g, 0:LANE] = per_head(neg_tail.astype(BF16))
        rhst_ref[g, LANE:2 * LANE] = qt[g]
        m_ref[g] = jnp.full(m_ref.shape[1:], M_INIT, F32)
        acc_ref[g] = jnp.zeros(acc_ref.shape[1:], F32)

    def far_softmax(g, c, s_ref, bmax_ref):
        step = jnp.minimum(c, last_far)
        mask = negf_ref[g, pl.ds(pl.multiple_of(blocks_per_step * step, blocks_per_step), blocks_per_step), :]
        mask = jnp.where(c >= n_far, NEG_INF, mask)
        m_old = m_ref[g]
        m_new = jnp.maximum(m_old, jnp.max(bmax_ref[...] + mask, axis=0, keepdims=True))
        shift = mask - m_new
        p = jnp.concatenate([jnp.exp2(s_ref[b * SEL_BLOCK:(b + 1) * SEL_BLOCK] + shift[b:b + 1])
                             for b in range(blocks_per_step)], axis=0).astype(BF16)
        pv = jnp.zeros(acc_ref.shape[1:], F32)
        for t in range(FAR_TILES // 2):
            pv = pv + _dot(vst2_ref[0, g, (FAR_TILES // 2) * step + t], p[2 * t * KEY_TILE:2 * (t + 1) * KEY_TILE])
        acc_ref[g] = jnp.exp2(m_old - m_new) * acc_ref[g] + pv
        m_ref[g] = m_new

    kk = lax.broadcasted_iota(jnp.int32, (KEY_TILE, QCOLS), 0)
    qq = lax.broadcasted_iota(jnp.int32, (KEY_TILE, QCOLS), 1) & (Q_BLOCK - 1)
    edge = jnp.where(kk > qq, 0.0, NEG_INF)
    win_tiles = [band_logits(kw_ref, g, qt[g], [edge] + [None] * (BAND_TILES - 3) + [bt1[g], bt0[g]], HEAD_DIM)
                 for g in groups]
    ow = [band_softmax(win_tiles[g], vwt_ref, g, jnp.full(m_ref.shape[1:], M_INIT, F32),
                       jnp.zeros(acc_ref.shape[1:], F32)) for g in groups]

    n_trips = jnp.maximum(n_far - FAR_AHEAD + FAR_BUFS - 1, 0) // FAR_BUFS

    def far_ring(j, carry):
        for u in range(FAR_BUFS):
            c = FAR_BUFS * j + u
            ahead = [far_logits(g, c + FAR_AHEAD) for g in groups]
            for g in groups:
                far_softmax(g, c, *bufs[g][u])
            for g in groups:
                ahead_s, ahead_mx = bufs[g][(u + FAR_AHEAD) % FAR_BUFS]
                ahead_s[...], ahead_mx[...] = ahead[g]
        return carry

    def far_ring_pair(j, carry):
        return far_ring(2 * j + 1, far_ring(2 * j, carry))

    lax.fori_loop(0, n_trips // 2, far_ring_pair, 0)
    lax.fori_loop(2 * (n_trips // 2), n_trips, far_ring, 0)

    tail_tiles = [band_logits(ks_ref, g, rhst_ref[g], [None] * (BAND_TILES - 2) + [bt1[g], bt0[g]],
                              LANE + HEAD_DIM) for g in groups]
    for u in range(FAR_AHEAD):
        for g in groups:
            far_softmax(g, FAR_BUFS * n_trips + u, *bufs[g][u])
    os_ = [band_softmax(tail_tiles[g], vst_ref, g, m_ref[g], acc_ref[g]) for g in groups]

    for g in groups:
        def gate(branch):
            rows = [ngt_ref[0, NSA_HPG * N_GATES * g + N_GATES * h + branch:
                            NSA_HPG * N_GATES * g + N_GATES * h + branch + 1, :] for h in range(NSA_HPG)]
            return _sigmoid(jnp.concatenate(rows, axis=1))

        o = (gate(0) * oc[g] + gate(1) * os_[g] + gate(2) * ow[g]).astype(BF16)
        heads = jnp.concatenate([o[:, h * Q_BLOCK:(h + 1) * Q_BLOCK] for h in range(NSA_HPG)], axis=0)
        o_ref[0, :, g * NSA_HPG * HEAD_DIM:(g + 1) * NSA_HPG * HEAD_DIM] = _dot_nt(eye, heads).astype(BF16)


def _nsa_call(q, ngt, kc, vct, ovt, bct, ks, vst, vst2, kw, vwt, bt0, bt1):
    B, _, _, S = q.shape
    G = NSA_KV_GROUPS
    nq = S // Q_BLOCK
    per_b = lambda a: pl.BlockSpec((1,) + a.shape[1:], lambda b, i: (b,) + (0,) * (a.ndim - 1),
                                   pipeline_mode=pl.Buffered(1))
    in_specs = [
        pl.BlockSpec((1, NSA_HEADS, LANE, Q_BLOCK), lambda b, i: (b, 0, 0, i)),
        pl.BlockSpec((1, LANE, Q_BLOCK), lambda b, i: (b, 0, i)),
        per_b(kc), per_b(vct), _full(ovt.shape), _full(bct.shape),
        per_b(ks), per_b(vst), per_b(vst2), per_b(kw), per_b(vwt), _full(bt0.shape), _full(bt1.shape),
    ]
    scratch = [
        pltpu.VMEM((G, CMP_ROWS, QCOLS), F32),
        pltpu.VMEM((G, 2 * LANE, QCOLS), BF16),
        pltpu.VMEM((G, 1, QCOLS), F32),
        pltpu.VMEM((G, V_ROWS, QCOLS), F32),
        pltpu.VMEM((G, LANE, QCOLS), F32),
    ]
    scratch += [pltpu.VMEM((FAR_KEYS, QCOLS), F32)] * (G * FAR_BUFS)
    scratch += [pltpu.VMEM((FAR_KEYS // SEL_BLOCK, QCOLS), F32)] * (G * FAR_BUFS)
    return pl.pallas_call(
        _nsa_kernel, grid=(B, nq), in_specs=in_specs,
        out_specs=pl.BlockSpec((1, Q_BLOCK, NSA_WIDTH), lambda b, i: (b, i, 0)),
        out_shape=jax.ShapeDtypeStruct((B, S, NSA_WIDTH), BF16),
        scratch_shapes=scratch, compiler_params=_params(2), name="sparse_attention",
    )(q, ngt, kc, vct, ovt, bct, ks, vst, vst2, kw, vwt, bt0, bt1)


def _merge_kernel(h_ref, a_ref, o_ref, wmg_ref, bmg_ref, wa_ref, wb_ref, wout_ref, lng_ref, lnb_ref, out_ref):
    tm = h_ref.shape[1]
    parts = [slice(k * ROW_TILE, (k + 1) * ROW_TILE) for k in range(tm // ROW_TILE)]
    pre = []
    for rows in parts:
        mg = _dot(h_ref[0, rows].astype(BF16), wmg_ref[...]) + bmg_ref[...]
        pre.append((mg, _dot(a_ref[0, rows], wa_ref[...]), _dot(o_ref[0, rows], wb_ref[...])))
    ms = []
    for mg, pa, pb in pre:
        y = _sigmoid(mg[:, :D_MODEL]) * pa + _sigmoid(mg[:, D_MODEL:]) * pb
        ms.append(_dot(y.astype(BF16), wout_ref[...]))
    for rows, m in zip(parts, ms):
        out_ref[0, rows] = _layer_norm(ALPHA * h_ref[0, rows] + m, lng_ref[...], lnb_ref[...])


def _merge_call(h, a, o, p):
    B, S, _ = h.shape
    tm = 4 * ROW_TILE
    row = lambda w: pl.BlockSpec((1, tm, w), lambda b, i: (b, i, 0))
    ws = (p["w_mg"], p["b_mg"], p["w_proj_a"], p["w_proj_b"], p["w_out"], p["ln1_g"], p["ln1_b"])
    return pl.pallas_call(
        _merge_kernel, grid=(B, S // tm),
        in_specs=[row(D_MODEL), row(GM_WIDTH), row(NSA_WIDTH)] + [_full(a.shape) for a in ws],
        out_specs=row(D_MODEL), out_shape=jax.ShapeDtypeStruct((B, S, D_MODEL), F32),
        compiler_params=_params(2), name="merge_norm")(h, a, o, *ws)


def _ffn_kernel(h_ref, wup_ref, cw_ref, wdown_ref, lng_ref, lnb_ref, out_ref, carry_ref):
    tm = h_ref.shape[1]

    @pl.when(pl.program_id(1) == 0)
    def _():
        carry_ref[...] = jnp.zeros(carry_ref.shape, F32)

    h = h_ref[0]
    hb = h.astype(BF16)
    row8 = lax.broadcasted_iota(jnp.int32, (8, FF_CHUNK), 0)

    def up_proj(c):
        cols = lambda idx: slice(idx * FF_CHUNK, (idx + 1) * FF_CHUNK)
        return _dot(hb, wup_ref[:, cols(c)]), _dot(hb, wup_ref[:, cols(N_FF + c)])

    def conv(up, idx):
        prev = carry_ref[idx]
        carry_ref[idx] = up[tm - 8:tm]
        cw = cw_ref[:, idx * FF_CHUNK:(idx + 1) * FF_CHUNK]
        out = cw[3:4] + cw[2:3] * up
        for shift in (1, 2):
            rolled = pltpu.roll(up, shift, 0)
            head = jnp.where(row8 < shift, pltpu.roll(prev, shift, 0), rolled[0:8])
            out = out + cw[2 - shift:3 - shift] * jnp.concatenate([head, rolled[8:]], axis=0)
        return out

    f = jnp.zeros((tm, D_MODEL), F32)
    ups = [up_proj(c) for c in range(FF_AHEAD)]
    for c in range(N_FF):
        g_up, v_up = ups.pop(0)
        if c + FF_AHEAD < N_FF:
            ups.append(up_proj(c + FF_AHEAD))
        gate = conv(g_up, c)
        act = (gate * _sigmoid(gate) * conv(v_up, N_FF + c)).astype(BF16)
        f = f + _dot(act, wdown_ref[c])
    out_ref[0] = _layer_norm(ALPHA * h + f, lng_ref[...], lnb_ref[...])


def _ffn_call(h, p):
    B, S, _ = h.shape
    tm = ROW_TILE
    row = pl.BlockSpec((1, tm, D_MODEL), lambda b, i: (b, i, 0))
    ws = (p["ffn_w_up"], p["ffn_cw"], p["ffn_w_down"], p["ln2_g"], p["ln2_b"])
    return pl.pallas_call(
        _ffn_kernel, grid=(B, S // tm),
        in_specs=[row] + [_full(a.shape) for a in ws],
        out_specs=row, out_shape=jax.ShapeDtypeStruct((B, S, D_MODEL), F32),
        scratch_shapes=[pltpu.VMEM((2 * N_FF, 8, FF_CHUNK), F32)],
        compiler_params=_params(2), name="conv_ffn_norm")(h, *ws)


def _t5_bucket(n):
    max_exact = NUM_BUCKETS // 2
    log_ratio = jnp.log(jnp.maximum(n, 1).astype(F32) / max_exact) / math.log(MAX_DISTANCE / max_exact)
    large = jnp.minimum(max_exact + (log_ratio * (NUM_BUCKETS - max_exact)).astype(jnp.int32), NUM_BUCKETS - 1)
    return jnp.where(n < max_exact, n, large)


def _bias_tiles(rel_bias):
    G, H = NSA_KV_GROUPS, NSA_HPG
    tab = rel_bias[_t5_bucket(jnp.arange(2 * KEY_TILE, dtype=jnp.int32))] - rel_bias[NUM_BUCKETS - 1]
    tab = tab * LOG2E

    def tile(offset, n, row_step=1):
        d = np.arange(2 * n)
        dist = offset + np.where(d < n, d, d - 2 * n)
        w = jnp.where((dist >= 0)[:, None], tab[np.clip(dist, 0, 2 * KEY_TILE - 1)], NEG_INF)
        m = jnp.broadcast_to(w[None], (n, 2 * n, w.shape[1])).reshape(2 * n * n, -1)[:n * (2 * n - 1)]
        t = m.reshape(n, 2 * n - 1, -1)[::row_step, :Q_BLOCK]
        rows = t.shape[0]
        return t.reshape(rows, Q_BLOCK, G, H).transpose(2, 0, 3, 1).reshape(G, rows, H * Q_BLOCK)

    bt0 = tile(0, KEY_TILE)
    bt1 = tile(KEY_TILE, KEY_TILE)
    bct = tile(CMP_STRIDE * CMP_PAD - CMP_BLOCK + 1, CMP_STRIDE * CMP_BAND, CMP_STRIDE)
    return bt0, bt1, bct


def _overlap_t(seq):
    n_cmp = seq // CMP_STRIDE - 1
    n_blk = seq // SEL_BLOCK
    cs = np.arange(n_cmp)[:, None] * CMP_STRIDE
    ss = np.arange(n_blk)[None, :] * SEL_BLOCK
    ov = np.maximum(np.minimum(cs + CMP_BLOCK, ss + SEL_BLOCK) - np.maximum(cs, ss), 0) / CMP_BLOCK
    out = np.zeros((LANE, CMP_ROWS), np.float32)
    out[:n_blk, CMP_PAD:CMP_PAD + n_cmp] = ov.T
    return jnp.asarray(out, BF16)


def _pad_cols(w, width):
    return jnp.pad(w, ((0, 0), (0, width - w.shape[1])))


def _layer_params(l, w_in, b_in, gm_ln_g, gm_ln_b, gm_ws, gm_bs, cmp_pos, cmp_w1, cmp_b1, cmp_w2,
                  w_proj_a, w_proj_b, w_out, ln1_g, ln1_b, ffn_w_up, ffn_conv_w, ffn_conv_b, ffn_w_down,
                  ln2_g, ln2_b):
    G = NSA_KV_GROUPS
    w, b = w_in[l], b_in[l]
    o_q = 2 * GM_WIDTH
    o_kc = o_q + NSA_WIDTH
    o_vc, o_ks, o_vs, o_kw, o_vw = (o_kc + KV_WIDTH * k for k in range(1, 6))
    o_ng = o_vw + KV_WIDTH
    n_ng = NSA_HEADS * N_GATES
    o_mg = o_ng + n_ng
    seg = lambda a, o, n: a[..., o:o + n]

    wq = jnp.pad(seg(w, o_q, NSA_WIDTH).T.reshape(NSA_HEADS, HEAD_DIM, D_MODEL),
                 ((0, 0), (0, LANE - HEAD_DIM), (0, 0))).reshape(NSA_HEADS * LANE, D_MODEL)
    bq = jnp.pad(seg(b, o_q, NSA_WIDTH).reshape(NSA_HEADS, HEAD_DIM), ((0, 0), (0, LANE - HEAD_DIM)))
    bq = bq.at[:, HEAD_DIM].set(NEG_INF / Q_SCALE).reshape(NSA_HEADS * LANE)

    def k_groups(o):
        return [_pad_cols(seg(w, o + g * HEAD_DIM, HEAD_DIM), LANE) for g in range(G)], \
               [jnp.pad(seg(b, o + g * HEAD_DIM, HEAD_DIM), (0, LANE - HEAD_DIM)) for g in range(G)]

    wks, bks = k_groups(o_ks)
    wkw, bkw = k_groups(o_kw)
    w_r = jnp.concatenate([seg(w, o_kc, KV_WIDTH), seg(w, o_vc, KV_WIDTH), *wks, *wkw,
                           _pad_cols(seg(w, o_ng, n_ng), LANE)], axis=1)
    b_r = jnp.concatenate([seg(b, o_kc, KV_WIDTH), seg(b, o_vc, KV_WIDTH), *bks, *bkw,
                           jnp.pad(seg(b, o_ng, n_ng), (0, LANE - n_ng))])

    def v_rows(o):
        wt = jnp.pad(seg(w, o, HEAD_DIM).T, ((0, V_ROWS - HEAD_DIM), (0, 0)))
        bt = jnp.pad(seg(b, o, HEAD_DIM), (0, V_ROWS - HEAD_DIM)).at[HEAD_DIM].set(1.0)
        return wt, bt

    vparts = [v_rows(o + g * HEAD_DIM) for o in (o_vs, o_vw) for g in range(G)]
    w_t = jnp.concatenate([wq] + [wt for wt, _ in vparts], axis=0)
    b_t = jnp.concatenate([bq] + [bt for _, bt in vparts])
    row_scale = jnp.where(jnp.arange(b_t.shape[0]) < NSA_HEADS * LANE, Q_SCALE, 1.0)
    b_t = jnp.stack([b_t, row_scale], axis=1)

    causal = jnp.tril(jnp.ones((GM_CHUNK, GM_CHUNK), F32))
    gw = GM_WIDTH // GM_GROUPS
    bs = gm_bs[l]
    bsb = jnp.concatenate([jnp.broadcast_to(bs[0::2, :, None], (GM_GROUPS // 2, GM_CHUNK, gw)),
                           jnp.broadcast_to(bs[1::2, :, None], (GM_GROUPS // 2, GM_CHUNK, gw))], axis=2)

    w1 = cmp_w1[l].reshape(2, CMP_BLOCK, HEAD_DIM, CMP_HIDDEN)
    w1 = jnp.stack([jnp.pad(w1, ((0, 0), (0, 0), (g * HEAD_DIM, (G - 1 - g) * HEAD_DIM), (0, 0)))
                    for g in range(G)], axis=1)
    w2 = cmp_w2[l]
    cw = jnp.concatenate([ffn_conv_w[l], ffn_conv_b[l][None], jnp.zeros((8 - CONV_WIDTH - 1, 2 * D_FF), F32)])
    return {
        "w_uv": seg(w, 0, 2 * GM_WIDTH).astype(BF16), "b_uv": seg(b, 0, 2 * GM_WIDTH)[None],
        "gm_ln_g": gm_ln_g[l][None], "gm_ln_b": gm_ln_b[l][None],
        "gm_wc": (gm_ws[l] * causal).astype(BF16), "gm_bsb": bsb,
        "w_r": w_r.astype(BF16), "b_r": b_r[None], "w_t": w_t.astype(BF16), "b_t": b_t,
        "cmp_pos": jnp.tile(cmp_pos[l], (1, 1, G)), "cmp_w1": w1.astype(BF16), "cmp_b1": cmp_b1[l][:, None],
        "cmp_w2k": _pad_cols(w2[0], LANE).astype(BF16),
        "cmp_w2vt": jnp.pad(w2[1].T, ((0, V_ROWS - HEAD_DIM), (0, 0))).astype(BF16),
        "w_mg": seg(w, o_mg, 2 * D_MODEL).astype(BF16), "b_mg": seg(b, o_mg, 2 * D_MODEL)[None],
        "w_proj_a": w_proj_a[l].astype(BF16), "w_proj_b": w_proj_b[l].astype(BF16),
        "w_out": w_out[l].astype(BF16), "ln1_g": ln1_g[l][None], "ln1_b": ln1_b[l][None],
        "ffn_w_up": ffn_w_up[l].astype(BF16), "ffn_cw": cw,
        "ffn_w_down": ffn_w_down[l].reshape(N_FF, FF_CHUNK, D_MODEL).astype(BF16),
        "ln2_g": ln2_g[l][None], "ln2_b": ln2_b[l][None],
    }


def kernel(x, w_in, b_in, gm_ln_g, gm_ln_b, gm_ws, gm_bs, cmp_pos, cmp_w1, cmp_b1, cmp_w2, rel_bias,
           w_proj_a, w_proj_b, w_out, ln1_g, ln1_b, ffn_w_up, ffn_conv_w, ffn_conv_b, ffn_w_down,
           ln2_g, ln2_b):
    B, S, _ = x.shape
    assert S % FAR_KEYS == 0 and N_SEL <= S // SEL_BLOCK <= LANE
    assert S // CMP_STRIDE - 1 + CMP_PAD < CMP_ROWS
    bt0, bt1, bct = _bias_tiles(rel_bias)
    ovt = _overlap_t(S)
    h = x
    for l in range(DEPTH):
        p = _layer_params(l, w_in, b_in, gm_ln_g, gm_ln_b, gm_ws, gm_bs, cmp_pos, cmp_w1, cmp_b1, cmp_w2,
                          w_proj_a, w_proj_b, w_out, ln1_g, ln1_b, ffn_w_up, ffn_conv_w, ffn_conv_b,
                          ffn_w_down, ln2_g, ln2_b)
        a, q, kvc, ks, kw, vst, vst2, vwt, ngt = _proj_call(h, p)
        kc, vct = _compress_call(kvc, p)
        o = _nsa_call(q, ngt, kc, vct, ovt, bct, ks, vst, vst2, kw, vwt, bt0, bt1)
        h = _merge_call(h, a, o, p)
        h = _ffn_call(h, p)
    return h
```
